```python
import jax, jax.numpy as jnp
from jax import lax
import numpy as np

D_MODEL = 2048
BATCH = 2
SEQ = 8192
DEPTH = 4

SGU_WIDTH = 1024
SGU_GROUPS = 8
SGU_GROUP_CH = SGU_WIDTH // SGU_GROUPS
CHUNK = 128
HEAD_DIM = 128
HEADS_PER_GROUP = 4
DILATED_GROUPS = ((128, 1), (512, 4), (2048, 16))
ATT_HEADS = HEADS_PER_GROUP * len(DILATED_GROUPS)
ATT_WIDTH = ATT_HEADS * HEAD_DIM
MERGED_ATT_WIDTH = HEADS_PER_GROUP * HEAD_DIM
QBLK = 128
ROT_DIM = HEAD_DIM // 4
ROPE_THETA = 500000.0
PROJ_SIZES = (SGU_WIDTH, SGU_WIDTH, ATT_WIDTH, ATT_WIDTH, ATT_WIDTH, D_MODEL, D_MODEL)
PROJ_WIDTH = sum(PROJ_SIZES)
PROJ_SPLITS = tuple(int(s) for s in np.cumsum(PROJ_SIZES)[:-1])
N_EXPERT_GROUPS = 4
EXPERTS_PER_GROUP = 8
N_EXPERTS = N_EXPERT_GROUPS * EXPERTS_PER_GROUP
TOP_K = 2
EXPERT_FF = 512
MOE_BLK = 256
PLE_DIM = 256
ALPHA = (2 * DEPTH) ** 0.25
BETA = (8 * DEPTH) ** -0.25
LN_EPS = 1e-5

kernel_name = "hybrid_sgu_dilated_attn_hmoe_deepnorm"


def layer_norm(x, g, b):
    xf = x.astype(jnp.float32)
    mu = xf.mean(-1, keepdims=True)
    xc = xf - mu
    var = jnp.mean(xc * xc, -1, keepdims=True)
    return (xc * lax.rsqrt(var + LN_EPS) * g + b).astype(x.dtype)


def rotary(x, cos, sin):
    half = ROT_DIM // 2
    xf = x.astype(jnp.float32)
    x1, x2 = xf[..., :half], xf[..., half:ROT_DIM]
    out = jnp.concatenate([x1 * cos - x2 * sin, x2 * cos + x1 * sin, xf[..., ROT_DIM:]], axis=-1)
    return out.astype(x.dtype)


def chunked_sgu(u, v, ln_g, ln_b, w_s, b_s):
    B, S, _ = u.shape
    u = jax.nn.gelu(u)
    v = layer_norm(jax.nn.gelu(v), ln_g, ln_b)
    vb = v.reshape(B, S // CHUNK, CHUNK, SGU_GROUPS, SGU_GROUP_CH)
    causal = jnp.tril(jnp.ones((CHUNK, CHUNK), dtype=w_s.dtype))
    ws = (w_s * causal).astype(v.dtype)
    z = jnp.einsum('gts,bnsgc->bntgc', ws, vb) + b_s.T.astype(v.dtype)[None, None, :, :, None]
    return u * z.reshape(B, S, SGU_WIDTH)


def dilated_window_attention(q, k, v, window, dilation):
    B, S, H, Dh = q.shape
    span = window // dilation
    L = S // dilation
    nb = -(-L // QBLK)
    Lp = nb * QBLK

    def to_blocks(a):
        a = a.reshape(B, L, dilation, H, Dh).transpose(0, 2, 1, 3, 4)
        a = jnp.pad(a, ((0, 0), (0, 0), (0, Lp - L), (0, 0), (0, 0)))
        return a.reshape(B, dilation, nb, QBLK, H, Dh)

    def band(a):
        prev = jnp.concatenate([jnp.zeros_like(a[:, :, :1]), a[:, :, :-1]], axis=2)
        return jnp.concatenate([prev, a], axis=3)

    qb, kb, vb = to_blocks(q), to_blocks(k), to_blocks(v)
    kw, vw = band(kb), band(vb)
    s = jnp.einsum('bcnqhd,bcnkhd->bcnhqk', qb, kw).astype(jnp.float32) * (Dh ** -0.5)
    qi = jnp.arange(QBLK)[:, None]
    kj = jnp.arange(2 * QBLK)[None, :]
    dist = qi + QBLK - kj
    blk = jnp.arange(nb)[:, None, None]
    valid = (dist >= 0) & (dist <= span) & ((blk - 1) * QBLK + kj >= 0)
    s = jnp.where(valid[None, None, :, None], s, -jnp.inf)
    m = s.max(-1, keepdims=True)
    e = jnp.exp(s - m)
    den = e.sum(-1, keepdims=True)
    o = jnp.einsum('bcnhqk,bcnkhd->bcnqhd', (e / den).astype(v.dtype), vw).astype(jnp.float32)
    lse = (m + jnp.log(den))[..., 0].transpose(0, 1, 2, 4, 3)
    o = o.reshape(B, dilation, Lp, H, Dh)[:, :, :L].transpose(0, 2, 1, 3, 4).reshape(B, S, H, Dh)
    lse = lse.reshape(B, dilation, Lp, H)[:, :, :L].transpose(0, 2, 1, 3).reshape(B, S, H)
    return o, lse


def dilated_mixture(q, k, v):
    B, S = q.shape[:2]
    outs, lses = [], []
    for g, (window, dilation) in enumerate(DILATED_GROUPS):
        sl = slice(g * HEADS_PER_GROUP, (g + 1) * HEADS_PER_GROUP)
        o, l = dilated_window_attention(q[:, :, sl], k[:, :, sl], v[:, :, sl], window, dilation)
        outs.append(o)
        lses.append(l)
    o = jnp.stack(outs, 0)
    w = jax.nn.softmax(jnp.stack(lses, 0), axis=0)
    merged = jnp.sum(w[..., None] * o, axis=0)
    return merged.reshape(B, S, MERGED_ATT_WIDTH).astype(q.dtype)


def hierarchical_moe(h, w_grp, b_grp, w_rt, b_rt, w1, w3, w2):
    B, S, D = h.shape
    N = B * S
    hf = h.reshape(N, D)
    grp_logits = (hf @ w_grp.astype(h.dtype)).astype(jnp.float32) + b_grp
    grp_prob = jax.nn.softmax(grp_logits, axis=-1)
    g_sel = jnp.argmax(grp_logits, axis=-1).astype(jnp.int32)
    p_g = jnp.take_along_axis(grp_prob, g_sel[:, None], axis=1)[:, 0]
    exp_logits = ((hf @ w_rt.astype(h.dtype)).astype(jnp.float32) + b_rt).reshape(N, N_EXPERT_GROUPS, EXPERTS_PER_GROUP)
    in_grp = jnp.take_along_axis(exp_logits, g_sel[:, None, None], axis=1)[:, 0]
    top_v, top_i = lax.top_k(in_grp, TOP_K)
    gate = jax.nn.softmax(top_v, axis=-1) * p_g[:, None]
    e_id = g_sel[:, None] * EXPERTS_PER_GROUP + top_i.astype(jnp.int32)

    A = N * TOP_K
    e_flat = e_id.reshape(A)
    tok = jnp.repeat(jnp.arange(N, dtype=jnp.int32), TOP_K)
    w_flat = gate.reshape(A)
    order = jnp.argsort(e_flat)
    e_s, tok_s, w_s = e_flat[order], tok[order], w_flat[order]
    counts = jnp.zeros((N_EXPERTS,), jnp.int32).at[e_flat].add(1)
    start = jnp.cumsum(counts) - counts
    padded = ((counts + MOE_BLK - 1) // MOE_BLK) * MOE_BLK
    pend = jnp.cumsum(padded)
    pstart = pend - padded
    dest = pstart[e_s] + (jnp.arange(A, dtype=jnp.int32) - start[e_s])
    n_blocks = (A + N_EXPERTS * (MOE_BLK - 1) + MOE_BLK - 1) // MOE_BLK
    P = n_blocks * MOE_BLK
    buf_tok = jnp.full((P,), N, jnp.int32).at[dest].set(tok_s)
    buf_w = jnp.zeros((P,), jnp.float32).at[dest].set(w_s)
    blk_e = jnp.minimum(jnp.searchsorted(pend, jnp.arange(n_blocks, dtype=jnp.int32) * MOE_BLK, side='right'), N_EXPERTS - 1).astype(jnp.int32)
    xpad = jnp.concatenate([hf, jnp.zeros((1, D), hf.dtype)], axis=0)
    xb = xpad[buf_tok].reshape(n_blocks, MOE_BLK, D)

    def expert_block(args):
        xblk, e = args
        hid = jax.nn.silu(xblk @ w1[e].astype(xblk.dtype)) * (xblk @ w3[e].astype(xblk.dtype))
        return hid @ w2[e].astype(xblk.dtype)

    yb = lax.map(expert_block, (xb, blk_e)).reshape(P, D)
    y = jax.ops.segment_sum(yb * buf_w.astype(yb.dtype)[:, None], buf_tok, num_segments=N + 1)[:N]
    return y.reshape(B, S, D)


def setup_inputs(seed: int = 0) -> dict:
    key = jax.random.key(seed)
    ks = jax.random.split(key, 32)
    f32 = jnp.float32
    n = lambda k, shape, scale: jax.random.normal(k, shape, f32) * scale
    x = jax.random.normal(ks[0], (BATCH, SEQ, D_MODEL), f32)
    p = jax.random.normal(ks[1], (DEPTH, BATCH, SEQ, PLE_DIM), f32)
    offsets = jax.random.randint(ks[2], (BATCH, 1), 0, 4096, dtype=jnp.int32)
    positions = offsets + jnp.arange(SEQ, dtype=jnp.int32)[None, :]
    col_scale = jnp.ones((PROJ_WIDTH,), f32).at[PROJ_SPLITS[3]:PROJ_SPLITS[4]].set(BETA)
    w_in = n(ks[3], (DEPTH, D_MODEL, PROJ_WIDTH), D_MODEL ** -0.5) * col_scale
    w_s = n(ks[4], (DEPTH, SGU_GROUPS, CHUNK, CHUNK), CHUNK ** -0.5)
    b_s = 1.0 + n(ks[5], (DEPTH, SGU_GROUPS, CHUNK), 0.1)
    ln_v_g = 1.0 + n(ks[6], (DEPTH, SGU_WIDTH), 0.02)
    ln_v_b = n(ks[7], (DEPTH, SGU_WIDTH), 0.02)
    w_a = n(ks[8], (DEPTH, SGU_WIDTH, D_MODEL), BETA * SGU_WIDTH ** -0.5)
    w_b = n(ks[9], (DEPTH, MERGED_ATT_WIDTH, D_MODEL), BETA * MERGED_ATT_WIDTH ** -0.5)
    w_o = n(ks[10], (DEPTH, D_MODEL, D_MODEL), BETA * D_MODEL ** -0.5)
    ln1_g = 1.0 + n(ks[11], (DEPTH, D_MODEL), 0.02)
    ln1_b = n(ks[12], (DEPTH, D_MODEL), 0.02)
    w_grp = n(ks[13], (DEPTH, D_MODEL, N_EXPERT_GROUPS), D_MODEL ** -0.5)
    b_grp = n(ks[14], (DEPTH, N_EXPERT_GROUPS), 0.01)
    w_rt = n(ks[15], (DEPTH, D_MODEL, N_EXPERTS), D_MODEL ** -0.5)
    b_rt = n(ks[16], (DEPTH, N_EXPERTS), 0.01)
    w1 = n(ks[17], (DEPTH, N_EXPERTS, D_MODEL, EXPERT_FF), D_MODEL ** -0.5)
    w3 = n(ks[18], (DEPTH, N_EXPERTS, D_MODEL, EXPERT_FF), D_MODEL ** -0.5)
    w2 = n(ks[19], (DEPTH, N_EXPERTS, EXPERT_FF, D_MODEL), BETA * EXPERT_FF ** -0.5)
    w_pg = n(ks[20], (DEPTH, D_MODEL, D_MODEL), D_MODEL ** -0.5)
    w_pp = n(ks[21], (DEPTH, PLE_DIM, D_MODEL), BETA * PLE_DIM ** -0.5)
    ln2_g = 1.0 + n(ks[22], (DEPTH, D_MODEL), 0.02)
    ln2_b = n(ks[23], (DEPTH, D_MODEL), 0.02)
    return {"x": x, "p": p, "positions": positions, "w_in": w_in, "w_s": w_s, "b_s": b_s,
            "ln_v_g": ln_v_g, "ln_v_b": ln_v_b, "w_a": w_a, "w_b": w_b, "w_o": w_o,
            "ln1_g": ln1_g, "ln1_b": ln1_b, "w_grp": w_grp, "b_grp": b_grp, "w_rt": w_rt,
            "b_rt": b_rt, "w1": w1, "w3": w3, "w2": w2, "w_pg": w_pg, "w_pp": w_pp,
            "ln2_g": ln2_g, "ln2_b": ln2_b}


def reference(x, p, positions, w_in, w_s, b_s, ln_v_g, ln_v_b, w_a, w_b, w_o, ln1_g, ln1_b,
              w_grp, b_grp, w_rt, b_rt, w1, w3, w2, w_pg, w_pp, ln2_g, ln2_b):
    B, S, D = x.shape
    dt = x.dtype
    inv_freq = ROPE_THETA ** (-jnp.arange(0, ROT_DIM, 2, dtype=jnp.float32) / ROT_DIM)
    ang = positions.astype(jnp.float32)[..., None] * inv_freq
    cos, sin = jnp.cos(ang)[:, :, None, :], jnp.sin(ang)[:, :, None, :]

    for i in range(DEPTH):
        proj = x @ w_in[i].astype(dt)
        u, vs, q, k, va, ga, gb = jnp.split(proj, PROJ_SPLITS, axis=-1)
        a_out = chunked_sgu(u, vs, ln_v_g[i], ln_v_b[i], w_s[i], b_s[i]) @ w_a[i].astype(dt)
        q = rotary(q.reshape(B, S, ATT_HEADS, HEAD_DIM), cos, sin)
        k = rotary(k.reshape(B, S, ATT_HEADS, HEAD_DIM), cos, sin)
        va = va.reshape(B, S, ATT_HEADS, HEAD_DIM)
        b_out = dilated_mixture(q, k, va) @ w_b[i].astype(dt)
        mixed = (jax.nn.sigmoid(ga) * a_out + jax.nn.sigmoid(gb) * b_out) @ w_o[i].astype(dt)
        x = layer_norm(ALPHA * x + mixed, ln1_g[i], ln1_b[i])
        y = hierarchical_moe(x, w_grp[i], b_grp[i], w_rt[i], b_rt[i], w1[i], w3[i], w2[i])
        ple = jax.nn.sigmoid(x @ w_pg[i].astype(dt)) * (p[i].astype(dt) @ w_pp[i].astype(dt))
        x = layer_norm(ALPHA * x + y + ple, ln2_g[i], ln2_b[i])
    return x
```

```python
import functools

import jax
import jax.numpy as jnp
from jax import lax
from jax.experimental import pallas as pl
from jax.experimental.pallas import tpu as pltpu

F32 = jnp.float32
BF16 = jnp.bfloat16

D_MODEL = 2048
DEPTH_FOR_DEEPNORM = 4
SGU_WIDTH = 1024
SGU_GROUPS = 8
SGU_GROUP_CH = SGU_WIDTH // SGU_GROUPS
CHUNK = 128
HEAD_DIM = 128
HEADS_PER_GROUP = 4
DILATED_GROUPS = ((128, 1), (512, 4), (2048, 16))
N_DIL = len(DILATED_GROUPS)
ATT_WIDTH = HEADS_PER_GROUP * N_DIL * HEAD_DIM
GROUP_WIDTH = HEADS_PER_GROUP * HEAD_DIM
QBLK = 128
ROT_DIM = HEAD_DIM // 4
ROPE_THETA = 500000.0
PROJ_WIDTH = 2 * SGU_WIDTH + 3 * ATT_WIDTH + 2 * D_MODEL
N_EXPERT_GROUPS = 4
EXPERTS_PER_GROUP = 8
N_EXPERTS = N_EXPERT_GROUPS * EXPERTS_PER_GROUP
TOP_K = 2
EXPERT_FF = 512
MOE_BLK = 256
PLE_DIM = 256
ALPHA = (2 * DEPTH_FOR_DEEPNORM) ** 0.25
LN_EPS = 1e-5
ROUTER_LANES = 128

PROJ_TN = 512
PROJ_TM = 1024
SGU_TM = 512
MIX_TM = 256
FINAL_TM = 256
VMEM_LIMIT = 56 * 1024 * 1024

_T_GA = 0
_T_GB = _T_GA + D_MODEL // PROJ_TN
_T_U = _T_GB + D_MODEL // PROJ_TN
_T_V = _T_U + SGU_WIDTH // PROJ_TN
_T_Q = _T_V + SGU_WIDTH // PROJ_TN
_T_K = _T_Q + ATT_WIDTH // PROJ_TN
_T_VA = _T_K + ATT_WIDTH // PROJ_TN
_T_END = _T_VA + ATT_WIDTH // PROJ_TN
assert _T_END * PROJ_TN == PROJ_WIDTH and GROUP_WIDTH == PROJ_TN
_REF_SPLITS = (0, SGU_WIDTH, 2 * SGU_WIDTH, 2 * SGU_WIDTH + ATT_WIDTH, 2 * SGU_WIDTH + 2 * ATT_WIDTH,
               2 * SGU_WIDTH + 3 * ATT_WIDTH, 2 * SGU_WIDTH + 3 * ATT_WIDTH + D_MODEL, PROJ_WIDTH)


def _reorder_proj_columns(w):
    sec = [w[:, _REF_SPLITS[k]:_REF_SPLITS[k + 1]] for k in range(7)]
    return jnp.concatenate([sec[5], sec[6], sec[0], sec[1], sec[2], sec[3], sec[4]], axis=1)


def _params(sem, vmem=VMEM_LIMIT):
    return pltpu.CompilerParams(dimension_semantics=sem, vmem_limit_bytes=vmem)


def _resident(shape):
    nd = len(shape)
    return pl.BlockSpec(shape, lambda *_: (0,) * nd, pipeline_mode=pl.Buffered(1))


def _layer_norm(y, g, b):
    mu = jnp.mean(y, axis=-1, keepdims=True)
    yc = y - mu
    var = jnp.mean(yc * yc, axis=-1, keepdims=True)
    return yc * lax.rsqrt(var + LN_EPS) * g + b


def _proj_kernel(x_ref, w_ref, cos_ref, sin_ref, o_ref):
    j = pl.program_id(1)
    acc = jnp.dot(x_ref[...], w_ref[...], preferred_element_type=F32)

    @pl.when(j < _T_U)
    def _():
        o_ref[...] = jax.nn.sigmoid(acc).astype(o_ref.dtype)

    @pl.when((j >= _T_U) & (j < _T_Q))
    def _():
        o_ref[...] = jax.nn.gelu(acc).astype(o_ref.dtype)

    @pl.when((j >= _T_Q) & (j < _T_VA))
    def _():
        cosf = cos_ref[...]
        sinf = sin_ref[...]
        lane = lax.broadcasted_iota(jnp.int32, cosf.shape, 1)
        half = ROT_DIM // 2
        for h in range(HEADS_PER_GROUP):
            hs = slice(h * HEAD_DIM, (h + 1) * HEAD_DIM)
            xh = acc[:, hs]
            partner = jnp.where(lane < half, pltpu.roll(xh, HEAD_DIM - half, 1), pltpu.roll(xh, half, 1))
            o_ref[:, hs] = (xh * cosf + partner * sinf).astype(o_ref.dtype)

    @pl.when(j >= _T_VA)
    def _():
        o_ref[...] = acc.astype(o_ref.dtype)


def _proj_call(xb, w, cosf, sinf):
    n, d = xb.shape
    tm = min(PROJ_TM, n)
    return pl.pallas_call(
        _proj_kernel,
        grid=(n // tm, PROJ_WIDTH // PROJ_TN),
        in_specs=[
            pl.BlockSpec((tm, d), lambda i, j: (i, 0)),
            pl.BlockSpec((d, PROJ_TN), lambda i, j: (0, j)),
            pl.BlockSpec((tm, HEAD_DIM), lambda i, j: (i, 0)),
            pl.BlockSpec((tm, HEAD_DIM), lambda i, j: (i, 0)),
        ],
        out_specs=pl.BlockSpec((tm, PROJ_TN), lambda i, j: (i, j)),
        out_shape=jax.ShapeDtypeStruct((n, PROJ_WIDTH), BF16),
        compiler_params=_params(("parallel", "arbitrary")),
        name="proj",
    )(xb, w, cosf, sinf)


def _sgu_kernel(u_ref, v_ref, ws_ref, bs_ref, g_ref, b_ref, o_ref):
    vn = _layer_norm(v_ref[...].astype(F32), g_ref[...], b_ref[...]).astype(BF16)
    tm = o_ref.shape[0]
    for c in range(tm // CHUNK):
        rows = slice(c * CHUNK, (c + 1) * CHUNK)
        for g in range(SGU_GROUPS):
            cols = slice(g * SGU_GROUP_CH, (g + 1) * SGU_GROUP_CH)
            z = jnp.dot(ws_ref[g], vn[rows, cols], preferred_element_type=F32) + bs_ref[:, cols]
            o_ref[rows, cols] = (u_ref[rows, cols].astype(F32) * z).astype(o_ref.dtype)


def _sgu_call(proj, ws, bs, ln_g, ln_b):
    n = proj.shape[0]
    tm = min(SGU_TM, n)
    u_blk, v_blk = _T_U * PROJ_TN // SGU_WIDTH, _T_V * PROJ_TN // SGU_WIDTH
    assert u_blk * SGU_WIDTH == _T_U * PROJ_TN and v_blk * SGU_WIDTH == _T_V * PROJ_TN
    return pl.pallas_call(
        _sgu_kernel,
        grid=(n // tm,),
        in_specs=[
            pl.BlockSpec((tm, SGU_WIDTH), lambda i: (i, u_blk)),
            pl.BlockSpec((tm, SGU_WIDTH), lambda i: (i, v_blk)),
            _resident(ws.shape),
            _resident(bs.shape),
            _resident(ln_g.shape),
            _resident(ln_b.shape),
        ],
        out_specs=pl.BlockSpec((tm, SGU_WIDTH), lambda i: (i, 0)),
        out_shape=jax.ShapeDtypeStruct((n, SGU_WIDTH), BF16),
        compiler_params=_params(("parallel",)),
        name="sgu",
    )(proj, proj, ws, bs, ln_g, ln_b)


def _attn_kernel(q_ref, kp_ref, kc_ref, vp_ref, vc_ref, o_ref, lse_ref, *, span):
    blk = pl.program_id(2)
    qi = lax.broadcasted_iota(jnp.int32, (QBLK, QBLK), 0)
    kj = lax.broadcasted_iota(jnp.int32, (QBLK, QBLK), 1)
    dist_prev = qi + QBLK - kj
    dist_cur = qi - kj
    mask_prev = (dist_prev >= 0) & (dist_prev <= span) & (blk > 0)
    mask_cur = (dist_cur >= 0) & (dist_cur <= span)
    scale = HEAD_DIM ** -0.5
    nt = (((1,), (1,)), ((), ()))
    lane = lax.broadcasted_iota(jnp.int32, (QBLK, HEAD_DIM), 1)
    lse_all = jnp.zeros((QBLK, HEAD_DIM), F32)
    for h in range(HEADS_PER_GROUP):
        hs = slice(h * HEAD_DIM, (h + 1) * HEAD_DIM)
        q = q_ref[:, hs]
        sp = lax.dot_general(q, kp_ref[:, hs], nt, preferred_element_type=F32) * scale
        sc = lax.dot_general(q, kc_ref[:, hs], nt, preferred_element_type=F32) * scale
        sp = jnp.where(mask_prev, sp, -jnp.inf)
        sc = jnp.where(mask_cur, sc, -jnp.inf)
        m = jnp.maximum(jnp.max(sp, axis=-1, keepdims=True), jnp.max(sc, axis=-1, keepdims=True))
        ep = jnp.exp(sp - m)
        ec = jnp.exp(sc - m)
        den = jnp.sum(ep, axis=-1, keepdims=True) + jnp.sum(ec, axis=-1, keepdims=True)
        o = jnp.dot(ep.astype(BF16), vp_ref[:, hs], preferred_element_type=F32)
        o = o + jnp.dot(ec.astype(BF16), vc_ref[:, hs], preferred_element_type=F32)
        o_ref[:, hs] = (o / den).astype(o_ref.dtype)
        lse_all = jnp.where(lane == h, m + jnp.log(den), lse_all)
    lse_ref[...] = lse_all


def _attn_call(proj3, group, batch, seq):
    window, dil = DILATED_GROUPS[group]
    span = window // dil
    length = seq // dil
    assert length % QBLK == 0 and span <= QBLK
    nb = length // QBLK
    tiles_per_tok = PROJ_WIDTH // PROJ_TN
    view = proj3.reshape(batch, length, dil * PROJ_WIDTH)

    def col(base):
        return lambda b, r, n: (b, n, r * tiles_per_tok + base + group)

    def col_prev(base):
        return lambda b, r, n: (b, jnp.maximum(n - 1, 0), r * tiles_per_tok + base + group)

    blk = (None, QBLK, PROJ_TN)
    o, lse = pl.pallas_call(
        functools.partial(_attn_kernel, span=span),
        grid=(batch, dil, nb),
        in_specs=[
            pl.BlockSpec(blk, col(_T_Q)),
            pl.BlockSpec(blk, col_prev(_T_K)),
            pl.BlockSpec(blk, col(_T_K)),
            pl.BlockSpec(blk, col_prev(_T_VA)),
            pl.BlockSpec(blk, col(_T_VA)),
        ],
        out_specs=[
            pl.BlockSpec((None, QBLK, GROUP_WIDTH), lambda b, r, n: (b, n, r)),
            pl.BlockSpec((None, QBLK, HEAD_DIM), lambda b, r, n: (b, n, r)),
        ],
        out_shape=[
            jax.ShapeDtypeStruct((batch, length, dil * GROUP_WIDTH), BF16),
            jax.ShapeDtypeStruct((batch, length, dil * HEAD_DIM), F32),
        ],
        compiler_params=_params(("parallel", "parallel", "arbitrary")),
        name=f"attn_d{dil}",
    )(view, view, view, view, view)
    return o.reshape(batch * seq, GROUP_WIDTH), lse.reshape(batch * seq, HEAD_DIM)


def _mix_kernel(x_ref, act_ref, o0_ref, o1_ref, o2_ref, l0_ref, l1_ref, l2_ref, sga_ref, sgb_ref,
                wa_ref, wb_ref, wo_ref, g_ref, b_ref, wr_ref, br_ref, x1_ref, logit_ref):
    a_out = jnp.dot(act_ref[...], wa_ref[...], preferred_element_type=F32)
    o_refs = (o0_ref, o1_ref, o2_ref)
    lses = (l0_ref[...], l1_ref[...], l2_ref[...])
    heads = []
    for h in range(HEADS_PER_GROUP):
        hs = slice(h * HEAD_DIM, (h + 1) * HEAD_DIM)
        lh = [l[:, h:h + 1] for l in lses]
        m = jnp.maximum(jnp.maximum(lh[0], lh[1]), lh[2])
        e = [jnp.exp(v - m) for v in lh]
        tot = e[0] + e[1] + e[2]
        acc = (e[0] / tot) * o_refs[0][:, hs].astype(F32)
        acc = acc + (e[1] / tot) * o_refs[1][:, hs].astype(F32)
        acc = acc + (e[2] / tot) * o_refs[2][:, hs].astype(F32)
        heads.append(acc.astype(BF16))
    merged = jnp.concatenate(heads, axis=1)
    b_out = jnp.dot(merged, wb_ref[...], preferred_element_type=F32)
    mixed_in = sga_ref[...].astype(F32) * a_out + sgb_ref[...].astype(F32) * b_out
    mixed = jnp.dot(mixed_in.astype(BF16), wo_ref[...], preferred_element_type=F32)
    x1 = _layer_norm(ALPHA * x_ref[...] + mixed, g_ref[...], b_ref[...])
    x1_ref[...] = x1
    logit_ref[...] = jnp.dot(x1, wr_ref[...], preferred_element_type=F32,
                             precision=lax.Precision.HIGHEST) + br_ref[...]


def _mix_call(x, act, os_, lses, proj, wa, wb, wo, g, b, wr, br):
    n = x.shape[0]
    tm = min(MIX_TM, n)
    ga_blk = _T_GA * PROJ_TN // D_MODEL
    gb_blk = _T_GB * PROJ_TN // D_MODEL
    assert ga_blk * D_MODEL == _T_GA * PROJ_TN and gb_blk * D_MODEL == _T_GB * PROJ_TN
    row = lambda w: pl.BlockSpec((tm, w), lambda i: (i, 0))
    return pl.pallas_call(
        _mix_kernel,
        grid=(n // tm,),
        in_specs=[
            row(D_MODEL), row(SGU_WIDTH),
            row(GROUP_WIDTH), row(GROUP_WIDTH), row(GROUP_WIDTH),
            row(HEAD_DIM), row(HEAD_DIM), row(HEAD_DIM),
            pl.BlockSpec((tm, D_MODEL), lambda i: (i, ga_blk)),
            pl.BlockSpec((tm, D_MODEL), lambda i: (i, gb_blk)),
            _resident(wa.shape), _resident(wb.shape), _resident(wo.shape),
            _resident(g.shape), _resident(b.shape), _resident(wr.shape), _resident(br.shape),
        ],
        out_specs=[row(D_MODEL), row(ROUTER_LANES)],
        out_shape=[jax.ShapeDtypeStruct((n, D_MODEL), F32), jax.ShapeDtypeStruct((n, ROUTER_LANES), F32)],
        compiler_params=_params(("parallel",)),
        name="mix",
    )(x, act, *os_, *lses, proj, proj, wa, wb, wo, g, b, wr, br)


def _start_row_gather(idx_ref, n_rows, src_hbm, dst, sem):
    def body(r, carry):
        pltpu.make_async_copy(src_hbm.at[pl.ds(idx_ref[0, r], 1)], dst.at[pl.ds(r, 1)], sem).start()
        return carry
    lax.fori_loop(0, n_rows, body, 0, unroll=8)


def _wait_row_gather(n_rows, src_hbm, dst, sem):
    pltpu.make_async_copy(src_hbm.at[pl.ds(0, n_rows)], dst, sem).wait()


def _expert_kernel(blk_e_ref, n_used_ref, tok_cur_ref, tok_next_ref, bw_ref, x_hbm, w1_ref, w3_ref, w2_ref,
                   o_ref, xbuf, sem):
    del blk_e_ref
    i = pl.program_id(0)
    n_used = n_used_ref[0]
    slot = i % 2

    @pl.when((i == 0) & (n_used > 0))
    def _():
        _start_row_gather(tok_cur_ref, MOE_BLK, x_hbm, xbuf.at[0], sem.at[0])

    @pl.when(i + 1 < n_used)
    def _():
        _start_row_gather(tok_next_ref, MOE_BLK, x_hbm, xbuf.at[1 - slot], sem.at[1 - slot])

    @pl.when(i < n_used)
    def _():
        _wait_row_gather(MOE_BLK, x_hbm, xbuf.at[slot], sem.at[slot])
        xb = xbuf[slot].astype(BF16)
        h1 = jnp.dot(xb, w1_ref[...], preferred_element_type=F32)
        h3 = jnp.dot(xb, w3_ref[...], preferred_element_type=F32)
        hid = (jax.nn.silu(h1) * h3).astype(BF16)
        y = jnp.dot(hid, w2_ref[...], preferred_element_type=F32)
        o_ref[...] = y * bw_ref[...]

    @pl.when(i >= n_used)
    def _():
        o_ref[...] = jnp.zeros_like(o_ref)


def _expert_call(blk_e, n_used, buf_tok, buf_w, x1, w1, w3, w2):
    n_blocks = blk_e.shape[0]
    d = x1.shape[1]
    tok3 = buf_tok.reshape(n_blocks, 1, MOE_BLK)
    grid_spec = pltpu.PrefetchScalarGridSpec(
        num_scalar_prefetch=2,
        grid=(n_blocks,),
        in_specs=[
            pl.BlockSpec((None, 1, MOE_BLK), lambda i, e, u: (i, 0, 0), memory_space=pltpu.SMEM),
            pl.BlockSpec((None, 1, MOE_BLK), lambda i, e, u: (jnp.minimum(i + 1, n_blocks - 1), 0, 0),
                         memory_space=pltpu.SMEM),
            pl.BlockSpec((MOE_BLK, 1), lambda i, e, u: (i, 0)),
            pl.BlockSpec(memory_space=pl.ANY),
            pl.BlockSpec((None, d, EXPERT_FF), lambda i, e, u: (e[i], 0, 0)),
            pl.BlockSpec((None, d, EXPERT_FF), lambda i, e, u: (e[i], 0, 0)),
            pl.BlockSpec((None, EXPERT_FF, d), lambda i, e, u: (e[i], 0, 0)),
        ],
        out_specs=pl.BlockSpec((MOE_BLK, d), lambda i, e, u: (i, 0)),
        scratch_shapes=[pltpu.VMEM((2, MOE_BLK, d), F32), pltpu.SemaphoreType.DMA((2,))],
    )
    return pl.pallas_call(
        _expert_kernel,
        grid_spec=grid_spec,
        out_shape=jax.ShapeDtypeStruct((n_blocks * MOE_BLK, d), F32),
        compiler_params=_params(("arbitrary",)),
        name="experts",
    )(blk_e, n_used, tok3, tok3, buf_w.reshape(-1, 1), x1, w1, w3, w2)


def _final_kernel(pos_cur_ref, pos_next_ref, x1_ref, p_ref, yb_hbm, wpg_ref, wpp_ref, g_ref, b_ref,
                  x2_ref, x2b_ref, ybuf, sem):
    i = pl.program_id(0)
    n_steps = pl.num_programs(0)
    slot = i % 2
    tm = x1_ref.shape[0]

    @pl.when(i == 0)
    def _():
        _start_row_gather(pos_cur_ref, 2 * tm, yb_hbm, ybuf.at[0], sem.at[0])

    @pl.when(i + 1 < n_steps)
    def _():
        _start_row_gather(pos_next_ref, 2 * tm, yb_hbm, ybuf.at[1 - slot], sem.at[1 - slot])

    x1 = x1_ref[...]
    gate = jax.nn.sigmoid(jnp.dot(x1.astype(BF16), wpg_ref[...], preferred_element_type=F32))
    emb = jnp.dot(p_ref[...].astype(BF16), wpp_ref[...], preferred_element_type=F32)
    _wait_row_gather(2 * tm, yb_hbm, ybuf.at[slot], sem.at[slot])
    y = ybuf[slot, pl.ds(0, tm), :] + ybuf[slot, pl.ds(tm, tm), :]
    x2 = _layer_norm(ALPHA * x1 + y + gate * emb, g_ref[...], b_ref[...])
    x2_ref[...] = x2
    x2b_ref[...] = x2.astype(BF16)


def _final_call(pos, x1, p, yb, wpg, wpp, g, b):
    n, d = x1.shape
    tm = min(FINAL_TM, n)
    steps = n // tm
    pos3 = pos.reshape(steps, tm, TOP_K).transpose(0, 2, 1).reshape(steps, 1, TOP_K * tm)
    row = lambda w: pl.BlockSpec((tm, w), lambda i: (i, 0))
    return pl.pallas_call(
        _final_kernel,
        grid=(steps,),
        in_specs=[
            pl.BlockSpec((None, 1, TOP_K * tm), lambda i: (i, 0, 0), memory_space=pltpu.SMEM),
            pl.BlockSpec((None, 1, TOP_K * tm), lambda i: (jnp.minimum(i + 1, steps - 1), 0, 0),
                         memory_space=pltpu.SMEM),
            row(d), row(PLE_DIM),
            pl.BlockSpec(memory_space=pl.ANY),
            _resident(wpg.shape), _resident(wpp.shape), _resident(g.shape), _resident(b.shape),
        ],
        out_specs=[row(d), row(d)],
        out_shape=[jax.ShapeDtypeStruct((n, d), F32), jax.ShapeDtypeStruct((n, d), BF16)],
        scratch_shapes=[pltpu.VMEM((2, TOP_K * tm, d), F32), pltpu.SemaphoreType.DMA((2,))],
        compiler_params=_params(("arbitrary",)),
        name="final",
    )(pos3, pos3, x1, p, yb, wpg, wpp, g, b)


def _route(logits, b_grp_unused=None):
    n = logits.shape[0]
    grp_logits = logits[:, :N_EXPERT_GROUPS]
    exp_logits = logits[:, N_EXPERT_GROUPS:N_EXPERT_GROUPS + N_EXPERTS].reshape(n, N_EXPERT_GROUPS, EXPERTS_PER_GROUP)
    grp_prob = jax.nn.softmax(grp_logits, axis=-1)
    g_sel = jnp.argmax(grp_logits, axis=-1).astype(jnp.int32)
    p_g = jnp.take_along_axis(grp_prob, g_sel[:, None], axis=1)[:, 0]
    in_grp = jnp.take_along_axis(exp_logits, g_sel[:, None, None], axis=1)[:, 0]
    top_v, top_i = lax.top_k(in_grp, TOP_K)
    gate = jax.nn.softmax(top_v, axis=-1) * p_g[:, None]
    e_id = g_sel[:, None] * EXPERTS_PER_GROUP + top_i.astype(jnp.int32)

    a = n * TOP_K
    e_flat = e_id.reshape(a)
    w_flat = gate.reshape(a)
    tok = jnp.arange(a, dtype=jnp.int32) // TOP_K
    onehot = (e_flat[:, None] == jnp.arange(N_EXPERTS, dtype=jnp.int32)[None, :]).astype(jnp.int32)
    csum = jnp.cumsum(onehot, axis=0)
    rank = jnp.take_along_axis(csum, e_flat[:, None], axis=1)[:, 0] - 1
    counts = csum[-1]
    padded = ((counts + MOE_BLK - 1) // MOE_BLK) * MOE_BLK
    pend = jnp.cumsum(padded)
    pstart = pend - padded
    dest = pstart[e_flat] + rank
    n_blocks = (a + N_EXPERTS * (MOE_BLK - 1) + MOE_BLK - 1) // MOE_BLK
    p_rows = n_blocks * MOE_BLK
    buf_tok = jnp.zeros((p_rows,), jnp.int32).at[dest].set(tok)
    buf_w = jnp.zeros((p_rows,), F32).at[dest].set(w_flat)
    blk_e = jnp.minimum(
        jnp.searchsorted(pend, jnp.arange(n_blocks, dtype=jnp.int32) * MOE_BLK, side="right"),
        N_EXPERTS - 1).astype(jnp.int32)
    n_used = (pend[-1:] // MOE_BLK).astype(jnp.int32)
    return blk_e, n_used, buf_tok, buf_w, dest.reshape(n, TOP_K)


def _rotary_tables(positions):
    half = ROT_DIM // 2
    inv_freq = ROPE_THETA ** (-jnp.arange(0, ROT_DIM, 2, dtype=F32) / ROT_DIM)
    ang = positions.astype(F32).reshape(-1)[:, None] * inv_freq
    cos, sin = jnp.cos(ang), jnp.sin(ang)
    n = ang.shape[0]
    cosf = jnp.concatenate([cos, cos, jnp.ones((n, HEAD_DIM - ROT_DIM), F32)], axis=1)
    sinf = jnp.concatenate([-sin, sin, jnp.zeros((n, HEAD_DIM - ROT_DIM), F32)], axis=1)
    del half
    return cosf, sinf


def kernel(x, p, positions, w_in, w_s, b_s, ln_v_g, ln_v_b, w_a, w_b, w_o, ln1_g, ln1_b, w_grp, b_grp, w_rt, b_rt, w1, w3, w2, w_pg, w_pp, ln2_g, ln2_b):
    batch, seq, d = x.shape
    depth = w_in.shape[0]
    n = batch * seq
    assert d == D_MODEL and w_in.shape[2] == PROJ_WIDTH
    cosf, sinf = _rotary_tables(positions)
    causal = jnp.tril(jnp.ones((CHUNK, CHUNK), F32))
    xf = x.reshape(n, d)
    xb = xf.astype(BF16)
    pad_r = ROUTER_LANES - N_EXPERT_GROUPS - N_EXPERTS

    for i in range(depth):
        proj = _proj_call(xb, _reorder_proj_columns(w_in[i]).astype(BF16), cosf, sinf)
        ws = (w_s[i] * causal).astype(BF16)
        bs = jnp.repeat(b_s[i].T, SGU_GROUP_CH, axis=1)
        act = _sgu_call(proj, ws, bs, ln_v_g[i][None, :], ln_v_b[i][None, :])
        proj3 = proj.reshape(batch, seq, PROJ_WIDTH)
        att = [_attn_call(proj3, g, batch, seq) for g in range(N_DIL)]
        wr = jnp.concatenate([w_grp[i], w_rt[i], jnp.zeros((d, pad_r), F32)], axis=1)
        br = jnp.concatenate([b_grp[i], b_rt[i], jnp.zeros((pad_r,), F32)])[None, :]
        x1, logits = _mix_call(xf, act, [a[0] for a in att], [a[1] for a in att], proj,
                               w_a[i].astype(BF16), w_b[i].astype(BF16), w_o[i].astype(BF16),
                               ln1_g[i][None, :], ln1_b[i][None, :], wr, br)
        blk_e, n_used, buf_tok, buf_w, pos = _route(logits)
        yb = _expert_call(blk_e, n_used, buf_tok, buf_w, x1,
                          w1[i].astype(BF16), w3[i].astype(BF16), w2[i].astype(BF16))
        xf, xb = _final_call(pos, x1, p[i].reshape(n, PLE_DIM), yb, w_pg[i].astype(BF16), w_pp[i].astype(BF16),
                             ln2_g[i][None, :], ln2_b[i][None, :])
    return xf.reshape(batch, seq, d)
```

```python
import functools

import jax
import jax.numpy as jnp
from jax import lax
from jax.experimental import pallas as pl
from jax.experimental.pallas import tpu as pltpu

F32 = jnp.float32
BF16 = jnp.bfloat16

D_MODEL = 2048
DEPTH_FOR_DEEPNORM = 4
SGU_WIDTH = 1024
SGU_GROUPS = 8
SGU_GROUP_CH = SGU_WIDTH // SGU_GROUPS
CHUNK = 128
HEAD_DIM = 128
HEADS_PER_GROUP = 4
DILATED_GROUPS = ((128, 1), (512, 4), (2048, 16))
N_DIL = len(DILATED_GROUPS)
ATT_HEADS = HEADS_PER_GROUP * N_DIL
ATT_WIDTH = ATT_HEADS * HEAD_DIM
QBLK = 128
ROT_DIM = HEAD_DIM // 4
ROT_HALF = ROT_DIM // 2
ROPE_THETA = 500000.0
PROJ_WIDTH = 2 * SGU_WIDTH + 3 * ATT_WIDTH + 2 * D_MODEL
N_EXPERT_GROUPS = 4
EXPERTS_PER_GROUP = 8
N_EXPERTS = N_EXPERT_GROUPS * EXPERTS_PER_GROUP
TOP_K = 2
EXPERT_FF = 512
MOE_BLK = 256
PLE_DIM = 256
ALPHA = (2 * DEPTH_FOR_DEEPNORM) ** 0.25
LN_EPS = 1e-5
ROUTER_LANES = 128

LANES = 128
SLABS = D_MODEL // LANES

PROJ_TM = 1024
PROJ_A_TN = 1024
PROJ_B_TN = ATT_WIDTH
PROJ_CHUNK = 256
SGU_TM = 512
MIX_TM = 256
FINAL_TM = 256
SUB_ROWS = 128
ATTN_STEP = {1: (8, 4), 4: (2, 4), 16: (1, 2)}
VMEM_LIMIT = 56 * 1024 * 1024

_REF_SPLITS = (0, SGU_WIDTH, 2 * SGU_WIDTH, 2 * SGU_WIDTH + ATT_WIDTH, 2 * SGU_WIDTH + 2 * ATT_WIDTH,
               2 * SGU_WIDTH + 3 * ATT_WIDTH, 2 * SGU_WIDTH + 3 * ATT_WIDTH + D_MODEL, PROJ_WIDTH)
_A_GB = D_MODEL
_A_U = 2 * D_MODEL
_A_V = 2 * D_MODEL + SGU_WIDTH
PROJ_A_WIDTH = 2 * D_MODEL + 2 * SGU_WIDTH


def _params(sem, vmem=VMEM_LIMIT):
    return pltpu.CompilerParams(dimension_semantics=sem, vmem_limit_bytes=vmem)


def _resident(shape):
    nd = len(shape)
    return pl.BlockSpec(shape, lambda *_: (0,) * nd, pipeline_mode=pl.Buffered(1))


def _layer_norm(y, g, b):
    mu = jnp.mean(y, axis=-1, keepdims=True)
    yc = y - mu
    var = jnp.mean(yc * yc, axis=-1, keepdims=True)
    return yc * lax.rsqrt(var + LN_EPS) * g + b


def _rotary_head_perm(w_sec):
    d, width = w_sec.shape
    w = w_sec.reshape(d, width // HEAD_DIM, HEAD_DIM)
    half_rest = HEAD_DIM // 2 - ROT_HALF
    w = jnp.concatenate([w[:, :, :ROT_HALF], w[:, :, ROT_DIM:ROT_DIM + half_rest],
                         w[:, :, ROT_HALF:ROT_DIM], w[:, :, ROT_DIM + half_rest:]], axis=2)
    return w.reshape(d, width)


def _split_proj_weights(w):
    sec = [w[:, _REF_SPLITS[k]:_REF_SPLITS[k + 1]] for k in range(7)]
    wa = jnp.concatenate([sec[5], sec[6], sec[0], sec[1]], axis=1).astype(BF16)
    wb = jnp.concatenate([_rotary_head_perm(sec[2]), _rotary_head_perm(sec[3]), sec[4]], axis=1).astype(BF16)
    return wa, wb


def _rotary_tables(positions):
    inv_freq = ROPE_THETA ** (-jnp.arange(0, ROT_DIM, 2, dtype=F32) / ROT_DIM)
    ang = positions.astype(F32).reshape(-1)[:, None] * inv_freq
    cos, sin = jnp.cos(ang), jnp.sin(ang)
    n = ang.shape[0]
    pad = HEAD_DIM // 2 - ROT_HALF
    ones, zeros = jnp.ones((n, pad), F32), jnp.zeros((n, pad), F32)
    cosf = jnp.concatenate([cos, ones, cos, ones], axis=1)
    sinf = jnp.concatenate([-sin, zeros, sin, zeros], axis=1)
    return cosf, sinf


def _proj_a_kernel(x_ref, w_ref, o_ref):
    j = pl.program_id(1)
    tn = o_ref.shape[1]

    def run(act):
        for c in range(tn // PROJ_CHUNK):
            cols = slice(c * PROJ_CHUNK, (c + 1) * PROJ_CHUNK)
            acc = jnp.dot(x_ref[...], w_ref[:, cols], preferred_element_type=F32)
            o_ref[:, cols] = act(acc).astype(o_ref.dtype)

    @pl.when(j < _A_U // tn)
    def _():
        run(jax.nn.sigmoid)

    @pl.when(j >= _A_U // tn)
    def _():
        run(jax.nn.gelu)


def _proj_a_call(xb, w):
    n, d = xb.shape
    tm = min(PROJ_TM, n)
    tn = PROJ_A_TN
    assert _A_U % tn == 0 and PROJ_A_WIDTH % tn == 0
    return pl.pallas_call(
        _proj_a_kernel,
        grid=(n // tm, PROJ_A_WIDTH // tn),
        in_specs=[
            pl.BlockSpec((tm, d), lambda i, j: (i, 0)),
            pl.BlockSpec((d, tn), lambda i, j: (0, j)),
        ],
        out_specs=pl.BlockSpec((tm, tn), lambda i, j: (i, j)),
        out_shape=jax.ShapeDtypeStruct((n, PROJ_A_WIDTH), BF16),
        compiler_params=_params(("parallel", "arbitrary")),
        name="proj_a",
    )(xb, w)


def _proj_b_kernel(x_ref, w_ref, cos_ref, sin_ref, o_ref):
    j = pl.program_id(1)
    heads_per_chunk = PROJ_CHUNK // HEAD_DIM

    def run(rotate):
        for c in range(PROJ_B_TN // PROJ_CHUNK):
            cols = slice(c * PROJ_CHUNK, (c + 1) * PROJ_CHUNK)
            acc = jnp.dot(x_ref[...], w_ref[:, cols], preferred_element_type=F32)
            for h in range(heads_per_chunk):
                xh = acc[:, h * HEAD_DIM:(h + 1) * HEAD_DIM]
                if rotate:
                    xh = xh * cos_ref[...] + pltpu.roll(xh, HEAD_DIM // 2, 1) * sin_ref[...]
                o_ref[c * heads_per_chunk + h] = xh

    @pl.when(j < 2)
    def _():
        run(True)

    @pl.when(j >= 2)
    def _():
        run(False)


def _proj_b_call(xb, w, cosf, sinf):
    n, d = xb.shape
    tm = min(PROJ_TM, n)
    slabs = PROJ_B_TN // HEAD_DIM
    return pl.pallas_call(
        _proj_b_kernel,
        grid=(n // tm, 3),
        in_specs=[
            pl.BlockSpec((tm, d), lambda i, j: (i, 0)),
            pl.BlockSpec((d, PROJ_B_TN), lambda i, j: (0, j)),
            pl.BlockSpec((tm, HEAD_DIM), lambda i, j: (i, 0)),
            pl.BlockSpec((tm, HEAD_DIM), lambda i, j: (i, 0)),
        ],
        out_specs=pl.BlockSpec((slabs, tm, HEAD_DIM), lambda i, j: (j, i, 0)),
        out_shape=jax.ShapeDtypeStruct((3 * slabs, n, HEAD_DIM), F32),
        compiler_params=_params(("parallel", "arbitrary")),
        name="proj_b",
    )(xb, w, cosf, sinf)


def _sgu_kernel(u_ref, v_ref, ws_ref, bs_ref, g_ref, b_ref, o_ref):
    vn = _layer_norm(v_ref[...].astype(F32), g_ref[...], b_ref[...]).astype(BF16)
    tm = o_ref.shape[0]
    for c in range(tm // CHUNK):
        rows = slice(c * CHUNK, (c + 1) * CHUNK)
        for g in range(SGU_GROUPS):
            cols = slice(g * SGU_GROUP_CH, (g + 1) * SGU_GROUP_CH)
            z = jnp.dot(ws_ref[g], vn[rows, cols], preferred_element_type=F32) + bs_ref[:, cols]
            o_ref[rows, cols] = (u_ref[rows, cols].astype(F32) * z).astype(o_ref.dtype)


def _sgu_call(proj_a, ws, bs, ln_g, ln_b):
    n = proj_a.shape[0]
    tm = min(SGU_TM, n)
    u_blk, v_blk = _A_U // SGU_WIDTH, _A_V // SGU_WIDTH
    return pl.pallas_call(
        _sgu_kernel,
        grid=(n // tm,),
        in_specs=[
            pl.BlockSpec((tm, SGU_WIDTH), lambda i: (i, u_blk)),
            pl.BlockSpec((tm, SGU_WIDTH), lambda i: (i, v_blk)),
            _resident(ws.shape),
            _resident(bs.shape),
            _resident(ln_g.shape),
            _resident(ln_b.shape),
        ],
        out_specs=pl.BlockSpec((tm, SGU_WIDTH), lambda i: (i, 0)),
        out_shape=jax.ShapeDtypeStruct((n, SGU_WIDTH), BF16),
        compiler_params=_params(("parallel",)),
        name="sgu",
    )(proj_a, proj_a, ws, bs, ln_g, ln_b)


def _attn_kernel(q_ref, kp_ref, k_ref, vp_ref, v_ref, o_ref, lse_ref, *, dil, span, nq, hps):
    slab_idx = pl.program_id(1)
    hc = pl.program_id(2)
    qi = lax.broadcasted_iota(jnp.int32, (QBLK, QBLK), 0)
    kj = lax.broadcasted_iota(jnp.int32, (QBLK, QBLK), 1)
    dist_prev = qi + QBLK - kj
    dist_cur = qi - kj
    in_prev = (dist_prev >= 0) & (dist_prev <= span)
    mask_cur = (dist_cur >= 0) & (dist_cur <= span)
    scale = HEAD_DIM ** -0.5
    nt = (((1,), (1,)), ((), ()))
    lane = lax.broadcasted_iota(jnp.int32, (QBLK, LANES), 1)

    @pl.when(hc == 0)
    def _():
        lse_ref[...] = jnp.zeros_like(lse_ref)

    def rows(r, blk):
        return pl.ds(blk * QBLK * dil + r, QBLK, stride=dil) if dil > 1 else pl.ds(blk * QBLK, QBLK)

    def one_class(r):
        for blk in range(nq):
            if blk == 0:
                mask_prev = in_prev & (slab_idx > 0)
            else:
                mask_prev = in_prev
            sel = rows(r, blk)
            lse_rows = lse_ref[sel, :]
            for h in range(hps):
                q = q_ref.at[h][sel, :].astype(BF16)
                if blk == 0:
                    kp = kp_ref.at[h][rows(r, 0), :].astype(BF16)
                    vp = vp_ref.at[h][rows(r, 0), :].astype(BF16)
                else:
                    kp = k_ref.at[h][rows(r, blk - 1), :].astype(BF16)
                    vp = v_ref.at[h][rows(r, blk - 1), :].astype(BF16)
                kc = k_ref.at[h][sel, :].astype(BF16)
                vc = v_ref.at[h][sel, :].astype(BF16)
                sp = lax.dot_general(q, kp, nt, preferred_element_type=F32) * scale
                sc = lax.dot_general(q, kc, nt, preferred_element_type=F32) * scale
                sp = jnp.where(mask_prev, sp, -jnp.inf)
                sc = jnp.where(mask_cur, sc, -jnp.inf)
                m = jnp.maximum(jnp.max(sp, axis=-1, keepdims=True), jnp.max(sc, axis=-1, keepdims=True))
                ep = jnp.exp(sp - m)
                ec = jnp.exp(sc - m)
                den = jnp.sum(ep, axis=-1, keepdims=True) + jnp.sum(ec, axis=-1, keepdims=True)
                o = jnp.dot(ep.astype(BF16), vp, preferred_element_type=F32)
                o = o + jnp.dot(ec.astype(BF16), vc, preferred_element_type=F32)
                o_ref.at[h][sel, :] = o / den
                lse_rows = jnp.where(lane == hc * hps + h, m + jnp.log(den), lse_rows)
            lse_ref[sel, :] = lse_rows

    if dil == 1:
        one_class(0)
    else:
        def body(r, carry):
            one_class(r)
            return carry
        lax.fori_loop(0, dil, body, 0)


def _attn_call(qkv, group, batch, seq):
    window, dil = DILATED_GROUPS[group]
    span = window // dil
    nq, hps = ATTN_STEP[dil]
    slab = nq * QBLK * dil
    prev = QBLK * dil
    assert seq % slab == 0 and span <= QBLK and HEADS_PER_GROUP % hps == 0
    slabs_per_seq = seq // slab
    prev_per_seq = seq // prev
    n = batch * seq
    hblocks = HEADS_PER_GROUP // hps

    def cur(section):
        base = (section * ATT_HEADS + group * HEADS_PER_GROUP) // hps
        return pl.BlockSpec((hps, slab, LANES), lambda b, s, hc: (base + hc, b * slabs_per_seq + s, 0))

    def before(section):
        base = (section * ATT_HEADS + group * HEADS_PER_GROUP) // hps
        return pl.BlockSpec(
            (hps, prev, LANES),
            lambda b, s, hc: (base + hc, b * prev_per_seq + jnp.maximum(s * nq - 1, 0), 0))

    return pl.pallas_call(
        functools.partial(_attn_kernel, dil=dil, span=span, nq=nq, hps=hps),
        grid=(batch, slabs_per_seq, hblocks),
        in_specs=[cur(0), before(1), cur(1), before(2), cur(2)],
        out_specs=[
            pl.BlockSpec((hps, slab, LANES), lambda b, s, hc: (hc, b * slabs_per_seq + s, 0)),
            pl.BlockSpec((slab, LANES), lambda b, s, hc: (b * slabs_per_seq + s, 0)),
        ],
        out_shape=[
            jax.ShapeDtypeStruct((HEADS_PER_GROUP, n, LANES), F32),
            jax.ShapeDtypeStruct((n, LANES), F32),
        ],
        compiler_params=_params(("parallel", "arbitrary", "arbitrary")),
        name=f"attn_d{dil}",
    )(qkv, qkv, qkv, qkv, qkv)


def _mix_kernel(x_ref, act_ref, o0_ref, o1_ref, o2_ref, l0_ref, l1_ref, l2_ref, sga_ref, sgb_ref,
                wa_ref, wb_ref, wo_ref, g_ref, b_ref, wrh_ref, wrl_ref, br_ref, x1g_ref, logit_ref):
    tm = x_ref.shape[0]
    o_refs = (o0_ref, o1_ref, o2_ref)
    l_refs = (l0_ref, l1_ref, l2_ref)
    for sub in range(tm // SUB_ROWS):
        rows = pl.ds(sub * SUB_ROWS, SUB_ROWS)
        a_out = jnp.dot(act_ref[rows, :], wa_ref[...], preferred_element_type=F32)
        lses = [l[rows, :] for l in l_refs]
        heads = []
        for h in range(HEADS_PER_GROUP):
            lh = [l[:, h:h + 1] for l in lses]
            m = jnp.maximum(jnp.maximum(lh[0], lh[1]), lh[2])
            e = [jnp.exp(v - m) for v in lh]
            tot = e[0] + e[1] + e[2]
            acc = (e[0] / tot) * o_refs[0][h, rows, :]
            acc = acc + (e[1] / tot) * o_refs[1][h, rows, :]
            acc = acc + (e[2] / tot) * o_refs[2][h, rows, :]
            heads.append(acc.astype(BF16))
        merged = jnp.concatenate(heads, axis=1)
        b_out = jnp.dot(merged, wb_ref[...], preferred_element_type=F32)
        mixed_in = sga_ref[rows, :].astype(F32) * a_out + sgb_ref[rows, :].astype(F32) * b_out
        mixed = jnp.dot(mixed_in.astype(BF16), wo_ref[...], preferred_element_type=F32)
        x1 = _layer_norm(ALPHA * x_ref[rows, :] + mixed, g_ref[...], b_ref[...])
        for k in range(SLABS):
            x1g_ref[pl.ds(sub * SUB_ROWS * SLABS + k, SUB_ROWS, stride=SLABS), :] = x1[:, k * LANES:(k + 1) * LANES]
        x_hi = x1.astype(BF16)
        x_lo = (x1 - x_hi.astype(F32)).astype(BF16)
        logits = jnp.dot(x_hi, wrh_ref[...], preferred_element_type=F32)
        logits = logits + jnp.dot(x_lo, wrh_ref[...], preferred_element_type=F32)
        logits = logits + jnp.dot(x_hi, wrl_ref[...], preferred_element_type=F32)
        logit_ref[rows, :] = logits + br_ref[...]


def _mix_call(x, act, os_, lses, proj_a, wa, wb, wo, g, b, wrh, wrl, br):
    n = x.shape[0]
    tm = min(MIX_TM, n)
    row = lambda w: pl.BlockSpec((tm, w), lambda i: (i, 0))
    o_spec = pl.BlockSpec((HEADS_PER_GROUP, tm, LANES), lambda i: (0, i, 0))
    return pl.pallas_call(
        _mix_kernel,
        grid=(n // tm,),
        in_specs=[
            row(D_MODEL), row(SGU_WIDTH),
            o_spec, o_spec, o_spec,
            row(LANES), row(LANES), row(LANES),
            pl.BlockSpec((tm, D_MODEL), lambda i: (i, 0)),
            pl.BlockSpec((tm, D_MODEL), lambda i: (i, _A_GB // D_MODEL)),
            _resident(wa.shape), _resident(wb.shape), _resident(wo.shape),
            _resident(g.shape), _resident(b.shape),
            _resident(wrh.shape), _resident(wrl.shape), _resident(br.shape),
        ],
        out_specs=[pl.BlockSpec((tm * SLABS, LANES), lambda i: (i, 0)), row(ROUTER_LANES)],
        out_shape=[jax.ShapeDtypeStruct((n * SLABS, LANES), F32), jax.ShapeDtypeStruct((n, ROUTER_LANES), F32)],
        compiler_params=_params(("parallel",)),
        name="mix",
    )(x, act, *os_, *lses, proj_a, proj_a, wa, wb, wo, g, b, wrh, wrl, br)


def _start_row_gather(idx_ref, n_rows, src_hbm, dst, sem):
    def body(r, carry):
        src_row = pl.multiple_of(idx_ref[0, r] * SLABS, SLABS)
        dst_row = pl.multiple_of(r * SLABS, SLABS)
        pltpu.make_async_copy(src_hbm.at[pl.ds(src_row, SLABS)], dst.at[pl.ds(dst_row, SLABS)], sem).start()
        return carry
    lax.fori_loop(0, n_rows, body, 0, unroll=8)


def _wait_row_gather(n_rows, src_hbm, dst, sem):
    pltpu.make_async_copy(src_hbm.at[pl.ds(0, n_rows * SLABS)], dst, sem).wait()


def _gathered_rows(buf, first_row, n_rows):
    return jnp.concatenate(
        [buf[pl.ds(first_row * SLABS + k, n_rows, stride=SLABS), :] for k in range(SLABS)], axis=1)


def _expert_kernel(blk_e_ref, n_used_ref, tok_cur_ref, tok_next_ref, bw_ref, x_hbm, w1_ref, w3_ref, w2_ref,
                   o_ref, xbuf, sem):
    del blk_e_ref
    i = pl.program_id(0)
    n_used = n_used_ref[0]
    slot = i % 2

    @pl.when((i == 0) & (n_used > 0))
    def _():
        _start_row_gather(tok_cur_ref, MOE_BLK, x_hbm, xbuf.at[0], sem.at[0])

    @pl.when(i + 1 < n_used)
    def _():
        _start_row_gather(tok_next_ref, MOE_BLK, x_hbm, xbuf.at[1 - slot], sem.at[1 - slot])

    @pl.when(i < n_used)
    def _():
        _wait_row_gather(MOE_BLK, x_hbm, xbuf.at[slot], sem.at[slot])
        xb = _gathered_rows(xbuf.at[slot], 0, MOE_BLK).astype(BF16)
        h1 = jnp.dot(xb, w1_ref[...], preferred_element_type=F32)
        h3 = jnp.dot(xb, w3_ref[...], preferred_element_type=F32)
        hid = (jax.nn.silu(h1) * h3).astype(BF16)
        y = jnp.dot(hid, w2_ref[...], preferred_element_type=F32) * bw_ref[...]
        for k in range(SLABS):
            o_ref[pl.ds(k, MOE_BLK, stride=SLABS), :] = y[:, k * LANES:(k + 1) * LANES]

    @pl.when(i >= n_used)
    def _():
        o_ref[...] = jnp.zeros_like(o_ref)


def _expert_call(blk_e, n_used, buf_tok, buf_w, x1g, w1, w3, w2):
    n_blocks = blk_e.shape[0]
    d = D_MODEL
    tok3 = buf_tok.reshape(n_blocks, 1, MOE_BLK)
    grid_spec = pltpu.PrefetchScalarGridSpec(
        num_scalar_prefetch=2,
        grid=(n_blocks,),
        in_specs=[
            pl.BlockSpec((None, 1, MOE_BLK), lambda i, e, u: (i, 0, 0), memory_space=pltpu.SMEM),
            pl.BlockSpec((None, 1, MOE_BLK), lambda i, e, u: (jnp.minimum(i + 1, n_blocks - 1), 0, 0),
                         memory_space=pltpu.SMEM),
            pl.BlockSpec((MOE_BLK, 1), lambda i, e, u: (i, 0)),
            pl.BlockSpec(memory_space=pl.ANY),
            pl.BlockSpec((None, d, EXPERT_FF), lambda i, e, u: (e[i], 0, 0)),
            pl.BlockSpec((None, d, EXPERT_FF), lambda i, e, u: (e[i], 0, 0)),
            pl.BlockSpec((None, EXPERT_FF, d), lambda i, e, u: (e[i], 0, 0)),
        ],
        out_specs=pl.BlockSpec((MOE_BLK * SLABS, LANES), lambda i, e, u: (i, 0)),
        scratch_shapes=[pltpu.VMEM((2, MOE_BLK * SLABS, LANES), F32), pltpu.SemaphoreType.DMA((2,))],
    )
    return pl.pallas_call(
        _expert_kernel,
        grid_spec=grid_spec,
        out_shape=jax.ShapeDtypeStruct((n_blocks * MOE_BLK * SLABS, LANES), F32),
        compiler_params=_params(("arbitrary",)),
        name="experts",
    )(blk_e, n_used, tok3, tok3, buf_w.reshape(-1, 1), x1g, w1, w3, w2)


def _final_kernel(pos_cur_ref, pos_next_ref, x1g_ref, p_ref, yb_hbm, wpg_ref, wpp_ref, g_ref, b_ref,
                  x2_ref, x2b_ref, ybuf, sem):
    i = pl.program_id(0)
    n_steps = pl.num_programs(0)
    slot = i % 2
    tm = p_ref.shape[0]

    @pl.when(i == 0)
    def _():
        _start_row_gather(pos_cur_ref, TOP_K * tm, yb_hbm, ybuf.at[0], sem.at[0])

    @pl.when(i + 1 < n_steps)
    def _():
        _start_row_gather(pos_next_ref, TOP_K * tm, yb_hbm, ybuf.at[1 - slot], sem.at[1 - slot])

    x1s, pre = [], []
    for sub in range(tm // SUB_ROWS):
        rows = pl.ds(sub * SUB_ROWS, SUB_ROWS)
        x1 = _gathered_rows(x1g_ref, sub * SUB_ROWS, SUB_ROWS)
        gate = jax.nn.sigmoid(jnp.dot(x1.astype(BF16), wpg_ref[...], preferred_element_type=F32))
        emb = jnp.dot(p_ref[rows, :].astype(BF16), wpp_ref[...], preferred_element_type=F32)
        x1s.append(x1)
        pre.append(gate * emb)
    _wait_row_gather(TOP_K * tm, yb_hbm, ybuf.at[slot], sem.at[slot])
    for sub in range(tm // SUB_ROWS):
        rows = pl.ds(sub * SUB_ROWS, SUB_ROWS)
        y = (_gathered_rows(ybuf.at[slot], sub * SUB_ROWS, SUB_ROWS)
             + _gathered_rows(ybuf.at[slot], tm + sub * SUB_ROWS, SUB_ROWS))
        x2 = _layer_norm(ALPHA * x1s[sub] + y + pre[sub], g_ref[...], b_ref[...])
        x2_ref[rows, :] = x2
        x2b_ref[rows, :] = x2.astype(BF16)


def _final_call(pos, x1g, p, yb, wpg, wpp, g, b):
    n = p.shape[0]
    d = D_MODEL
    tm = min(FINAL_TM, n)
    steps = n // tm
    pos3 = pos.reshape(steps, tm, TOP_K).transpose(0, 2, 1).reshape(steps, 1, TOP_K * tm)
    row = lambda w: pl.BlockSpec((tm, w), lambda i: (i, 0))
    return pl.pallas_call(
        _final_kernel,
        grid=(steps,),
        in_specs=[
            pl.BlockSpec((None, 1, TOP_K * tm), lambda i: (i, 0, 0), memory_space=pltpu.SMEM),
            pl.BlockSpec((None, 1, TOP_K * tm), lambda i: (jnp.minimum(i + 1, steps - 1), 0, 0),
                         memory_space=pltpu.SMEM),
            pl.BlockSpec((tm * SLABS, LANES), lambda i: (i, 0)),
            row(PLE_DIM),
            pl.BlockSpec(memory_space=pl.ANY),
            _resident(wpg.shape), _resident(wpp.shape), _resident(g.shape), _resident(b.shape),
        ],
        out_specs=[row(d), row(d)],
        out_shape=[jax.ShapeDtypeStruct((n, d), F32), jax.ShapeDtypeStruct((n, d), BF16)],
        scratch_shapes=[pltpu.VMEM((2, TOP_K * tm * SLABS, LANES), F32), pltpu.SemaphoreType.DMA((2,))],
        compiler_params=_params(("arbitrary",)),
        name="final",
    )(pos3, pos3, x1g, p, yb, wpg, wpp, g, b)


def _route(logits):
    n = logits.shape[0]
    grp_logits = logits[:, :N_EXPERT_GROUPS]
    exp_logits = logits[:, N_EXPERT_GROUPS:N_EXPERT_GROUPS + N_EXPERTS].reshape(n, N_EXPERT_GROUPS, EXPERTS_PER_GROUP)
    grp_prob = jax.nn.softmax(grp_logits, axis=-1)
    g_sel = jnp.argmax(grp_logits, axis=-1).astype(jnp.int32)
    p_g = jnp.take_along_axis(grp_prob, g_sel[:, None], axis=1)[:, 0]
    in_grp = jnp.take_along_axis(exp_logits, g_sel[:, None, None], axis=1)[:, 0]
    top_v, top_i = lax.top_k(in_grp, TOP_K)
    gate = jax.nn.softmax(top_v, axis=-1) * p_g[:, None]
    e_id = g_sel[:, None] * EXPERTS_PER_GROUP + top_i.astype(jnp.int32)

    a = n * TOP_K
    e_flat = e_id.reshape(a)
    w_flat = gate.reshape(a)
    tok = jnp.arange(a, dtype=jnp.int32) // TOP_K
    onehot = (e_flat[:, None] == jnp.arange(N_EXPERTS, dtype=jnp.int32)[None, :]).astype(jnp.int32)
    csum = jnp.cumsum(onehot, axis=0)
    rank = jnp.take_along_axis(csum, e_flat[:, None], axis=1)[:, 0] - 1
    counts = csum[-1]
    padded = ((counts + MOE_BLK - 1) // MOE_BLK) * MOE_BLK
    pend = jnp.cumsum(padded)
    pstart = pend - padded
    dest = pstart[e_flat] + rank
    n_blocks = (a + N_EXPERTS * (MOE_BLK - 1) + MOE_BLK - 1) // MOE_BLK
    p_rows = n_blocks * MOE_BLK
    buf_tok = jnp.zeros((p_rows,), jnp.int32).at[dest].set(tok)
    buf_w = jnp.zeros((p_rows,), F32).at[dest].set(w_flat)
    blk_e = jnp.minimum(
        jnp.searchsorted(pend, jnp.arange(n_blocks, dtype=jnp.int32) * MOE_BLK, side="right"),
        N_EXPERTS - 1).astype(jnp.int32)
    n_used = (pend[-1:] // MOE_BLK).astype(jnp.int32)
    return blk_e, n_used, buf_tok, buf_w, dest.reshape(n, TOP_K)


def kernel(x, p, positions, w_in, w_s, b_s, ln_v_g, ln_v_b, w_a, w_b, w_o, ln1_g, ln1_b, w_grp, b_grp, w_rt, b_rt, w1, w3, w2, w_pg, w_pp, ln2_g, ln2_b):
    batch, seq, d = x.shape
    depth = w_in.shape[0]
    n = batch * seq
    assert d == D_MODEL and w_in.shape[2] == PROJ_WIDTH
    cosf, sinf = _rotary_tables(positions)
    causal = jnp.tril(jnp.ones((CHUNK, CHUNK), F32))
    xf = x.reshape(n, d)
    xb = xf.astype(BF16)
    pad_r = ROUTER_LANES - N_EXPERT_GROUPS - N_EXPERTS

    for i in range(depth):
        w_pa, w_pb = _split_proj_weights(w_in[i])
        proj_a = _proj_a_call(xb, w_pa)
        qkv = _proj_b_call(xb, w_pb, cosf, sinf)
        ws = (w_s[i] * causal).astype(BF16)
        bs = jnp.repeat(b_s[i].T, SGU_GROUP_CH, axis=1)
        act = _sgu_call(proj_a, ws, bs, ln_v_g[i][None, :], ln_v_b[i][None, :])
        att = [_attn_call(qkv, g, batch, seq) for g in range(N_DIL)]
        wr = jnp.concatenate([w_grp[i], w_rt[i], jnp.zeros((d, pad_r), F32)], axis=1)
        wr_hi = wr.astype(BF16)
        wr_lo = (wr - wr_hi.astype(F32)).astype(BF16)
        br = jnp.concatenate([b_grp[i], b_rt[i], jnp.zeros((pad_r,), F32)])[None, :]
        x1g, logits = _mix_call(xf, act, [a[0] for a in att], [a[1] for a in att], proj_a,
                                w_a[i].astype(BF16), w_b[i].astype(BF16), w_o[i].astype(BF16),
                                ln1_g[i][None, :], ln1_b[i][None, :], wr_hi, wr_lo, br)
        blk_e, n_used, buf_tok, buf_w, pos = _route(logits)
        yb = _expert_call(blk_e, n_used, buf_tok, buf_w, x1g,
                          w1[i].astype(BF16), w3[i].astype(BF16), w2[i].astype(BF16))
        xf, xb = _final_call(pos, x1g, p[i].reshape(n, PLE_DIM), yb, w_pg[i].astype(BF16), w_pp[i].astype(BF16),
                             ln2_g[i][None, :], ln2_b[i][None, :])
    return xf.reshape(batch, seq, d)
```

```python
import functools

import jax
import jax.numpy as jnp
from jax import lax
from jax.experimental import pallas as pl
from jax.experimental.pallas import tpu as pltpu

F32 = jnp.float32
BF16 = jnp.bfloat16

D_MODEL = 2048
DEPTH_FOR_DEEPNORM = 4
SGU_WIDTH = 1024
SGU_GROUPS = 8
SGU_GROUP_CH = SGU_WIDTH // SGU_GROUPS
CHUNK = 128
HEAD_DIM = 128
HEADS_PER_GROUP = 4
DILATED_GROUPS = ((128, 1), (512, 4), (2048, 16))
N_DIL = len(DILATED_GROUPS)
ATT_HEADS = HEADS_PER_GROUP * N_DIL
ATT_WIDTH = ATT_HEADS * HEAD_DIM
QBLK = 128
ROT_DIM = HEAD_DIM // 4
ROT_HALF = ROT_DIM // 2
ROPE_THETA = 500000.0
PROJ_WIDTH = 2 * SGU_WIDTH + 3 * ATT_WIDTH + 2 * D_MODEL
N_EXPERT_GROUPS = 4
EXPERTS_PER_GROUP = 8
N_EXPERTS = N_EXPERT_GROUPS * EXPERTS_PER_GROUP
TOP_K = 2
EXPERT_FF = 512
MOE_BLK = 256
PLE_DIM = 256
ALPHA = (2 * DEPTH_FOR_DEEPNORM) ** 0.25
LN_EPS = 1e-5
ROUTER_LANES = 128

LANES = 128
SLABS = D_MODEL // LANES
ROW_PITCH = SLABS + 1

PROJ_TM = 1024
PROJ_A_TN = 1024
PROJ_B_TN = ATT_WIDTH
PROJ_CHUNK = 256
SGU_TM = 512
MIX_TM = 256
FINAL_TM = 256
SUB_ROWS = 128
ATTN_STEP = {1: (8, 4), 4: (2, 4), 16: (1, 2)}
VMEM_LIMIT = 56 * 1024 * 1024

_REF_SPLITS = (0, SGU_WIDTH, 2 * SGU_WIDTH, 2 * SGU_WIDTH + ATT_WIDTH, 2 * SGU_WIDTH + 2 * ATT_WIDTH,
               2 * SGU_WIDTH + 3 * ATT_WIDTH, 2 * SGU_WIDTH + 3 * ATT_WIDTH + D_MODEL, PROJ_WIDTH)
_A_GB = D_MODEL
_A_U = 2 * D_MODEL
_A_V = 2 * D_MODEL + SGU_WIDTH
PROJ_A_WIDTH = 2 * D_MODEL + 2 * SGU_WIDTH


def _params(sem, vmem=VMEM_LIMIT):
    return pltpu.CompilerParams(dimension_semantics=sem, vmem_limit_bytes=vmem)


def _resident(shape):
    nd = len(shape)
    return pl.BlockSpec(shape, lambda *_: (0,) * nd, pipeline_mode=pl.Buffered(1))


def _layer_norm(y, g, b):
    mu = jnp.mean(y, axis=-1, keepdims=True)
    yc = y - mu
    var = jnp.mean(yc * yc, axis=-1, keepdims=True)
    return yc * lax.rsqrt(var + LN_EPS) * g + b


def _split_proj_weights(w):
    sec = [w[:, :, _REF_SPLITS[k]:_REF_SPLITS[k + 1]] for k in range(7)]
    wa = jnp.concatenate([sec[5], sec[6], sec[0], sec[1]], axis=2).astype(BF16)
    wb = w[:, :, _REF_SPLITS[2]:_REF_SPLITS[5]].astype(BF16)
    return wa, wb


def _rotary_tables(positions):
    inv_freq = ROPE_THETA ** (-jnp.arange(0, ROT_DIM, 2, dtype=F32) / ROT_DIM)
    ang = positions.astype(F32).reshape(-1)[:, None] * inv_freq
    cos, sin = jnp.cos(ang), jnp.sin(ang)
    n = ang.shape[0]
    cosf = jnp.concatenate([cos, cos, jnp.ones((n, HEAD_DIM - ROT_DIM), F32)], axis=1)
    sinf = jnp.concatenate([-sin, sin, jnp.zeros((n, HEAD_DIM - ROT_DIM), F32)], axis=1)
    return cosf, sinf


def _proj_a_kernel(x_ref, w_ref, o_ref):
    j = pl.program_id(1)
    tn = o_ref.shape[1]

    def run(act):
        for c in range(tn // PROJ_CHUNK):
            cols = slice(c * PROJ_CHUNK, (c + 1) * PROJ_CHUNK)
            acc = jnp.dot(x_ref[...], w_ref[:, cols], preferred_element_type=F32)
            o_ref[:, cols] = act(acc).astype(o_ref.dtype)

    @pl.when(j < _A_U // tn)
    def _():
        run(jax.nn.sigmoid)

    @pl.when(j >= _A_U // tn)
    def _():
        run(jax.nn.gelu)


def _proj_a_call(xb, w):
    n, d = xb.shape
    tm = min(PROJ_TM, n)
    tn = PROJ_A_TN
    assert _A_U % tn == 0 and PROJ_A_WIDTH % tn == 0
    return pl.pallas_call(
        _proj_a_kernel,
        grid=(n // tm, PROJ_A_WIDTH // tn),
        in_specs=[
            pl.BlockSpec((tm, d), lambda i, j: (i, 0)),
            pl.BlockSpec((d, tn), lambda i, j: (0, j)),
        ],
        out_specs=pl.BlockSpec((tm, tn), lambda i, j: (i, j)),
        out_shape=jax.ShapeDtypeStruct((n, PROJ_A_WIDTH), BF16),
        compiler_params=_params(("parallel", "arbitrary")),
        name="proj_a",
    )(xb, w)


def _proj_b_kernel(x_ref, w_ref, cos_ref, sin_ref, o_ref):
    j = pl.program_id(1)
    heads_per_chunk = PROJ_CHUNK // HEAD_DIM
    lane = lax.broadcasted_iota(jnp.int32, cos_ref.shape, 1)

    def run(rotate):
        for c in range(PROJ_B_TN // PROJ_CHUNK):
            cols = slice(c * PROJ_CHUNK, (c + 1) * PROJ_CHUNK)
            acc = jnp.dot(x_ref[...], w_ref[:, cols], preferred_element_type=F32)
            for h in range(heads_per_chunk):
                xh = acc[:, h * HEAD_DIM:(h + 1) * HEAD_DIM]
                if rotate:
                    partner = jnp.where(lane < ROT_HALF, pltpu.roll(xh, HEAD_DIM - ROT_HALF, 1),
                                        pltpu.roll(xh, ROT_HALF, 1))
                    xh = xh * cos_ref[...] + partner * sin_ref[...]
                o_ref[c * heads_per_chunk + h] = xh

    @pl.when(j < 2)
    def _():
        run(True)

    @pl.when(j >= 2)
    def _():
        run(False)


def _proj_b_call(xb, w, cosf, sinf):
    n, d = xb.shape
    tm = min(PROJ_TM, n)
    slabs = PROJ_B_TN // HEAD_DIM
    return pl.pallas_call(
        _proj_b_kernel,
        grid=(n // tm, 3),
        in_specs=[
            pl.BlockSpec((tm, d), lambda i, j: (i, 0)),
            pl.BlockSpec((d, PROJ_B_TN), lambda i, j: (0, j)),
            pl.BlockSpec((tm, HEAD_DIM), lambda i, j: (i, 0)),
            pl.BlockSpec((tm, HEAD_DIM), lambda i, j: (i, 0)),
        ],
        out_specs=pl.BlockSpec((slabs, tm, HEAD_DIM), lambda i, j: (j, i, 0)),
        out_shape=jax.ShapeDtypeStruct((3 * slabs, n, HEAD_DIM), F32),
        compiler_params=_params(("parallel", "arbitrary")),
        name="proj_b",
    )(xb, w, cosf, sinf)


def _sgu_kernel(u_ref, v_ref, ws_ref, bs_ref, g_ref, b_ref, o_ref):
    vn = _layer_norm(v_ref[...].astype(F32), g_ref[...], b_ref[...]).astype(BF16)
    tm = o_ref.shape[0]
    for c in range(tm // CHUNK):
        rows = slice(c * CHUNK, (c + 1) * CHUNK)
        for g in range(SGU_GROUPS):
            cols = slice(g * SGU_GROUP_CH, (g + 1) * SGU_GROUP_CH)
            z = jnp.dot(ws_ref[g], vn[rows, cols], preferred_element_type=F32) + bs_ref[:, cols]
            o_ref[rows, cols] = (u_ref[rows, cols].astype(F32) * z).astype(o_ref.dtype)


def _sgu_call(proj_a, ws, bs, ln_g, ln_b):
    n = proj_a.shape[0]
    tm = min(SGU_TM, n)
    u_blk, v_blk = _A_U // SGU_WIDTH, _A_V // SGU_WIDTH
    return pl.pallas_call(
        _sgu_kernel,
        grid=(n // tm,),
        in_specs=[
            pl.BlockSpec((tm, SGU_WIDTH), lambda i: (i, u_blk)),
            pl.BlockSpec((tm, SGU_WIDTH), lambda i: (i, v_blk)),
            _resident(ws.shape),
            _resident(bs.shape),
            _resident(ln_g.shape),
            _resident(ln_b.shape),
        ],
        out_specs=pl.BlockSpec((tm, SGU_WIDTH), lambda i: (i, 0)),
        out_shape=jax.ShapeDtypeStruct((n, SGU_WIDTH), BF16),
        compiler_params=_params(("parallel",)),
        name="sgu",
    )(proj_a, proj_a, ws, bs, ln_g, ln_b)


def _attn_kernel(q_ref, kp_ref, k_ref, vp_ref, v_ref, o_ref, lse_ref, *, dil, span, nq, hps):
    slab_idx = pl.program_id(1)
    hc = pl.program_id(2)
    qi = lax.broadcasted_iota(jnp.int32, (QBLK, QBLK), 0)
    kj = lax.broadcasted_iota(jnp.int32, (QBLK, QBLK), 1)
    dist_prev = qi + QBLK - kj
    dist_cur = qi - kj
    in_prev = (dist_prev >= 0) & (dist_prev <= span)
    mask_cur = (dist_cur >= 0) & (dist_cur <= span)
    scale = HEAD_DIM ** -0.5
    nt = (((1,), (1,)), ((), ()))
    lane = lax.broadcasted_iota(jnp.int32, (QBLK, LANES), 1)

    @pl.when(hc == 0)
    def _():
        lse_ref[...] = jnp.zeros_like(lse_ref)

    def rows(r, blk):
        return pl.ds(blk * QBLK * dil + r, QBLK, stride=dil) if dil > 1 else pl.ds(blk * QBLK, QBLK)

    def one_class(r):
        for blk in range(nq):
            if blk == 0:
                mask_prev = in_prev & (slab_idx > 0)
            else:
                mask_prev = in_prev
            sel = rows(r, blk)
            lse_rows = lse_ref[sel, :]
            for h in range(hps):
                q = q_ref.at[h][sel, :].astype(BF16)
                if blk == 0:
                    kp = kp_ref.at[h][rows(r, 0), :].astype(BF16)
                    vp = vp_ref.at[h][rows(r, 0), :].astype(BF16)
                else:
                    kp = k_ref.at[h][rows(r, blk - 1), :].astype(BF16)
                    vp = v_ref.at[h][rows(r, blk - 1), :].astype(BF16)
                kc = k_ref.at[h][sel, :].astype(BF16)
                vc = v_ref.at[h][sel, :].astype(BF16)
                sp = lax.dot_general(q, kp, nt, preferred_element_type=F32) * scale
                sc = lax.dot_general(q, kc, nt, preferred_element_type=F32) * scale
                sp = jnp.where(mask_prev, sp, -jnp.inf)
                sc = jnp.where(mask_cur, sc, -jnp.inf)
                m = jnp.maximum(jnp.max(sp, axis=-1, keepdims=True), jnp.max(sc, axis=-1, keepdims=True))
                ep = jnp.exp(sp - m)
                ec = jnp.exp(sc - m)
                den = jnp.sum(ep, axis=-1, keepdims=True) + jnp.sum(ec, axis=-1, keepdims=True)
                o = jnp.dot(ep.astype(BF16), vp, preferred_element_type=F32)
                o = o + jnp.dot(ec.astype(BF16), vc, preferred_element_type=F32)
                o_ref.at[h][sel, :] = o / den
                lse_rows = jnp.where(lane == hc * hps + h, m + jnp.log(den), lse_rows)
            lse_ref[sel, :] = lse_rows

    if dil == 1:
        one_class(0)
    else:
        def body(r, carry):
            one_class(r)
            return carry
        lax.fori_loop(0, dil, body, 0)


def _attn_call(qkv, group, batch, seq):
    window, dil = DILATED_GROUPS[group]
    span = window // dil
    nq, hps = ATTN_STEP[dil]
    slab = nq * QBLK * dil
    prev = QBLK * dil
    assert seq % slab == 0 and span <= QBLK and HEADS_PER_GROUP % hps == 0
    slabs_per_seq = seq // slab
    prev_per_seq = seq // prev
    n = batch * seq
    hblocks = HEADS_PER_GROUP // hps

    def cur(section):
        base = (section * ATT_HEADS + group * HEADS_PER_GROUP) // hps
        return pl.BlockSpec((hps, slab, LANES), lambda b, s, hc: (base + hc, b * slabs_per_seq + s, 0))

    def before(section):
        base = (section * ATT_HEADS + group * HEADS_PER_GROUP) // hps
        return pl.BlockSpec(
            (hps, prev, LANES),
            lambda b, s, hc: (base + hc, b * prev_per_seq + jnp.maximum(s * nq - 1, 0), 0))

    return pl.pallas_call(
        functools.partial(_attn_kernel, dil=dil, span=span, nq=nq, hps=hps),
        grid=(batch, slabs_per_seq, hblocks),
        in_specs=[cur(0), before(1), cur(1), before(2), cur(2)],
        out_specs=[
            pl.BlockSpec((hps, slab, LANES), lambda b, s, hc: (hc, b * slabs_per_seq + s, 0)),
            pl.BlockSpec((slab, LANES), lambda b, s, hc: (b * slabs_per_seq + s, 0)),
        ],
        out_shape=[
            jax.ShapeDtypeStruct((HEADS_PER_GROUP, n, LANES), F32),
            jax.ShapeDtypeStruct((n, LANES), F32),
        ],
        compiler_params=_params(("parallel", "arbitrary", "arbitrary")),
        name=f"attn_d{dil}",
    )(qkv, qkv, qkv, qkv, qkv)


def _mix_kernel(x_ref, act_ref, o0_ref, o1_ref, o2_ref, l0_ref, l1_ref, l2_ref, sga_ref, sgb_ref,
                wa_ref, wb_ref, wo_ref, g_ref, b_ref, wrh_ref, wrl_ref, br_ref, x1g_ref, logit_ref):
    tm = x_ref.shape[0]
    o_refs = (o0_ref, o1_ref, o2_ref)
    l_refs = (l0_ref, l1_ref, l2_ref)
    for sub in range(tm // SUB_ROWS):
        rows = pl.ds(sub * SUB_ROWS, SUB_ROWS)
        a_out = jnp.dot(act_ref[rows, :], wa_ref[...], preferred_element_type=F32)
        lses = [l[rows, :] for l in l_refs]
        heads = []
        for h in range(HEADS_PER_GROUP):
            lh = [l[:, h:h + 1] for l in lses]
            m = jnp.maximum(jnp.maximum(lh[0], lh[1]), lh[2])
            e = [jnp.exp(v - m) for v in lh]
            tot = e[0] + e[1] + e[2]
            acc = (e[0] / tot) * o_refs[0][h, rows, :]
            acc = acc + (e[1] / tot) * o_refs[1][h, rows, :]
            acc = acc + (e[2] / tot) * o_refs[2][h, rows, :]
            heads.append(acc.astype(BF16))
        merged = jnp.concatenate(heads, axis=1)
        b_out = jnp.dot(merged, wb_ref[...], preferred_element_type=F32)
        mixed_in = sga_ref[rows, :].astype(F32) * a_out + sgb_ref[rows, :].astype(F32) * b_out
        mixed = jnp.dot(mixed_in.astype(BF16), wo_ref[...], preferred_element_type=F32)
        x1 = _layer_norm(ALPHA * x_ref[rows, :] + mixed, g_ref[...], b_ref[...])
        _store_gather_rows(x1g_ref, sub * SUB_ROWS, x1)
        x_hi = x1.astype(BF16)
        x_lo = (x1 - x_hi.astype(F32)).astype(BF16)
        logits = jnp.dot(x_hi, wrh_ref[...], preferred_element_type=F32)
        logits = logits + jnp.dot(x_lo, wrh_ref[...], preferred_element_type=F32)
        logits = logits + jnp.dot(x_hi, wrl_ref[...], preferred_element_type=F32)
        logit_ref[rows, :] = logits + br_ref[...]


def _mix_call(x, act, os_, lses, proj_a, wa, wb, wo, g, b, wrh, wrl, br):
    n = x.shape[0]
    tm = min(MIX_TM, n)
    row = lambda w: pl.BlockSpec((tm, w), lambda i: (i, 0))
    o_spec = pl.BlockSpec((HEADS_PER_GROUP, tm, LANES), lambda i: (0, i, 0))
    return pl.pallas_call(
        _mix_kernel,
        grid=(n // tm,),
        in_specs=[
            row(D_MODEL), row(SGU_WIDTH),
            o_spec, o_spec, o_spec,
            row(LANES), row(LANES), row(LANES),
            pl.BlockSpec((tm, D_MODEL), lambda i: (i, 0)),
            pl.BlockSpec((tm, D_MODEL), lambda i: (i, _A_GB // D_MODEL)),
            _resident(wa.shape), _resident(wb.shape), _resident(wo.shape),
            _resident(g.shape), _resident(b.shape),
            _resident(wrh.shape), _resident(wrl.shape), _resident(br.shape),
        ],
        out_specs=[pl.BlockSpec((tm * ROW_PITCH, LANES), lambda i: (i, 0)), row(ROUTER_LANES)],
        out_shape=[jax.ShapeDtypeStruct((n * ROW_PITCH, LANES), F32), jax.ShapeDtypeStruct((n, ROUTER_LANES), F32)],
        compiler_params=_params(("parallel",)),
        name="mix",
    )(x, act, *os_, *lses, proj_a, proj_a, wa, wb, wo, g, b, wrh, wrl, br)


def _start_row_gather(idx_ref, n_rows, src_hbm, dst, sem):
    def body(r, carry):
        src = src_hbm.at[pl.ds(idx_ref[0, r] * ROW_PITCH, SLABS)]
        pltpu.make_async_copy(src, dst.at[pl.ds(r * ROW_PITCH, SLABS)], sem).start()
        return carry
    lax.fori_loop(0, n_rows, body, 0, unroll=8)


def _wait_row_gather(n_rows, src_hbm, dst, sem):
    pltpu.make_async_copy(src_hbm.at[pl.ds(0, n_rows * SLABS)], dst.at[pl.ds(0, n_rows * SLABS)], sem).wait()


def _gathered_rows(buf, first_row, n_rows):
    return jnp.concatenate(
        [buf[pl.ds(first_row * ROW_PITCH + k, n_rows, stride=ROW_PITCH), :] for k in range(SLABS)], axis=1)


def _store_gather_rows(ref, first_row, val):
    n_rows = val.shape[0]
    for k in range(SLABS):
        ref[pl.ds(first_row * ROW_PITCH + k, n_rows, stride=ROW_PITCH), :] = val[:, k * LANES:(k + 1) * LANES]
    ref[pl.ds(first_row * ROW_PITCH + SLABS, n_rows, stride=ROW_PITCH), :] = jnp.zeros((n_rows, LANES), val.dtype)


def _expert_kernel(blk_e_ref, n_used_ref, tok_cur_ref, tok_next_ref, bw_ref, x_hbm, w1_ref, w3_ref, w2_ref,
                   o_ref, xbuf, w1b, w3b, w2b, sem):
    i = pl.program_id(0)
    n_used = n_used_ref[0]
    slot = i % 2

    @pl.when((i == 0) & (n_used > 0))
    def _():
        _start_row_gather(tok_cur_ref, MOE_BLK, x_hbm, xbuf.at[0], sem.at[0])

    @pl.when(i + 1 < n_used)
    def _():
        _start_row_gather(tok_next_ref, MOE_BLK, x_hbm, xbuf.at[1 - slot], sem.at[1 - slot])

    @pl.when((i == 0) | (blk_e_ref[i] != blk_e_ref[jnp.maximum(i - 1, 0)]))
    def _():
        w1b[...] = w1_ref[...].astype(BF16)
        w3b[...] = w3_ref[...].astype(BF16)
        w2b[...] = w2_ref[...].astype(BF16)

    @pl.when(i < n_used)
    def _():
        _wait_row_gather(MOE_BLK, x_hbm, xbuf.at[slot], sem.at[slot])
        xb = _gathered_rows(xbuf.at[slot], 0, MOE_BLK).astype(BF16)
        h1 = jnp.dot(xb, w1b[...], preferred_element_type=F32)
        h3 = jnp.dot(xb, w3b[...], preferred_element_type=F32)
        hid = (jax.nn.silu(h1) * h3).astype(BF16)
        y = jnp.dot(hid, w2b[...], preferred_element_type=F32) * bw_ref[...]
        _store_gather_rows(o_ref, 0, y)

    @pl.when(i >= n_used)
    def _():
        o_ref[...] = jnp.zeros_like(o_ref)


def _expert_call(blk_e, n_used, buf_tok, buf_w, x1g, w1, w3, w2, layer):
    n_blocks = blk_e.shape[0]
    d = D_MODEL
    tok3 = buf_tok.reshape(n_blocks, 1, MOE_BLK)
    grid_spec = pltpu.PrefetchScalarGridSpec(
        num_scalar_prefetch=2,
        grid=(n_blocks,),
        in_specs=[
            pl.BlockSpec((None, 1, MOE_BLK), lambda i, e, u: (i, 0, 0), memory_space=pltpu.SMEM),
            pl.BlockSpec((None, 1, MOE_BLK), lambda i, e, u: (jnp.minimum(i + 1, n_blocks - 1), 0, 0),
                         memory_space=pltpu.SMEM),
            pl.BlockSpec((MOE_BLK, 1), lambda i, e, u: (i, 0)),
            pl.BlockSpec(memory_space=pl.ANY),
            pl.BlockSpec((None, None, d, EXPERT_FF), lambda i, e, u: (layer, e[i], 0, 0)),
            pl.BlockSpec((None, None, d, EXPERT_FF), lambda i, e, u: (layer, e[i], 0, 0)),
            pl.BlockSpec((None, None, EXPERT_FF, d), lambda i, e, u: (layer, e[i], 0, 0)),
        ],
        out_specs=pl.BlockSpec((MOE_BLK * ROW_PITCH, LANES), lambda i, e, u: (i, 0)),
        scratch_shapes=[pltpu.VMEM((2, MOE_BLK * ROW_PITCH, LANES), F32),
                        pltpu.VMEM((d, EXPERT_FF), BF16), pltpu.VMEM((d, EXPERT_FF), BF16),
                        pltpu.VMEM((EXPERT_FF, d), BF16), pltpu.SemaphoreType.DMA((2,))],
    )
    return pl.pallas_call(
        _expert_kernel,
        grid_spec=grid_spec,
        out_shape=jax.ShapeDtypeStruct((n_blocks * MOE_BLK * ROW_PITCH, LANES), F32),
        compiler_params=_params(("arbitrary",)),
        name="experts",
    )(blk_e, n_used, tok3, tok3, buf_w.reshape(-1, 1), x1g, w1, w3, w2)


def _final_kernel(pos_cur_ref, pos_next_ref, x1g_ref, p_ref, yb_hbm, wpg_ref, wpp_ref, g_ref, b_ref,
                  x2_ref, x2b_ref, ybuf, sem):
    i = pl.program_id(0)
    n_steps = pl.num_programs(0)
    slot = i % 2
    tm = p_ref.shape[0]

    @pl.when(i == 0)
    def _():
        _start_row_gather(pos_cur_ref, TOP_K * tm, yb_hbm, ybuf.at[0], sem.at[0])

    @pl.when(i + 1 < n_steps)
    def _():
        _start_row_gather(pos_next_ref, TOP_K * tm, yb_hbm, ybuf.at[1 - slot], sem.at[1 - slot])

    _wait_row_gather(TOP_K * tm, yb_hbm, ybuf.at[slot], sem.at[slot])
    for sub in range(tm // SUB_ROWS):
        rows = pl.ds(sub * SUB_ROWS, SUB_ROWS)
        x1 = _gathered_rows(x1g_ref, sub * SUB_ROWS, SUB_ROWS)
        gate = jax.nn.sigmoid(jnp.dot(x1.astype(BF16), wpg_ref[...], preferred_element_type=F32))
        emb = jnp.dot(p_ref[rows, :].astype(BF16), wpp_ref[...], preferred_element_type=F32)
        y = (_gathered_rows(ybuf.at[slot], sub * SUB_ROWS, SUB_ROWS)
             + _gathered_rows(ybuf.at[slot], tm + sub * SUB_ROWS, SUB_ROWS))
        x2 = _layer_norm(ALPHA * x1 + y + gate * emb, g_ref[...], b_ref[...])
        x2_ref[rows, :] = x2
        x2b_ref[rows, :] = x2.astype(BF16)


def _final_call(pos, x1g, p, yb, wpg, wpp, g, b):
    n = p.shape[0]
    d = D_MODEL
    tm = min(FINAL_TM, n)
    steps = n // tm
    pos3 = pos.reshape(steps, tm, TOP_K).transpose(0, 2, 1).reshape(steps, 1, TOP_K * tm)
    row = lambda w: pl.BlockSpec((tm, w), lambda i: (i, 0))
    return pl.pallas_call(
        _final_kernel,
        grid=(steps,),
        in_specs=[
            pl.BlockSpec((None, 1, TOP_K * tm), lambda i: (i, 0, 0), memory_space=pltpu.SMEM),
            pl.BlockSpec((None, 1, TOP_K * tm), lambda i: (jnp.minimum(i + 1, steps - 1), 0, 0),
                         memory_space=pltpu.SMEM),
            pl.BlockSpec((tm * ROW_PITCH, LANES), lambda i: (i, 0)),
            row(PLE_DIM),
            pl.BlockSpec(memory_space=pl.ANY),
            _resident(wpg.shape), _resident(wpp.shape), _resident(g.shape), _resident(b.shape),
        ],
        out_specs=[row(d), row(d)],
        out_shape=[jax.ShapeDtypeStruct((n, d), F32), jax.ShapeDtypeStruct((n, d), BF16)],
        scratch_shapes=[pltpu.VMEM((2, TOP_K * tm * ROW_PITCH, LANES), F32), pltpu.SemaphoreType.DMA((2,))],
        compiler_params=_params(("arbitrary",)),
        name="final",
    )(pos3, pos3, x1g, p, yb, wpg, wpp, g, b)


def _route(logits):
    n = logits.shape[0]
    grp_logits = logits[:, :N_EXPERT_GROUPS]
    exp_logits = logits[:, N_EXPERT_GROUPS:N_EXPERT_GROUPS + N_EXPERTS].reshape(n, N_EXPERT_GROUPS, EXPERTS_PER_GROUP)
    grp_prob = jax.nn.softmax(grp_logits, axis=-1)
    g_sel = jnp.argmax(grp_logits, axis=-1).astype(jnp.int32)
    p_g = jnp.take_along_axis(grp_prob, g_sel[:, None], axis=1)[:, 0]
    in_grp = jnp.take_along_axis(exp_logits, g_sel[:, None, None], axis=1)[:, 0]
    top_v, top_i = lax.top_k(in_grp, TOP_K)
    gate = jax.nn.softmax(top_v, axis=-1) * p_g[:, None]
    e_id = g_sel[:, None] * EXPERTS_PER_GROUP + top_i.astype(jnp.int32)

    a = n * TOP_K
    e_flat = e_id.reshape(a)
    w_flat = gate.reshape(a)
    tok = jnp.arange(a, dtype=jnp.int32) // TOP_K
    onehot = (e_flat[:, None] == jnp.arange(N_EXPERTS, dtype=jnp.int32)[None, :]).astype(jnp.int32)
    csum = jnp.cumsum(onehot, axis=0)
    rank = jnp.take_along_axis(csum, e_flat[:, None], axis=1)[:, 0] - 1
    counts = csum[-1]
    padded = ((counts + MOE_BLK - 1) // MOE_BLK) * MOE_BLK
    pend = jnp.cumsum(padded)
    pstart = pend - padded
    dest = pstart[e_flat] + rank
    n_blocks = (a + N_EXPERTS * (MOE_BLK - 1) + MOE_BLK - 1) // MOE_BLK
    p_rows = n_blocks * MOE_BLK
    buf_tok = jnp.zeros((p_rows,), jnp.int32).at[dest].set(tok)
    buf_w = jnp.zeros((p_rows,), F32).at[dest].set(w_flat)
    blk_e = jnp.minimum(
        jnp.searchsorted(pend, jnp.arange(n_blocks, dtype=jnp.int32) * MOE_BLK, side="right"),
        N_EXPERTS - 1).astype(jnp.int32)
    n_used = (pend[-1:] // MOE_BLK).astype(jnp.int32)
    return blk_e, n_used, buf_tok, buf_w, dest.reshape(n, TOP_K)


def kernel(x, p, positions, w_in, w_s, b_s, ln_v_g, ln_v_b, w_a, w_b, w_o, ln1_g, ln1_b, w_grp, b_grp, w_rt, b_rt, w1, w3, w2, w_pg, w_pp, ln2_g, ln2_b):
    batch, seq, d = x.shape
    depth = w_in.shape[0]
    n = batch * seq
    assert d == D_MODEL and w_in.shape[2] == PROJ_WIDTH
    cosf, sinf = _rotary_tables(positions)
    causal = jnp.tril(jnp.ones((CHUNK, CHUNK), F32))
    xf = x.reshape(n, d)
    xb = xf.astype(BF16)
    pad_r = ROUTER_LANES - N_EXPERT_GROUPS - N_EXPERTS
    w_pa, w_pb = _split_proj_weights(w_in)

    for i in range(depth):
        proj_a = _proj_a_call(xb, w_pa[i])
        qkv = _proj_b_call(xb, w_pb[i], cosf, sinf)
        ws = (w_s[i] * causal).astype(BF16)
        bs = jnp.repeat(b_s[i].T, SGU_GROUP_CH, axis=1)
        act = _sgu_call(proj_a, ws, bs, ln_v_g[i][None, :], ln_v_b[i][None, :])
        att = [_attn_call(qkv, g, batch, seq) for g in range(N_DIL)]
        wr = jnp.concatenate([w_grp[i], w_rt[i], jnp.zeros((d, pad_r), F32)], axis=1)
        wr_hi = wr.astype(BF16)
        wr_lo = (wr - wr_hi.astype(F32)).astype(BF16)
        br = jnp.concatenate([b_grp[i], b_rt[i], jnp.zeros((pad_r,), F32)])[None, :]
        x1g, logits = _mix_call(xf, act, [a[0] for a in att], [a[1] for a in att], proj_a,
                                w_a[i].astype(BF16), w_b[i].astype(BF16), w_o[i].astype(BF16),
                                ln1_g[i][None, :], ln1_b[i][None, :], wr_hi, wr_lo, br)
        blk_e, n_used, buf_tok, buf_w, pos = _route(logits)
        yb = _expert_call(blk_e, n_used, buf_tok, buf_w, x1g, w1, w3, w2, i)
        xf, xb = _final_call(pos, x1g, p[i].reshape(n, PLE_DIM), yb, w_pg[i].astype(BF16), w_pp[i].astype(BF16),
                             ln2_g[i][None, :], ln2_b[i][None, :])
    return xf.reshape(batch, seq, d)
```

```python
import functools

import jax
import jax.numpy as jnp
from jax import lax
from jax.experimental import pallas as pl
from jax.experimental.pallas import tpu as pltpu

F32 = jnp.float32
BF16 = jnp.bfloat16

D_MODEL = 2048
DEPTH_FOR_DEEPNORM = 4
SGU_WIDTH = 1024
SGU_GROUPS = 8
SGU_GROUP_CH = SGU_WIDTH // SGU_GROUPS
CHUNK = 128
HEAD_DIM = 128
HEADS_PER_GROUP = 4
DILATED_GROUPS = ((128, 1), (512, 4), (2048, 16))
N_DIL = len(DILATED_GROUPS)
ATT_HEADS = HEADS_PER_GROUP * N_DIL
ATT_WIDTH = ATT_HEADS * HEAD_DIM
QBLK = 128
ROT_DIM = HEAD_DIM // 4
ROT_HALF = ROT_DIM // 2
ROPE_THETA = 500000.0
PROJ_WIDTH = 2 * SGU_WIDTH + 3 * ATT_WIDTH + 2 * D_MODEL
N_EXPERT_GROUPS = 4
EXPERTS_PER_GROUP = 8
N_EXPERTS = N_EXPERT_GROUPS * EXPERTS_PER_GROUP
TOP_K = 2
EXPERT_FF = 512
MOE_BLK = 256
PLE_DIM = 256
ALPHA = (2 * DEPTH_FOR_DEEPNORM) ** 0.25
LN_EPS = 1e-5
ROUTER_LANES = 128

LANES = 128
SLABS = D_MODEL // LANES
ROW_PITCH = SLABS + 1

PROJ_TM = 1024
PROJ_A_TN = 1024
PROJ_B_TN = ATT_WIDTH
PROJ_CHUNK = 256
SGU_TM = 512
MIX_TM = 256
FINAL_TM = 256
SUB_ROWS = 128
EXPERT_CHUNK = 256
EXPERT_OUT_CHUNK = 512
FINAL_CHUNK = 512
ATTN_STEP = {1: (8, 4), 4: (2, 4), 16: (1, 2)}
VMEM_LIMIT = 56 * 1024 * 1024

_REF_SPLITS = (0, SGU_WIDTH, 2 * SGU_WIDTH, 2 * SGU_WIDTH + ATT_WIDTH, 2 * SGU_WIDTH + 2 * ATT_WIDTH,
               2 * SGU_WIDTH + 3 * ATT_WIDTH, 2 * SGU_WIDTH + 3 * ATT_WIDTH + D_MODEL, PROJ_WIDTH)
_A_GB = D_MODEL
_A_U = 2 * D_MODEL
_A_V = 2 * D_MODEL + SGU_WIDTH
PROJ_A_WIDTH = 2 * D_MODEL + 2 * SGU_WIDTH


def _params(sem, vmem=VMEM_LIMIT):
    return pltpu.CompilerParams(dimension_semantics=sem, vmem_limit_bytes=vmem)


def _resident(shape):
    nd = len(shape)
    return pl.BlockSpec(shape, lambda *_: (0,) * nd, pipeline_mode=pl.Buffered(1))


def _layer_norm(y, g, b):
    mu = jnp.mean(y, axis=-1, keepdims=True)
    yc = y - mu
    var = jnp.mean(yc * yc, axis=-1, keepdims=True)
    return yc * lax.rsqrt(var + LN_EPS) * g + b


def _split_proj_weights(w):
    sec = [w[:, :, _REF_SPLITS[k]:_REF_SPLITS[k + 1]] for k in range(7)]
    wa = jnp.concatenate([sec[5], sec[6], sec[0], sec[1]], axis=2).astype(BF16)
    wb = w[:, :, _REF_SPLITS[2]:_REF_SPLITS[5]].astype(BF16)
    return wa, wb


def _rotary_tables(positions):
    inv_freq = ROPE_THETA ** (-jnp.arange(0, ROT_DIM, 2, dtype=F32) / ROT_DIM)
    ang = positions.astype(F32).reshape(-1)[:, None] * inv_freq
    cos, sin = jnp.cos(ang), jnp.sin(ang)
    n = ang.shape[0]
    cosf = jnp.concatenate([cos, cos, jnp.ones((n, HEAD_DIM - ROT_DIM), F32)], axis=1)
    sinf = jnp.concatenate([-sin, sin, jnp.zeros((n, HEAD_DIM - ROT_DIM), F32)], axis=1)
    return cosf, sinf


def _proj_a_kernel(x_ref, w_ref, o_ref):
    j = pl.program_id(1)
    tn = o_ref.shape[1]

    def run(act):
        for c in range(tn // PROJ_CHUNK):
            cols = slice(c * PROJ_CHUNK, (c + 1) * PROJ_CHUNK)
            acc = jnp.dot(x_ref[...], w_ref[:, cols], preferred_element_type=F32)
            o_ref[:, cols] = act(acc).astype(o_ref.dtype)

    @pl.when(j < _A_U // tn)
    def _():
        run(jax.nn.sigmoid)

    @pl.when(j >= _A_U // tn)
    def _():
        run(jax.nn.gelu)


def _proj_a_call(xb, w):
    n, d = xb.shape
    tm = min(PROJ_TM, n)
    tn = PROJ_A_TN
    assert _A_U % tn == 0 and PROJ_A_WIDTH % tn == 0
    return pl.pallas_call(
        _proj_a_kernel,
        grid=(n // tm, PROJ_A_WIDTH // tn),
        in_specs=[
            pl.BlockSpec((tm, d), lambda i, j: (i, 0)),
            pl.BlockSpec((d, tn), lambda i, j: (0, j)),
        ],
        out_specs=pl.BlockSpec((tm, tn), lambda i, j: (i, j)),
        out_shape=jax.ShapeDtypeStruct((n, PROJ_A_WIDTH), BF16),
        compiler_params=_params(("parallel", "arbitrary")),
        name="proj_a",
    )(xb, w)


def _proj_b_kernel(x_ref, w_ref, cos_ref, sin_ref, o_ref):
    j = pl.program_id(1)
    heads_per_chunk = PROJ_CHUNK // HEAD_DIM
    lane = lax.broadcasted_iota(jnp.int32, cos_ref.shape, 1)

    def run(rotate):
        for c in range(PROJ_B_TN // PROJ_CHUNK):
            cols = slice(c * PROJ_CHUNK, (c + 1) * PROJ_CHUNK)
            acc = jnp.dot(x_ref[...], w_ref[:, cols], preferred_element_type=F32)
            for h in range(heads_per_chunk):
                xh = acc[:, h * HEAD_DIM:(h + 1) * HEAD_DIM]
                if rotate:
                    partner = jnp.where(lane < ROT_HALF, pltpu.roll(xh, HEAD_DIM - ROT_HALF, 1),
                                        pltpu.roll(xh, ROT_HALF, 1))
                    xh = xh * cos_ref[...] + partner * sin_ref[...]
                o_ref[c * heads_per_chunk + h] = xh

    @pl.when(j < 2)
    def _():
        run(True)

    @pl.when(j >= 2)
    def _():
        run(False)


def _proj_b_call(xb, w, cosf, sinf):
    n, d = xb.shape
    tm = min(PROJ_TM, n)
    slabs = PROJ_B_TN // HEAD_DIM
    return pl.pallas_call(
        _proj_b_kernel,
        grid=(n // tm, 3),
        in_specs=[
            pl.BlockSpec((tm, d), lambda i, j: (i, 0)),
            pl.BlockSpec((d, PROJ_B_TN), lambda i, j: (0, j)),
            pl.BlockSpec((tm, HEAD_DIM), lambda i, j: (i, 0)),
            pl.BlockSpec((tm, HEAD_DIM), lambda i, j: (i, 0)),
        ],
        out_specs=pl.BlockSpec((slabs, tm, HEAD_DIM), lambda i, j: (j, i, 0)),
        out_shape=jax.ShapeDtypeStruct((3 * slabs, n, HEAD_DIM), F32),
        compiler_params=_params(("parallel", "arbitrary")),
        name="proj_b",
    )(xb, w, cosf, sinf)


def _sgu_kernel(u_ref, v_ref, ws_ref, bs_ref, g_ref, b_ref, o_ref):
    vn = _layer_norm(v_ref[...].astype(F32), g_ref[...], b_ref[...]).astype(BF16)
    tm = o_ref.shape[0]
    for c in range(tm // CHUNK):
        rows = slice(c * CHUNK, (c + 1) * CHUNK)
        for g in range(SGU_GROUPS):
            cols = slice(g * SGU_GROUP_CH, (g + 1) * SGU_GROUP_CH)
            z = jnp.dot(ws_ref[g], vn[rows, cols], preferred_element_type=F32) + bs_ref[:, cols]
            o_ref[rows, cols] = (u_ref[rows, cols].astype(F32) * z).astype(o_ref.dtype)


def _sgu_call(proj_a, ws, bs, ln_g, ln_b):
    n = proj_a.shape[0]
    tm = min(SGU_TM, n)
    u_blk, v_blk = _A_U // SGU_WIDTH, _A_V // SGU_WIDTH
    return pl.pallas_call(
        _sgu_kernel,
        grid=(n // tm,),
        in_specs=[
            pl.BlockSpec((tm, SGU_WIDTH), lambda i: (i, u_blk)),
            pl.BlockSpec((tm, SGU_WIDTH), lambda i: (i, v_blk)),
            _resident(ws.shape),
            _resident(bs.shape),
            _resident(ln_g.shape),
            _resident(ln_b.shape),
        ],
        out_specs=pl.BlockSpec((tm, SGU_WIDTH), lambda i: (i, 0)),
        out_shape=jax.ShapeDtypeStruct((n, SGU_WIDTH), BF16),
        compiler_params=_params(("parallel",)),
        name="sgu",
    )(proj_a, proj_a, ws, bs, ln_g, ln_b)


def _attn_kernel(q_ref, kp_ref, k_ref, vp_ref, v_ref, o_ref, lse_ref, *, dil, span, nq, hps):
    slab_idx = pl.program_id(1)
    hc = pl.program_id(2)
    qi = lax.broadcasted_iota(jnp.int32, (QBLK, QBLK), 0)
    kj = lax.broadcasted_iota(jnp.int32, (QBLK, QBLK), 1)
    dist_prev = qi + QBLK - kj
    dist_cur = qi - kj
    in_prev = (dist_prev >= 0) & (dist_prev <= span)
    mask_cur = (dist_cur >= 0) & (dist_cur <= span)
    scale = HEAD_DIM ** -0.5
    nt = (((1,), (1,)), ((), ()))
    lane = lax.broadcasted_iota(jnp.int32, (QBLK, LANES), 1)

    @pl.when(hc == 0)
    def _():
        lse_ref[...] = jnp.zeros_like(lse_ref)

    def rows(r, blk):
        return pl.ds(blk * QBLK * dil + r, QBLK, stride=dil) if dil > 1 else pl.ds(blk * QBLK, QBLK)

    def one_class(r):
        for blk in range(nq):
            if blk == 0:
                mask_prev = in_prev & (slab_idx > 0)
            else:
                mask_prev = in_prev
            sel = rows(r, blk)
            lse_rows = lse_ref[sel, :]
            for h in range(hps):
                q = q_ref.at[h][sel, :].astype(BF16)
                if blk == 0:
                    kp = kp_ref.at[h][rows(r, 0), :].astype(BF16)
                    vp = vp_ref.at[h][rows(r, 0), :].astype(BF16)
                else:
                    kp = k_ref.at[h][rows(r, blk - 1), :].astype(BF16)
                    vp = v_ref.at[h][rows(r, blk - 1), :].astype(BF16)
                kc = k_ref.at[h][sel, :].astype(BF16)
                vc = v_ref.at[h][sel, :].astype(BF16)
                sp = lax.dot_general(q, kp, nt, preferred_element_type=F32) * scale
                sc = lax.dot_general(q, kc, nt, preferred_element_type=F32) * scale
                sp = jnp.where(mask_prev, sp, -jnp.inf)
                sc = jnp.where(mask_cur, sc, -jnp.inf)
                m = jnp.maximum(jnp.max(sp, axis=-1, keepdims=True), jnp.max(sc, axis=-1, keepdims=True))
                ep = jnp.exp(sp - m)
                ec = jnp.exp(sc - m)
                den = jnp.sum(ep, axis=-1, keepdims=True) + jnp.sum(ec, axis=-1, keepdims=True)
                o = jnp.dot(ep.astype(BF16), vp, preferred_element_type=F32)
                o = o + jnp.dot(ec.astype(BF16), vc, preferred_element_type=F32)
                o_ref.at[h][sel, :] = o / den
                lse_rows = jnp.where(lane == hc * hps + h, m + jnp.log(den), lse_rows)
            lse_ref[sel, :] = lse_rows

    if dil == 1:
        one_class(0)
    else:
        def body(r, carry):
            one_class(r)
            return carry
        lax.fori_loop(0, dil, body, 0)


def _attn_call(qkv, group, batch, seq):
    window, dil = DILATED_GROUPS[group]
    span = window // dil
    nq, hps = ATTN_STEP[dil]
    slab = nq * QBLK * dil
    prev = QBLK * dil
    assert seq % slab == 0 and span <= QBLK and HEADS_PER_GROUP % hps == 0
    slabs_per_seq = seq // slab
    prev_per_seq = seq // prev
    n = batch * seq
    hblocks = HEADS_PER_GROUP // hps

    def cur(section):
        base = (section * ATT_HEADS + group * HEADS_PER_GROUP) // hps
        return pl.BlockSpec((hps, slab, LANES), lambda b, s, hc: (base + hc, b * slabs_per_seq + s, 0))

    def before(section):
        base = (section * ATT_HEADS + group * HEADS_PER_GROUP) // hps
        return pl.BlockSpec(
            (hps, prev, LANES),
            lambda b, s, hc: (base + hc, b * prev_per_seq + jnp.maximum(s * nq - 1, 0), 0))

    return pl.pallas_call(
        functools.partial(_attn_kernel, dil=dil, span=span, nq=nq, hps=hps),
        grid=(batch, slabs_per_seq, hblocks),
        in_specs=[cur(0), before(1), cur(1), before(2), cur(2)],
        out_specs=[
            pl.BlockSpec((hps, slab, LANES), lambda b, s, hc: (hc, b * slabs_per_seq + s, 0)),
            pl.BlockSpec((slab, LANES), lambda b, s, hc: (b * slabs_per_seq + s, 0)),
        ],
        out_shape=[
            jax.ShapeDtypeStruct((HEADS_PER_GROUP, n, LANES), F32),
            jax.ShapeDtypeStruct((n, LANES), F32),
        ],
        compiler_params=_params(("parallel", "arbitrary", "arbitrary")),
        name=f"attn_d{dil}",
    )(qkv, qkv, qkv, qkv, qkv)


def _mix_kernel(x_ref, act_ref, o0_ref, o1_ref, o2_ref, l0_ref, l1_ref, l2_ref, sga_ref, sgb_ref,
                wa_ref, wb_ref, wo_ref, g_ref, b_ref, wrh_ref, wrl_ref, br_ref, x1g_ref, logit_ref):
    tm = x_ref.shape[0]
    o_refs = (o0_ref, o1_ref, o2_ref)
    l_refs = (l0_ref, l1_ref, l2_ref)
    for sub in range(tm // SUB_ROWS):
        rows = pl.ds(sub * SUB_ROWS, SUB_ROWS)
        a_out = jnp.dot(act_ref[rows, :], wa_ref[...], preferred_element_type=F32)
        lses = [l[rows, :] for l in l_refs]
        heads = []
        for h in range(HEADS_PER_GROUP):
            lh = [l[:, h:h + 1] for l in lses]
            m = jnp.maximum(jnp.maximum(lh[0], lh[1]), lh[2])
            e = [jnp.exp(v - m) for v in lh]
            tot = e[0] + e[1] + e[2]
            acc = (e[0] / tot) * o_refs[0][h, rows, :]
            acc = acc + (e[1] / tot) * o_refs[1][h, rows, :]
            acc = acc + (e[2] / tot) * o_refs[2][h, rows, :]
            heads.append(acc.astype(BF16))
        merged = jnp.concatenate(heads, axis=1)
        b_out = jnp.dot(merged, wb_ref[...], preferred_element_type=F32)
        mixed_in = sga_ref[rows, :].astype(F32) * a_out + sgb_ref[rows, :].astype(F32) * b_out
        mixed = jnp.dot(mixed_in.astype(BF16), wo_ref[...], preferred_element_type=F32)
        x1 = _layer_norm(ALPHA * x_ref[rows, :] + mixed, g_ref[...], b_ref[...])
        _store_gather_rows(x1g_ref, sub * SUB_ROWS, x1)
        x_hi = x1.astype(BF16)
        x_lo = (x1 - x_hi.astype(F32)).astype(BF16)
        logits = jnp.dot(x_hi, wrh_ref[...], preferred_element_type=F32)
        logits = logits + jnp.dot(x_lo, wrh_ref[...], preferred_element_type=F32)
        logits = logits + jnp.dot(x_hi, wrl_ref[...], preferred_element_type=F32)
        logit_ref[rows, :] = logits + br_ref[...]


def _mix_call(x, act, os_, lses, proj_a, wa, wb, wo, g, b, wrh, wrl, br):
    n = x.shape[0]
    tm = min(MIX_TM, n)
    row = lambda w: pl.BlockSpec((tm, w), lambda i: (i, 0))
    o_spec = pl.BlockSpec((HEADS_PER_GROUP, tm, LANES), lambda i: (0, i, 0))
    return pl.pallas_call(
        _mix_kernel,
        grid=(n // tm,),
        in_specs=[
            row(D_MODEL), row(SGU_WIDTH),
            o_spec, o_spec, o_spec,
            row(LANES), row(LANES), row(LANES),
            pl.BlockSpec((tm, D_MODEL), lambda i: (i, 0)),
            pl.BlockSpec((tm, D_MODEL), lambda i: (i, _A_GB // D_MODEL)),
            _resident(wa.shape), _resident(wb.shape), _resident(wo.shape),
            _resident(g.shape), _resident(b.shape),
            _resident(wrh.shape), _resident(wrl.shape), _resident(br.shape),
        ],
        out_specs=[pl.BlockSpec((tm * ROW_PITCH, LANES), lambda i: (i, 0)), row(ROUTER_LANES)],
        out_shape=[jax.ShapeDtypeStruct((n * ROW_PITCH, LANES), F32), jax.ShapeDtypeStruct((n, ROUTER_LANES), F32)],
        compiler_params=_params(("parallel",)),
        name="mix",
    )(x, act, *os_, *lses, proj_a, proj_a, wa, wb, wo, g, b, wrh, wrl, br)


def _start_row(idx_ref, r, src_hbm, dst, sem, priority):
    src = src_hbm.at[pl.ds(idx_ref[0, r] * ROW_PITCH, SLABS)]
    pltpu.make_async_copy(src, dst.at[pl.ds(r * ROW_PITCH, SLABS)], sem).start(priority=priority)


def _start_row_gather(idx_ref, n_rows, src_hbm, dst, sem):
    unroll = 8
    def body(blk, carry):
        for j in range(unroll):
            _start_row(idx_ref, blk * unroll + j, src_hbm, dst, sem, j % 2)
        return carry
    lax.fori_loop(0, n_rows // unroll, body, 0)


ANCHOR_ROWS = 8


def _interleaved_row_gather(idx_ref, n_rows, src_hbm, dst, sem, n_parts):
    per = n_rows // n_parts
    def issue(part, after=None):
        if after is not None:
            dst[pl.ds(n_rows * ROW_PITCH, ANCHOR_ROWS), :] = after[0:ANCHOR_ROWS, 0:LANES]
        for r in range(part * per, (part + 1) * per):
            _start_row(idx_ref, r, src_hbm, dst, sem, r % 2)
    return issue


def _wait_row_gather(n_rows, src_hbm, dst, sem):
    pltpu.make_async_copy(src_hbm.at[pl.ds(0, n_rows * SLABS)], dst.at[pl.ds(0, n_rows * SLABS)], sem).wait()


def _gathered_rows(buf, first_row, n_rows):
    return jnp.concatenate(
        [buf[pl.ds(first_row * ROW_PITCH + k, n_rows, stride=ROW_PITCH), :] for k in range(SLABS)], axis=1)


def _store_gather_rows(ref, first_row, val):
    n_rows = val.shape[0]
    for k in range(SLABS):
        ref[pl.ds(first_row * ROW_PITCH + k, n_rows, stride=ROW_PITCH), :] = val[:, k * LANES:(k + 1) * LANES]
    ref[pl.ds(first_row * ROW_PITCH + SLABS, n_rows, stride=ROW_PITCH), :] = jnp.zeros((n_rows, LANES), val.dtype)


def _expert_kernel(blk_e_ref, n_used_ref, tok_cur_ref, tok_next_ref, bw_ref, x_hbm, w1_ref, w3_ref, w2_ref,
                   o_ref, xbuf, w1b, w3b, w2b, sem):
    i = pl.program_id(0)
    n_used = n_used_ref[0]
    slot = i % 2

    @pl.when((i == 0) & (n_used > 0))
    def _():
        _start_row_gather(tok_cur_ref, MOE_BLK, x_hbm, xbuf.at[0], sem.at[0])

    @pl.when((i == 0) | (blk_e_ref[i] != blk_e_ref[jnp.maximum(i - 1, 0)]))
    def _():
        w1b[...] = w1_ref[...].astype(BF16)
        w3b[...] = w3_ref[...].astype(BF16)
        w2b[...] = w2_ref[...].astype(BF16)

    @pl.when(i < n_used)
    def _():
        _wait_row_gather(MOE_BLK, x_hbm, xbuf.at[slot], sem.at[slot])
        n_parts = 2 * (EXPERT_FF // EXPERT_CHUNK) + D_MODEL // EXPERT_OUT_CHUNK
        issue = _interleaved_row_gather(tok_next_ref, MOE_BLK, x_hbm, xbuf.at[1 - slot], sem.at[1 - slot], n_parts)
        part = 0
        last = None
        xb = _gathered_rows(xbuf.at[slot], 0, MOE_BLK).astype(BF16)
        hid = []
        for c in range(EXPERT_FF // EXPERT_CHUNK):
            cols = slice(c * EXPERT_CHUNK, (c + 1) * EXPERT_CHUNK)
            issue(part, last)
            h1 = jnp.dot(xb, w1b[:, cols], preferred_element_type=F32)
            issue(part + 1, h1)
            h3 = jnp.dot(xb, w3b[:, cols], preferred_element_type=F32)
            part += 2
            last = h3
            hid.append((jax.nn.silu(h1) * h3).astype(BF16))
        hid = jnp.concatenate(hid, axis=1)
        slabs_per_chunk = EXPERT_OUT_CHUNK // LANES
        for c in range(D_MODEL // EXPERT_OUT_CHUNK):
            cols = slice(c * EXPERT_OUT_CHUNK, (c + 1) * EXPERT_OUT_CHUNK)
            issue(part, last)
            part += 1
            y = jnp.dot(hid, w2b[:, cols], preferred_element_type=F32) * bw_ref[...]
            last = y
            for k in range(slabs_per_chunk):
                o_ref[pl.ds(c * slabs_per_chunk + k, MOE_BLK, stride=ROW_PITCH), :] = y[:, k * LANES:(k + 1) * LANES]
        o_ref[pl.ds(SLABS, MOE_BLK, stride=ROW_PITCH), :] = jnp.zeros((MOE_BLK, LANES), F32)

    @pl.when(i == n_used - 1)
    def _():
        _wait_row_gather(MOE_BLK, x_hbm, xbuf.at[1 - slot], sem.at[1 - slot])

    @pl.when(i >= n_used)
    def _():
        o_ref[...] = jnp.zeros_like(o_ref)


def _expert_call(blk_e, n_used, buf_tok, buf_w, x1g, w1, w3, w2, layer):
    n_blocks = blk_e.shape[0]
    d = D_MODEL
    tok3 = buf_tok.reshape(n_blocks, 1, MOE_BLK)
    grid_spec = pltpu.PrefetchScalarGridSpec(
        num_scalar_prefetch=2,
        grid=(n_blocks,),
        in_specs=[
            pl.BlockSpec((None, 1, MOE_BLK), lambda i, e, u: (i, 0, 0), memory_space=pltpu.SMEM),
            pl.BlockSpec((None, 1, MOE_BLK), lambda i, e, u: (jnp.minimum(i + 1, jnp.maximum(u[0] - 1, 0)), 0, 0),
                         memory_space=pltpu.SMEM),
            pl.BlockSpec((MOE_BLK, 1), lambda i, e, u: (i, 0)),
            pl.BlockSpec(memory_space=pl.ANY),
            pl.BlockSpec((None, None, d, EXPERT_FF), lambda i, e, u: (layer, e[i], 0, 0)),
            pl.BlockSpec((None, None, d, EXPERT_FF), lambda i, e, u: (layer, e[i], 0, 0)),
            pl.BlockSpec((None, None, EXPERT_FF, d), lambda i, e, u: (layer, e[i], 0, 0)),
        ],
        out_specs=pl.BlockSpec((MOE_BLK * ROW_PITCH, LANES), lambda i, e, u: (i, 0)),
        scratch_shapes=[pltpu.VMEM((2, MOE_BLK * ROW_PITCH + ANCHOR_ROWS, LANES), F32),
                        pltpu.VMEM((d, EXPERT_FF), BF16), pltpu.VMEM((d, EXPERT_FF), BF16),
                        pltpu.VMEM((EXPERT_FF, d), BF16), pltpu.SemaphoreType.DMA((2,))],
    )
    return pl.pallas_call(
        _expert_kernel,
        grid_spec=grid_spec,
        out_shape=jax.ShapeDtypeStruct((n_blocks * MOE_BLK * ROW_PITCH, LANES), F32),
        compiler_params=_params(("arbitrary",)),
        name="experts",
    )(blk_e, n_used, tok3, tok3, buf_w.reshape(-1, 1), x1g, w1, w3, w2)


def _final_kernel(pos_cur_ref, pos_next_ref, x1g_ref, p_ref, yb_hbm, wpg_ref, wpp_ref, g_ref, b_ref,
                  x2_ref, x2b_ref, ybuf, sem):
    i = pl.program_id(0)
    n_steps = pl.num_programs(0)
    slot = i % 2
    tm = p_ref.shape[0]

    @pl.when(i == 0)
    def _():
        _start_row_gather(pos_cur_ref, TOP_K * tm, yb_hbm, ybuf.at[0], sem.at[0])

    _wait_row_gather(TOP_K * tm, yb_hbm, ybuf.at[slot], sem.at[slot])
    n_sub = tm // SUB_ROWS
    n_chunks = D_MODEL // FINAL_CHUNK
    issue = _interleaved_row_gather(pos_next_ref, TOP_K * tm, yb_hbm, ybuf.at[1 - slot], sem.at[1 - slot],
                                    n_sub * n_chunks)
    last = None
    for sub in range(n_sub):
        rows = pl.ds(sub * SUB_ROWS, SUB_ROWS)
        x1 = _gathered_rows(x1g_ref, sub * SUB_ROWS, SUB_ROWS)
        x1b = x1.astype(BF16)
        pb = p_ref[rows, :].astype(BF16)
        pre = []
        for c in range(n_chunks):
            cols = slice(c * FINAL_CHUNK, (c + 1) * FINAL_CHUNK)
            issue(sub * n_chunks + c, last)
            gate = jax.nn.sigmoid(jnp.dot(x1b, wpg_ref[:, cols], preferred_element_type=F32))
            last = gate * jnp.dot(pb, wpp_ref[:, cols], preferred_element_type=F32)
            pre.append(last)
        pre = jnp.concatenate(pre, axis=1)
        y = (_gathered_rows(ybuf.at[slot], sub * SUB_ROWS, SUB_ROWS)
             + _gathered_rows(ybuf.at[slot], tm + sub * SUB_ROWS, SUB_ROWS))
        x2 = _layer_norm(ALPHA * x1 + y + pre, g_ref[...], b_ref[...])
        x2_ref[rows, :] = x2
        x2b_ref[rows, :] = x2.astype(BF16)

    @pl.when(i == n_steps - 1)
    def _():
        _wait_row_gather(TOP_K * tm, yb_hbm, ybuf.at[1 - slot], sem.at[1 - slot])


def _final_call(pos, x1g, p, yb, wpg, wpp, g, b):
    n = p.shape[0]
    d = D_MODEL
    tm = min(FINAL_TM, n)
    steps = n // tm
    pos3 = pos.reshape(steps, tm, TOP_K).transpose(0, 2, 1).reshape(steps, 1, TOP_K * tm)
    row = lambda w: pl.BlockSpec((tm, w), lambda i: (i, 0))
    return pl.pallas_call(
        _final_kernel,
        grid=(steps,),
        in_specs=[
            pl.BlockSpec((None, 1, TOP_K * tm), lambda i: (i, 0, 0), memory_space=pltpu.SMEM),
            pl.BlockSpec((None, 1, TOP_K * tm), lambda i: (jnp.minimum(i + 1, steps - 1), 0, 0),
                         memory_space=pltpu.SMEM),
            pl.BlockSpec((tm * ROW_PITCH, LANES), lambda i: (i, 0)),
            row(PLE_DIM),
            pl.BlockSpec(memory_space=pl.ANY),
            _resident(wpg.shape), _resident(wpp.shape), _resident(g.shape), _resident(b.shape),
        ],
        out_specs=[row(d), row(d)],
        out_shape=[jax.ShapeDtypeStruct((n, d), F32), jax.ShapeDtypeStruct((n, d), BF16)],
        scratch_shapes=[pltpu.VMEM((2, TOP_K * tm * ROW_PITCH + ANCHOR_ROWS, LANES), F32),
                        pltpu.SemaphoreType.DMA((2,))],
        compiler_params=_params(("arbitrary",)),
        name="final",
    )(pos3, pos3, x1g, p, yb, wpg, wpp, g, b)


def _route(logits):
    n = logits.shape[0]
    grp_logits = logits[:, :N_EXPERT_GROUPS]
    exp_logits = logits[:, N_EXPERT_GROUPS:N_EXPERT_GROUPS + N_EXPERTS].reshape(n, N_EXPERT_GROUPS, EXPERTS_PER_GROUP)
    grp_prob = jax.nn.softmax(grp_logits, axis=-1)
    g_sel = jnp.argmax(grp_logits, axis=-1).astype(jnp.int32)
    p_g = jnp.take_along_axis(grp_prob, g_sel[:, None], axis=1)[:, 0]
    in_grp = jnp.take_along_axis(exp_logits, g_sel[:, None, None], axis=1)[:, 0]
    top_v, top_i = lax.top_k(in_grp, TOP_K)
    gate = jax.nn.softmax(top_v, axis=-1) * p_g[:, None]
    e_id = g_sel[:, None] * EXPERTS_PER_GROUP + top_i.astype(jnp.int32)

    a = n * TOP_K
    e_flat = e_id.reshape(a)
    w_flat = gate.reshape(a)
    tok = jnp.arange(a, dtype=jnp.int32) // TOP_K
    onehot = (e_flat[:, None] == jnp.arange(N_EXPERTS, dtype=jnp.int32)[None, :]).astype(jnp.int32)
    csum = jnp.cumsum(onehot, axis=0)
    rank = jnp.take_along_axis(csum, e_flat[:, None], axis=1)[:, 0] - 1
    counts = csum[-1]
    padded = ((counts + MOE_BLK - 1) // MOE_BLK) * MOE_BLK
    pend = jnp.cumsum(padded)
    pstart = pend - padded
    dest = pstart[e_flat] + rank
    n_blocks = (a + N_EXPERTS * (MOE_BLK - 1) + MOE_BLK - 1) // MOE_BLK
    p_rows = n_blocks * MOE_BLK
    buf_tok = jnp.zeros((p_rows,), jnp.int32).at[dest].set(tok)
    buf_w = jnp.zeros((p_rows,), F32).at[dest].set(w_flat)
    blk_e = jnp.minimum(
        jnp.searchsorted(pend, jnp.arange(n_blocks, dtype=jnp.int32) * MOE_BLK, side="right"),
        N_EXPERTS - 1).astype(jnp.int32)
    n_used = (pend[-1:] // MOE_BLK).astype(jnp.int32)
    return blk_e, n_used, buf_tok, buf_w, dest.reshape(n, TOP_K)


def kernel(x, p, positions, w_in, w_s, b_s, ln_v_g, ln_v_b, w_a, w_b, w_o, ln1_g, ln1_b, w_grp, b_grp, w_rt, b_rt, w1, w3, w2, w_pg, w_pp, ln2_g, ln2_b):
    batch, seq, d = x.shape
    depth = w_in.shape[0]
    n = batch * seq
    assert d == D_MODEL and w_in.shape[2] == PROJ_WIDTH
    cosf, sinf = _rotary_tables(positions)
    causal = jnp.tril(jnp.ones((CHUNK, CHUNK), F32))
    xf = x.reshape(n, d)
    xb = xf.astype(BF16)
    pad_r = ROUTER_LANES - N_EXPERT_GROUPS - N_EXPERTS
    w_pa, w_pb = _split_proj_weights(w_in)

    for i in range(depth):
        proj_a = _proj_a_call(xb, w_pa[i])
        qkv = _proj_b_call(xb, w_pb[i], cosf, sinf)
        ws = (w_s[i] * causal).astype(BF16)
        bs = jnp.repeat(b_s[i].T, SGU_GROUP_CH, axis=1)
        act = _sgu_call(proj_a, ws, bs, ln_v_g[i][None, :], ln_v_b[i][None, :])
        att = [_attn_call(qkv, g, batch, seq) for g in range(N_DIL)]
        wr = jnp.concatenate([w_grp[i], w_rt[i], jnp.zeros((d, pad_r), F32)], axis=1)
        wr_hi = wr.astype(BF16)
        wr_lo = (wr - wr_hi.astype(F32)).astype(BF16)
        br = jnp.concatenate([b_grp[i], b_rt[i], jnp.zeros((pad_r,), F32)])[None, :]
        x1g, logits = _mix_call(xf, act, [a[0] for a in att], [a[1] for a in att], proj_a,
                                w_a[i].astype(BF16), w_b[i].astype(BF16), w_o[i].astype(BF16),
                                ln1_g[i][None, :], ln1_b[i][None, :], wr_hi, wr_lo, br)
        blk_e, n_used, buf_tok, buf_w, pos = _route(logits)
        yb = _expert_call(blk_e, n_used, buf_tok, buf_w, x1g, w1, w3, w2, i)
        xf, xb = _final_call(pos, x1g, p[i].reshape(n, PLE_DIM), yb, w_pg[i].astype(BF16), w_pp[i].astype(BF16),
                             ln2_g[i][None, :], ln2_b[i][None, :])
    return xf.reshape(batch, seq, d)
```

```python
import functools

import jax
import jax.numpy as jnp
from jax import lax
from jax.experimental import pallas as pl
from jax.experimental.pallas import tpu as pltpu

F32 = jnp.float32
BF16 = jnp.bfloat16

D_MODEL = 2048
DEPTH_FOR_DEEPNORM = 4
SGU_WIDTH = 1024
SGU_GROUPS = 8
SGU_GROUP_CH = SGU_WIDTH // SGU_GROUPS
CHUNK = 128
HEAD_DIM = 128
HEADS_PER_GROUP = 4
DILATED_GROUPS = ((128, 1), (512, 4), (2048, 16))
N_DIL = len(DILATED_GROUPS)
ATT_HEADS = HEADS_PER_GROUP * N_DIL
ATT_WIDTH = ATT_HEADS * HEAD_DIM
QBLK = 128
ROT_DIM = HEAD_DIM // 4
ROT_HALF = ROT_DIM // 2
ROPE_THETA = 500000.0
PROJ_WIDTH = 2 * SGU_WIDTH + 3 * ATT_WIDTH + 2 * D_MODEL
N_EXPERT_GROUPS = 4
EXPERTS_PER_GROUP = 8
N_EXPERTS = N_EXPERT_GROUPS * EXPERTS_PER_GROUP
TOP_K = 2
EXPERT_FF = 512
MOE_BLK = 256
PLE_DIM = 256
ALPHA = (2 * DEPTH_FOR_DEEPNORM) ** 0.25
LN_EPS = 1e-5
ROUTER_LANES = 128

LANES = 128
SLABS = D_MODEL // LANES
ROW_PITCH = SLABS + 1

PROJ_TM = 1024
PROJ_A_TN = 1024
PROJ_B_TN = ATT_WIDTH
PROJ_CHUNK = 256
SGU_TM = 512
MIX_TM = 256
FINAL_TM = 256
SUB_ROWS = 128
EXPERT_CHUNK = 256
EXPERT_OUT_CHUNK = 512
FINAL_CHUNK = 512
ATTN_STEP = {1: (8, 4), 4: (2, 4), 16: (1, 2)}
VMEM_LIMIT = 56 * 1024 * 1024

_REF_SPLITS = (0, SGU_WIDTH, 2 * SGU_WIDTH, 2 * SGU_WIDTH + ATT_WIDTH, 2 * SGU_WIDTH + 2 * ATT_WIDTH,
               2 * SGU_WIDTH + 3 * ATT_WIDTH, 2 * SGU_WIDTH + 3 * ATT_WIDTH + D_MODEL, PROJ_WIDTH)
_A_GB = D_MODEL
_A_U = 2 * D_MODEL
_A_V = 2 * D_MODEL + SGU_WIDTH
PROJ_A_WIDTH = 2 * D_MODEL + 2 * SGU_WIDTH


def _params(sem, vmem=VMEM_LIMIT):
    return pltpu.CompilerParams(dimension_semantics=sem, vmem_limit_bytes=vmem)


def _resident(shape):
    nd = len(shape)
    return pl.BlockSpec(shape, lambda *_: (0,) * nd, pipeline_mode=pl.Buffered(1))


def _layer_norm(y, g, b):
    mu = jnp.mean(y, axis=-1, keepdims=True)
    yc = y - mu
    var = jnp.mean(yc * yc, axis=-1, keepdims=True)
    return yc * lax.rsqrt(var + LN_EPS) * g + b


def _split_proj_weights(w):
    sec = [w[:, :, _REF_SPLITS[k]:_REF_SPLITS[k + 1]] for k in range(7)]
    wa = jnp.concatenate([sec[5], sec[6], sec[0], sec[1]], axis=2).astype(BF16)
    wb = w[:, :, _REF_SPLITS[2]:_REF_SPLITS[5]].astype(BF16)
    return wa, wb


def _rotary_tables(positions):
    inv_freq = ROPE_THETA ** (-jnp.arange(0, ROT_DIM, 2, dtype=F32) / ROT_DIM)
    ang = positions.astype(F32).reshape(-1)[:, None] * inv_freq
    cos, sin = jnp.cos(ang), jnp.sin(ang)
    n = ang.shape[0]
    cosf = jnp.concatenate([cos, cos, jnp.ones((n, HEAD_DIM - ROT_DIM), F32)], axis=1)
    sinf = jnp.concatenate([-sin, sin, jnp.zeros((n, HEAD_DIM - ROT_DIM), F32)], axis=1)
    return cosf, sinf


def _proj_a_kernel(x_ref, w_ref, o_ref):
    j = pl.program_id(1)
    tn = o_ref.shape[1]

    def run(act):
        for c in range(tn // PROJ_CHUNK):
            cols = slice(c * PROJ_CHUNK, (c + 1) * PROJ_CHUNK)
            acc = jnp.dot(x_ref[...], w_ref[:, cols], preferred_element_type=F32)
            o_ref[:, cols] = act(acc).astype(o_ref.dtype)

    @pl.when(j < _A_U // tn)
    def _():
        run(jax.nn.sigmoid)

    @pl.when(j >= _A_U // tn)
    def _():
        run(jax.nn.gelu)


def _proj_a_call(xb, w):
    n, d = xb.shape
    tm = min(PROJ_TM, n)
    tn = PROJ_A_TN
    assert _A_U % tn == 0 and PROJ_A_WIDTH % tn == 0
    return pl.pallas_call(
        _proj_a_kernel,
        grid=(n // tm, PROJ_A_WIDTH // tn),
        in_specs=[
            pl.BlockSpec((tm, d), lambda i, j: (i, 0)),
            pl.BlockSpec((d, tn), lambda i, j: (0, j)),
        ],
        out_specs=pl.BlockSpec((tm, tn), lambda i, j: (i, j)),
        out_shape=jax.ShapeDtypeStruct((n, PROJ_A_WIDTH), BF16),
        compiler_params=_params(("parallel", "arbitrary")),
        name="proj_a",
    )(xb, w)


def _proj_b_kernel(x_ref, w_ref, cos_ref, sin_ref, o_ref):
    j = pl.program_id(1)
    heads_per_chunk = PROJ_CHUNK // HEAD_DIM
    lane = lax.broadcasted_iota(jnp.int32, cos_ref.shape, 1)

    def run(rotate):
        for c in range(PROJ_B_TN // PROJ_CHUNK):
            cols = slice(c * PROJ_CHUNK, (c + 1) * PROJ_CHUNK)
            acc = jnp.dot(x_ref[...], w_ref[:, cols], preferred_element_type=F32)
            for h in range(heads_per_chunk):
                xh = acc[:, h * HEAD_DIM:(h + 1) * HEAD_DIM]
                if rotate:
                    partner = jnp.where(lane < ROT_HALF, pltpu.roll(xh, HEAD_DIM - ROT_HALF, 1),
                                        pltpu.roll(xh, ROT_HALF, 1))
                    xh = xh * cos_ref[...] + partner * sin_ref[...]
                o_ref[c * heads_per_chunk + h] = xh

    @pl.when(j < 2)
    def _():
        run(True)

    @pl.when(j >= 2)
    def _():
        run(False)


def _proj_b_call(xb, w, cosf, sinf):
    n, d = xb.shape
    tm = min(PROJ_TM, n)
    slabs = PROJ_B_TN // HEAD_DIM
    return pl.pallas_call(
        _proj_b_kernel,
        grid=(n // tm, 3),
        in_specs=[
            pl.BlockSpec((tm, d), lambda i, j: (i, 0)),
            pl.BlockSpec((d, PROJ_B_TN), lambda i, j: (0, j)),
            pl.BlockSpec((tm, HEAD_DIM), lambda i, j: (i, 0)),
            pl.BlockSpec((tm, HEAD_DIM), lambda i, j: (i, 0)),
        ],
        out_specs=pl.BlockSpec((slabs, tm, HEAD_DIM), lambda i, j: (j, i, 0)),
        out_shape=jax.ShapeDtypeStruct((3 * slabs, n, HEAD_DIM), F32),
        compiler_params=_params(("parallel", "arbitrary")),
        name="proj_b",
    )(xb, w, cosf, sinf)


def _sgu_kernel(u_ref, v_ref, ws_ref, bs_ref, g_ref, b_ref, o_ref):
    vn = _layer_norm(v_ref[...].astype(F32), g_ref[...], b_ref[...]).astype(BF16)
    tm = o_ref.shape[0]
    for c in range(tm // CHUNK):
        rows = slice(c * CHUNK, (c + 1) * CHUNK)
        for g in range(SGU_GROUPS):
            cols = slice(g * SGU_GROUP_CH, (g + 1) * SGU_GROUP_CH)
            z = jnp.dot(ws_ref[g], vn[rows, cols], preferred_element_type=F32) + bs_ref[:, cols]
            o_ref[rows, cols] = (u_ref[rows, cols].astype(F32) * z).astype(o_ref.dtype)


def _sgu_call(proj_a, ws, bs, ln_g, ln_b):
    n = proj_a.shape[0]
    tm = min(SGU_TM, n)
    u_blk, v_blk = _A_U // SGU_WIDTH, _A_V // SGU_WIDTH
    return pl.pallas_call(
        _sgu_kernel,
        grid=(n // tm,),
        in_specs=[
            pl.BlockSpec((tm, SGU_WIDTH), lambda i: (i, u_blk)),
            pl.BlockSpec((tm, SGU_WIDTH), lambda i: (i, v_blk)),
            _resident(ws.shape),
            _resident(bs.shape),
            _resident(ln_g.shape),
            _resident(ln_b.shape),
        ],
        out_specs=pl.BlockSpec((tm, SGU_WIDTH), lambda i: (i, 0)),
        out_shape=jax.ShapeDtypeStruct((n, SGU_WIDTH), BF16),
        compiler_params=_params(("parallel",)),
        name="sgu",
    )(proj_a, proj_a, ws, bs, ln_g, ln_b)


def _attn_kernel(q_ref, kp_ref, k_ref, vp_ref, v_ref, o_ref, lse_ref, *, dil, span, nq, hps):
    slab_idx = pl.program_id(1)
    hc = pl.program_id(2)
    qi = lax.broadcasted_iota(jnp.int32, (QBLK, QBLK), 0)
    kj = lax.broadcasted_iota(jnp.int32, (QBLK, QBLK), 1)
    dist_prev = qi + QBLK - kj
    dist_cur = qi - kj
    in_prev = (dist_prev >= 0) & (dist_prev <= span)
    mask_cur = (dist_cur >= 0) & (dist_cur <= span)
    scale = HEAD_DIM ** -0.5
    nt = (((1,), (1,)), ((), ()))
    lane = lax.broadcasted_iota(jnp.int32, (QBLK, LANES), 1)

    @pl.when(hc == 0)
    def _():
        lse_ref[...] = jnp.zeros_like(lse_ref)

    def rows(r, blk):
        return pl.ds(blk * QBLK * dil + r, QBLK, stride=dil) if dil > 1 else pl.ds(blk * QBLK, QBLK)

    def one_class(r):
        for blk in range(nq):
            if blk == 0:
                mask_prev = in_prev & (slab_idx > 0)
            else:
                mask_prev = in_prev
            sel = rows(r, blk)
            lse_rows = lse_ref[sel, :]
            for h in range(hps):
                q = q_ref.at[h][sel, :].astype(BF16)
                if blk == 0:
                    kp = kp_ref.at[h][rows(r, 0), :].astype(BF16)
                    vp = vp_ref.at[h][rows(r, 0), :].astype(BF16)
                else:
                    kp = k_ref.at[h][rows(r, blk - 1), :].astype(BF16)
                    vp = v_ref.at[h][rows(r, blk - 1), :].astype(BF16)
                kc = k_ref.at[h][sel, :].astype(BF16)
                vc = v_ref.at[h][sel, :].astype(BF16)
                sp = lax.dot_general(q, kp, nt, preferred_element_type=F32) * scale
                sc = lax.dot_general(q, kc, nt, preferred_element_type=F32) * scale
                sp = jnp.where(mask_prev, sp, -jnp.inf)
                sc = jnp.where(mask_cur, sc, -jnp.inf)
                m = jnp.maximum(jnp.max(sp, axis=-1, keepdims=True), jnp.max(sc, axis=-1, keepdims=True))
                ep = jnp.exp(sp - m)
                ec = jnp.exp(sc - m)
                den = jnp.sum(ep, axis=-1, keepdims=True) + jnp.sum(ec, axis=-1, keepdims=True)
                o = jnp.dot(ep.astype(BF16), vp, preferred_element_type=F32)
                o = o + jnp.dot(ec.astype(BF16), vc, preferred_element_type=F32)
                o_ref.at[h][sel, :] = o / den
                lse_rows = jnp.where(lane == hc * hps + h, m + jnp.log(den), lse_rows)
            lse_ref[sel, :] = lse_rows

    if dil == 1:
        one_class(0)
    else:
        def body(r, carry):
            one_class(r)
            return carry
        lax.fori_loop(0, dil, body, 0)


def _attn_call(qkv, group, batch, seq):
    window, dil = DILATED_GROUPS[group]
    span = window // dil
    nq, hps = ATTN_STEP[dil]
    slab = nq * QBLK * dil
    prev = QBLK * dil
    assert seq % slab == 0 and span <= QBLK and HEADS_PER_GROUP % hps == 0
    slabs_per_seq = seq // slab
    prev_per_seq = seq // prev
    n = batch * seq
    hblocks = HEADS_PER_GROUP // hps

    def cur(section):
        base = (section * ATT_HEADS + group * HEADS_PER_GROUP) // hps
        return pl.BlockSpec((hps, slab, LANES), lambda b, s, hc: (base + hc, b * slabs_per_seq + s, 0))

    def before(section):
        base = (section * ATT_HEADS + group * HEADS_PER_GROUP) // hps
        return pl.BlockSpec(
            (hps, prev, LANES),
            lambda b, s, hc: (base + hc, b * prev_per_seq + jnp.maximum(s * nq - 1, 0), 0))

    return pl.pallas_call(
        functools.partial(_attn_kernel, dil=dil, span=span, nq=nq, hps=hps),
        grid=(batch, slabs_per_seq, hblocks),
        in_specs=[cur(0), before(1), cur(1), before(2), cur(2)],
        out_specs=[
            pl.BlockSpec((hps, slab, LANES), lambda b, s, hc: (hc, b * slabs_per_seq + s, 0)),
            pl.BlockSpec((slab, LANES), lambda b, s, hc: (b * slabs_per_seq + s, 0)),
        ],
        out_shape=[
            jax.ShapeDtypeStruct((HEADS_PER_GROUP, n, LANES), F32),
            jax.ShapeDtypeStruct((n, LANES), F32),
        ],
        compiler_params=_params(("parallel", "arbitrary", "arbitrary")),
        name=f"attn_d{dil}",
    )(qkv, qkv, qkv, qkv, qkv)


def _mix_kernel(x_ref, act_ref, o0_ref, o1_ref, o2_ref, l0_ref, l1_ref, l2_ref, sga_ref, sgb_ref,
                wa_ref, wb_ref, wo_ref, g_ref, b_ref, wrh_ref, wrl_ref, br_ref, x1g_ref, logit_ref):
    tm = x_ref.shape[0]
    o_refs = (o0_ref, o1_ref, o2_ref)
    l_refs = (l0_ref, l1_ref, l2_ref)
    for sub in range(tm // SUB_ROWS):
        rows = pl.ds(sub * SUB_ROWS, SUB_ROWS)
        a_out = jnp.dot(act_ref[rows, :], wa_ref[...], preferred_element_type=F32)
        lses = [l[rows, :] for l in l_refs]
        heads = []
        for h in range(HEADS_PER_GROUP):
            lh = [l[:, h:h + 1] for l in lses]
            m = jnp.maximum(jnp.maximum(lh[0], lh[1]), lh[2])
            e = [jnp.exp(v - m) for v in lh]
            tot = e[0] + e[1] + e[2]
            acc = (e[0] / tot) * o_refs[0][h, rows, :]
            acc = acc + (e[1] / tot) * o_refs[1][h, rows, :]
            acc = acc + (e[2] / tot) * o_refs[2][h, rows, :]
            heads.append(acc.astype(BF16))
        merged = jnp.concatenate(heads, axis=1)
        b_out = jnp.dot(merged, wb_ref[...], preferred_element_type=F32)
        mixed_in = sga_ref[rows, :].astype(F32) * a_out + sgb_ref[rows, :].astype(F32) * b_out
        mixed = jnp.dot(mixed_in.astype(BF16), wo_ref[...], preferred_element_type=F32)
        x1 = _layer_norm(ALPHA * x_ref[rows, :] + mixed, g_ref[...], b_ref[...])
        _store_gather_rows(x1g_ref, sub * SUB_ROWS, x1)
        x_hi = x1.astype(BF16)
        x_lo = (x1 - x_hi.astype(F32)).astype(BF16)
        logits = jnp.dot(x_hi, wrh_ref[...], preferred_element_type=F32)
        logits = logits + jnp.dot(x_lo, wrh_ref[...], preferred_element_type=F32)
        logits = logits + jnp.dot(x_hi, wrl_ref[...], preferred_element_type=F32)
        logit_ref[rows, :] = logits + br_ref[...]


def _mix_call(x, act, os_, lses, proj_a, wa, wb, wo, g, b, wrh, wrl, br):
    n = x.shape[0]
    tm = min(MIX_TM, n)
    row = lambda w: pl.BlockSpec((tm, w), lambda i: (i, 0))
    o_spec = pl.BlockSpec((HEADS_PER_GROUP, tm, LANES), lambda i: (0, i, 0))
    return pl.pallas_call(
        _mix_kernel,
        grid=(n // tm,),
        in_specs=[
            row(D_MODEL), row(SGU_WIDTH),
            o_spec, o_spec, o_spec,
            row(LANES), row(LANES), row(LANES),
            pl.BlockSpec((tm, D_MODEL), lambda i: (i, 0)),
            pl.BlockSpec((tm, D_MODEL), lambda i: (i, _A_GB // D_MODEL)),
            _resident(wa.shape), _resident(wb.shape), _resident(wo.shape),
            _resident(g.shape), _resident(b.shape),
            _resident(wrh.shape), _resident(wrl.shape), _resident(br.shape),
        ],
        out_specs=[pl.BlockSpec((tm * ROW_PITCH, LANES), lambda i: (i, 0)), row(ROUTER_LANES)],
        out_shape=[jax.ShapeDtypeStruct((n * ROW_PITCH, LANES), F32), jax.ShapeDtypeStruct((n, ROUTER_LANES), F32)],
        compiler_params=_params(("parallel",)),
        name="mix",
    )(x, act, *os_, *lses, proj_a, proj_a, wa, wb, wo, g, b, wrh, wrl, br)


def _start_row(idx_ref, r, src_hbm, dst, sem, priority):
    src = src_hbm.at[pl.ds(idx_ref[0, r] * ROW_PITCH, SLABS)]
    pltpu.make_async_copy(src, dst.at[pl.ds(r * ROW_PITCH, SLABS)], sem).start(priority=priority)


def _start_row_gather(idx_ref, n_rows, src_hbm, dst, sem):
    unroll = 8
    def body(blk, carry):
        for j in range(unroll):
            _start_row(idx_ref, blk * unroll + j, src_hbm, dst, sem, j % 2)
        return carry
    lax.fori_loop(0, n_rows // unroll, body, 0)


ANCHOR_ROWS = 8


def _interleaved_row_gather(idx_ref, n_rows, src_hbm, dst, sem, n_parts):
    per = n_rows // n_parts
    def issue(part, after=None):
        if after is not None:
            dst[pl.ds(n_rows * ROW_PITCH, ANCHOR_ROWS), :] = after[0:ANCHOR_ROWS, 0:LANES]
        for r in range(part * per, (part + 1) * per):
            _start_row(idx_ref, r, src_hbm, dst, sem, r % 2)
    return issue


def _wait_row_gather(n_rows, src_hbm, dst, sem):
    pltpu.make_async_copy(src_hbm.at[pl.ds(0, n_rows * SLABS)], dst.at[pl.ds(0, n_rows * SLABS)], sem).wait()


def _gathered_rows(buf, first_row, n_rows):
    return jnp.concatenate(
        [buf[pl.ds(first_row * ROW_PITCH + k, n_rows, stride=ROW_PITCH), :] for k in range(SLABS)], axis=1)


def _store_gather_rows(ref, first_row, val):
    n_rows = val.shape[0]
    for k in range(SLABS):
        ref[pl.ds(first_row * ROW_PITCH + k, n_rows, stride=ROW_PITCH), :] = val[:, k * LANES:(k + 1) * LANES]
    ref[pl.ds(first_row * ROW_PITCH + SLABS, n_rows, stride=ROW_PITCH), :] = jnp.zeros((n_rows, LANES), val.dtype)


def _expert_kernel(blk_e_ref, n_used_ref, tok_cur_ref, tok_next_ref, x_hbm, w1_ref, w3_ref, w2_ref,
                   o_ref, xbuf, w1b, w3b, w2b, sem):
    i = pl.program_id(0)
    n_used = n_used_ref[0]
    slot = i % 2

    @pl.when((i == 0) & (n_used > 0))
    def _():
        _start_row_gather(tok_cur_ref, MOE_BLK, x_hbm, xbuf.at[0], sem.at[0])

    @pl.when((i == 0) | (blk_e_ref[i] != blk_e_ref[jnp.maximum(i - 1, 0)]))
    def _():
        w1b[...] = w1_ref[...].astype(BF16)
        w3b[...] = w3_ref[...].astype(BF16)
        w2b[...] = w2_ref[...].astype(BF16)

    @pl.when(i < n_used)
    def _():
        _wait_row_gather(MOE_BLK, x_hbm, xbuf.at[slot], sem.at[slot])
        n_parts = 2 * (EXPERT_FF // EXPERT_CHUNK) + D_MODEL // EXPERT_OUT_CHUNK
        issue = _interleaved_row_gather(tok_next_ref, MOE_BLK, x_hbm, xbuf.at[1 - slot], sem.at[1 - slot], n_parts)
        part = 0
        last = None
        xb = _gathered_rows(xbuf.at[slot], 0, MOE_BLK).astype(BF16)
        hid = []
        for c in range(EXPERT_FF // EXPERT_CHUNK):
            cols = slice(c * EXPERT_CHUNK, (c + 1) * EXPERT_CHUNK)
            issue(part, last)
            h1 = jnp.dot(xb, w1b[:, cols], preferred_element_type=F32)
            issue(part + 1, h1)
            h3 = jnp.dot(xb, w3b[:, cols], preferred_element_type=F32)
            part += 2
            last = h3
            hid.append((jax.nn.silu(h1) * h3).astype(BF16))
        hid = jnp.concatenate(hid, axis=1)
        slabs_per_chunk = EXPERT_OUT_CHUNK // LANES
        for c in range(D_MODEL // EXPERT_OUT_CHUNK):
            cols = slice(c * EXPERT_OUT_CHUNK, (c + 1) * EXPERT_OUT_CHUNK)
            issue(part, last)
            part += 1
            y = jnp.dot(hid, w2b[:, cols], preferred_element_type=F32)
            last = y
            for k in range(slabs_per_chunk):
                o_ref[pl.ds(c * slabs_per_chunk + k, MOE_BLK, stride=ROW_PITCH), :] = y[:, k * LANES:(k + 1) * LANES]
        o_ref[pl.ds(SLABS, MOE_BLK, stride=ROW_PITCH), :] = jnp.zeros((MOE_BLK, LANES), F32)

    @pl.when(i == n_used - 1)
    def _():
        _wait_row_gather(MOE_BLK, x_hbm, xbuf.at[1 - slot], sem.at[1 - slot])

    @pl.when(i >= n_used)
    def _():
        o_ref[...] = jnp.zeros_like(o_ref)


def _expert_call(blk_e, n_used, buf_tok, x1g, w1, w3, w2, layer):
    n_blocks = blk_e.shape[0]
    d = D_MODEL
    tok3 = buf_tok.reshape(n_blocks, 1, MOE_BLK)
    grid_spec = pltpu.PrefetchScalarGridSpec(
        num_scalar_prefetch=2,
        grid=(n_blocks,),
        in_specs=[
            pl.BlockSpec((None, 1, MOE_BLK), lambda i, e, u: (i, 0, 0), memory_space=pltpu.SMEM),
            pl.BlockSpec((None, 1, MOE_BLK), lambda i, e, u: (jnp.minimum(i + 1, jnp.maximum(u[0] - 1, 0)), 0, 0),
                         memory_space=pltpu.SMEM),
            pl.BlockSpec(memory_space=pl.ANY),
            pl.BlockSpec((None, None, d, EXPERT_FF), lambda i, e, u: (layer, e[i], 0, 0)),
            pl.BlockSpec((None, None, d, EXPERT_FF), lambda i, e, u: (layer, e[i], 0, 0)),
            pl.BlockSpec((None, None, EXPERT_FF, d), lambda i, e, u: (layer, e[i], 0, 0)),
        ],
        out_specs=pl.BlockSpec((MOE_BLK * ROW_PITCH, LANES), lambda i, e, u: (i, 0)),
        scratch_shapes=[pltpu.VMEM((2, MOE_BLK * ROW_PITCH + ANCHOR_ROWS, LANES), F32),
                        pltpu.VMEM((d, EXPERT_FF), BF16), pltpu.VMEM((d, EXPERT_FF), BF16),
                        pltpu.VMEM((EXPERT_FF, d), BF16), pltpu.SemaphoreType.DMA((2,))],
    )
    return pl.pallas_call(
        _expert_kernel,
        grid_spec=grid_spec,
        out_shape=jax.ShapeDtypeStruct((n_blocks * MOE_BLK * ROW_PITCH, LANES), F32),
        compiler_params=_params(("arbitrary",)),
        name="experts",
    )(blk_e, n_used, tok3, tok3, x1g, w1, w3, w2)


def _final_kernel(pos_cur_ref, pos_next_ref, x1g_ref, p_ref, info_ref, yb_hbm, wpg_ref, wpp_ref, g_ref, b_ref,
                  x2_ref, x2b_ref, ybuf, sem):
    i = pl.program_id(0)
    n_steps = pl.num_programs(0)
    slot = i % 2
    tm = p_ref.shape[0]

    @pl.when(i == 0)
    def _():
        _start_row_gather(pos_cur_ref, TOP_K * tm, yb_hbm, ybuf.at[0], sem.at[0])

    _wait_row_gather(TOP_K * tm, yb_hbm, ybuf.at[slot], sem.at[slot])
    n_sub = tm // SUB_ROWS
    n_chunks = D_MODEL // FINAL_CHUNK
    issue = _interleaved_row_gather(pos_next_ref, TOP_K * tm, yb_hbm, ybuf.at[1 - slot], sem.at[1 - slot],
                                    n_sub * n_chunks)
    last = None
    for sub in range(n_sub):
        rows = pl.ds(sub * SUB_ROWS, SUB_ROWS)
        x1 = _gathered_rows(x1g_ref, sub * SUB_ROWS, SUB_ROWS)
        x1b = x1.astype(BF16)
        pb = p_ref[rows, :].astype(BF16)
        pre = []
        for c in range(n_chunks):
            cols = slice(c * FINAL_CHUNK, (c + 1) * FINAL_CHUNK)
            issue(sub * n_chunks + c, last)
            gate = jax.nn.sigmoid(jnp.dot(x1b, wpg_ref[:, cols], preferred_element_type=F32))
            last = gate * jnp.dot(pb, wpp_ref[:, cols], preferred_element_type=F32)
            pre.append(last)
        pre = jnp.concatenate(pre, axis=1)
        info = info_ref[rows, :]
        y = (_gathered_rows(ybuf.at[slot], sub * SUB_ROWS, SUB_ROWS) * info[:, INFO_GATE:INFO_GATE + 1]
             + _gathered_rows(ybuf.at[slot], tm + sub * SUB_ROWS, SUB_ROWS) * info[:, INFO_GATE + 1:INFO_GATE + 2])
        x2 = _layer_norm(ALPHA * x1 + y + pre, g_ref[...], b_ref[...])
        x2_ref[rows, :] = x2
        x2b_ref[rows, :] = x2.astype(BF16)

    @pl.when(i == n_steps - 1)
    def _():
        _wait_row_gather(TOP_K * tm, yb_hbm, ybuf.at[1 - slot], sem.at[1 - slot])


def _final_call(pos, info, x1g, p, yb, wpg, wpp, g, b):
    n = p.shape[0]
    d = D_MODEL
    tm = min(FINAL_TM, n)
    steps = n // tm
    pos3 = pos.reshape(steps, tm, TOP_K).transpose(0, 2, 1).reshape(steps, 1, TOP_K * tm)
    row = lambda w: pl.BlockSpec((tm, w), lambda i: (i, 0))
    return pl.pallas_call(
        _final_kernel,
        grid=(steps,),
        in_specs=[
            pl.BlockSpec((None, 1, TOP_K * tm), lambda i: (i, 0, 0), memory_space=pltpu.SMEM),
            pl.BlockSpec((None, 1, TOP_K * tm), lambda i: (jnp.minimum(i + 1, steps - 1), 0, 0),
                         memory_space=pltpu.SMEM),
            pl.BlockSpec((tm * ROW_PITCH, LANES), lambda i: (i, 0)),
            row(PLE_DIM),
            row(LANES),
            pl.BlockSpec(memory_space=pl.ANY),
            _resident(wpg.shape), _resident(wpp.shape), _resident(g.shape), _resident(b.shape),
        ],
        out_specs=[row(d), row(d)],
        out_shape=[jax.ShapeDtypeStruct((n, d), F32), jax.ShapeDtypeStruct((n, d), BF16)],
        scratch_shapes=[pltpu.VMEM((2, TOP_K * tm * ROW_PITCH + ANCHOR_ROWS, LANES), F32),
                        pltpu.SemaphoreType.DMA((2,))],
        compiler_params=_params(("arbitrary",)),
        name="final",
    )(pos3, pos3, x1g, p, info, yb, wpg, wpp, g, b)


INFO_DEST = 0
INFO_GATE = 2
EXPERT_LANE0 = N_EXPERT_GROUPS
ROUTE_TM = 512
META_ROWS = 256


def _lane_min_index(mask, lane_f):
    return jnp.min(jnp.where(mask, lane_f, float(LANES)), axis=1, keepdims=True)


def _route_kernel(logit_ref, tri_ref, info_ref, meta_ref, stash, run, pstart, *, n_blocks):
    phase = pl.program_id(0)
    i = pl.program_id(1)
    tm = logit_ref.shape[0]
    lane = lax.broadcasted_iota(jnp.int32, (tm, LANES), 1)
    lane_f = lane.astype(F32)
    rows = pl.ds(pl.multiple_of(i * tm, tm), tm)

    @pl.when((phase == 0) & (i == 0))
    def _():
        run[...] = jnp.zeros_like(run)

    @pl.when(phase == 0)
    def _():
        x = logit_ref[...]
        xg = jnp.where(lane < N_EXPERT_GROUPS, x, -jnp.inf)
        mg = jnp.max(xg, axis=1, keepdims=True)
        g_sel = _lane_min_index(xg == mg, lane_f)
        p_g = 1.0 / jnp.sum(jnp.exp(xg - mg), axis=1, keepdims=True)
        lo = EXPERT_LANE0 + EXPERTS_PER_GROUP * g_sel
        xe = jnp.where((lane_f >= lo) & (lane_f < lo + EXPERTS_PER_GROUP), x, -jnp.inf)
        v1 = jnp.max(xe, axis=1, keepdims=True)
        i1 = _lane_min_index(xe == v1, lane_f)
        xe2 = jnp.where(lane_f == i1, -jnp.inf, xe)
        v2 = jnp.max(xe2, axis=1, keepdims=True)
        i2 = _lane_min_index(xe2 == v2, lane_f)
        e21 = jnp.exp(v2 - v1)
        den = 1.0 + e21
        gate1 = (1.0 / den) * p_g
        gate2 = (e21 / den) * p_g
        o1 = lane_f == i1
        o2 = lane_f == i2
        both = jnp.where(o1 | o2, 1.0, 0.0)
        before = jnp.dot(tri_ref[...], both.astype(BF16), preferred_element_type=F32) + run[0:1, :]
        rank1 = jnp.sum(jnp.where(o1, before, 0.0), axis=1, keepdims=True)
        rank2 = jnp.sum(jnp.where(o2, before, 0.0), axis=1, keepdims=True)
        run[0:1, :] = run[0:1, :] + jnp.sum(both, axis=0, keepdims=True)
        info = jnp.where(lane == INFO_DEST, rank1, 0.0)
        info = jnp.where(lane == INFO_DEST + 1, rank2, info)
        info = jnp.where(lane == INFO_GATE, gate1, info)
        info = jnp.where(lane == INFO_GATE + 1, gate2, info)
        info = jnp.where(lane == INFO_GATE + 2, i1, info)
        info = jnp.where(lane == INFO_GATE + 3, i2, info)
        stash[rows, :] = info

    @pl.when((phase == 1) & (i == 0))
    def _():
        lane8 = lax.broadcasted_iota(jnp.int32, run.shape, 1)
        counts = jnp.where((lane8 >= EXPERT_LANE0) & (lane8 < EXPERT_LANE0 + N_EXPERTS), run[...], 0.0)
        padded = jnp.floor((counts + (MOE_BLK - 1)) * (1.0 / MOE_BLK)) * MOE_BLK
        pend = padded
        shift = 1
        while shift < LANES:
            pend = pend + jnp.where(lane8 >= shift, pltpu.roll(pend, shift, 1), 0.0)
            shift *= 2
        pstart[...] = pend - padded
        blk_row = lax.broadcasted_iota(jnp.int32, meta_ref.shape, 0)
        lane_m = lax.broadcasted_iota(jnp.int32, meta_ref.shape, 1)
        is_e = (lane_m >= EXPERT_LANE0) & (lane_m < EXPERT_LANE0 + N_EXPERTS)
        ended = is_e & (pend[0:1, :] <= (blk_row * MOE_BLK).astype(F32))
        blk_e = jnp.minimum(jnp.sum(jnp.where(ended, 1.0, 0.0), axis=1, keepdims=True), N_EXPERTS - 1.0)
        n_used = jnp.max(pend[0:1, :], axis=1, keepdims=True) * (1.0 / MOE_BLK)
        meta_ref[...] = jnp.where(blk_row == n_blocks, n_used, blk_e) + jnp.zeros(meta_ref.shape, F32)

    @pl.when(phase == 1)
    def _():
        info = stash[rows, :]
        i1 = info[:, INFO_GATE + 2:INFO_GATE + 3]
        i2 = info[:, INFO_GATE + 3:INFO_GATE + 4]
        start = pstart[0:1, :]
        s1 = jnp.sum(jnp.where(lane_f == i1, start, 0.0), axis=1, keepdims=True)
        s2 = jnp.sum(jnp.where(lane_f == i2, start, 0.0), axis=1, keepdims=True)
        out = jnp.where(lane == INFO_DEST, info + s1, info)
        out = jnp.where(lane == INFO_DEST + 1, info + s2, out)
        info_ref[...] = out


def _route_call(logits, n_blocks):
    n = logits.shape[0]
    tm = min(ROUTE_TM, n)
    assert n_blocks < META_ROWS
    tri = jnp.tril(jnp.ones((tm, tm), F32), k=-1).astype(BF16)
    return pl.pallas_call(
        functools.partial(_route_kernel, n_blocks=n_blocks),
        grid=(2, n // tm),
        in_specs=[
            pl.BlockSpec((tm, LANES), lambda ph, i: (jnp.where(ph == 0, i, n // tm - 1), 0)),
            _resident(tri.shape),
        ],
        out_specs=[
            pl.BlockSpec((tm, LANES), lambda ph, i: (i * ph, 0)),
            pl.BlockSpec((META_ROWS, LANES), lambda ph, i: (0, 0)),
        ],
        out_shape=[jax.ShapeDtypeStruct((n, LANES), F32), jax.ShapeDtypeStruct((META_ROWS, LANES), F32)],
        scratch_shapes=[pltpu.VMEM((n, LANES), F32), pltpu.VMEM((8, LANES), F32), pltpu.VMEM((8, LANES), F32)],
        compiler_params=_params(("arbitrary", "arbitrary")),
        name="route",
    )(logits, tri)


def _route(logits):
    n = logits.shape[0]
    a = n * TOP_K
    n_blocks = (a + N_EXPERTS * (MOE_BLK - 1) + MOE_BLK - 1) // MOE_BLK
    info, meta = _route_call(logits, n_blocks)
    dest = info[:, INFO_DEST:INFO_DEST + TOP_K].astype(jnp.int32)
    blk_e = meta[:n_blocks, 0].astype(jnp.int32)
    n_used = meta[n_blocks:n_blocks + 1, 0].astype(jnp.int32)
    tok = jnp.arange(a, dtype=jnp.int32) // TOP_K
    buf_tok = jnp.zeros((n_blocks * MOE_BLK,), jnp.int32).at[dest.reshape(a)].set(tok)
    return blk_e, n_used, buf_tok, dest, info


def kernel(x, p, positions, w_in, w_s, b_s, ln_v_g, ln_v_b, w_a, w_b, w_o, ln1_g, ln1_b, w_grp, b_grp, w_rt, b_rt, w1, w3, w2, w_pg, w_pp, ln2_g, ln2_b):
    batch, seq, d = x.shape
    depth = w_in.shape[0]
    n = batch * seq
    assert d == D_MODEL and w_in.shape[2] == PROJ_WIDTH
    cosf, sinf = _rotary_tables(positions)
    causal = jnp.tril(jnp.ones((CHUNK, CHUNK), F32))
    xf = x.reshape(n, d)
    xb = xf.astype(BF16)
    pad_r = ROUTER_LANES - N_EXPERT_GROUPS - N_EXPERTS
    w_pa, w_pb = _split_proj_weights(w_in)

    for i in range(depth):
        proj_a = _proj_a_call(xb, w_pa[i])
        qkv = _proj_b_call(xb, w_pb[i], cosf, sinf)
        ws = (w_s[i] * causal).astype(BF16)
        bs = jnp.repeat(b_s[i].T, SGU_GROUP_CH, axis=1)
        act = _sgu_call(proj_a, ws, bs, ln_v_g[i][None, :], ln_v_b[i][None, :])
        att = [_attn_call(qkv, g, batch, seq) for g in range(N_DIL)]
        wr = jnp.concatenate([w_grp[i], w_rt[i], jnp.zeros((d, pad_r), F32)], axis=1)
        wr_hi = wr.astype(BF16)
        wr_lo = (wr - wr_hi.astype(F32)).astype(BF16)
        br = jnp.concatenate([b_grp[i], b_rt[i], jnp.zeros((pad_r,), F32)])[None, :]
        x1g, logits = _mix_call(xf, act, [a[0] for a in att], [a[1] for a in att], proj_a,
                                w_a[i].astype(BF16), w_b[i].astype(BF16), w_o[i].astype(BF16),
                                ln1_g[i][None, :], ln1_b[i][None, :], wr_hi, wr_lo, br)
        blk_e, n_used, buf_tok, pos, info = _route(logits)
        yb = _expert_call(blk_e, n_used, buf_tok, x1g, w1, w3, w2, i)
        xf, xb = _final_call(pos, info, x1g, p[i].reshape(n, PLE_DIM), yb, w_pg[i].astype(BF16), w_pp[i].astype(BF16),
                             ln2_g[i][None, :], ln2_b[i][None, :])
    return xf.reshape(batch, seq, d)
```

```python
import functools

import jax
import jax.numpy as jnp
import numpy as np
from jax import lax
from jax.experimental import pallas as pl
from jax.experimental.pallas import tpu as pltpu

F32 = jnp.float32
BF16 = jnp.bfloat16

D_MODEL = 2048
DEPTH_FOR_DEEPNORM = 4
SGU_WIDTH = 1024
SGU_GROUPS = 8
SGU_GROUP_CH = SGU_WIDTH // SGU_GROUPS
CHUNK = 128
HEAD_DIM = 128
HEADS_PER_GROUP = 4
DILATED_GROUPS = ((128, 1), (512, 4), (2048, 16))
N_DIL = len(DILATED_GROUPS)
ATT_HEADS = HEADS_PER_GROUP * N_DIL
ATT_WIDTH = ATT_HEADS * HEAD_DIM
QBLK = 128
ROT_DIM = HEAD_DIM // 4
ROT_HALF = ROT_DIM // 2
ROPE_THETA = 500000.0
PROJ_WIDTH = 2 * SGU_WIDTH + 3 * ATT_WIDTH + 2 * D_MODEL
N_EXPERT_GROUPS = 4
EXPERTS_PER_GROUP = 8
N_EXPERTS = N_EXPERT_GROUPS * EXPERTS_PER_GROUP
TOP_K = 2
EXPERT_FF = 512
MOE_BLK = 256
PLE_DIM = 256
ALPHA = (2 * DEPTH_FOR_DEEPNORM) ** 0.25
LN_EPS = 1e-5
ROUTER_LANES = 128

LANES = 128
U32 = jnp.uint32
SLABS = D_MODEL // (2 * LANES)
ROW_PITCH = SLABS + 1
HI_MASK = np.uint32(0xFFFF0000)

PROJ_TM = 1024
PROJ_A_TN = 1024
PROJ_B_TN = ATT_WIDTH
PROJ_CHUNK = 256
SGU_TM = 512
MIX_TM = 256
FINAL_TM = 256
SUB_ROWS = 128
EXPERT_CHUNK = 256
EXPERT_OUT_CHUNK = 512
FINAL_CHUNK = 512
ATTN_STEP = {1: (8, 4), 4: (2, 4), 16: (1, 2)}
VMEM_LIMIT = 56 * 1024 * 1024

_REF_SPLITS = (0, SGU_WIDTH, 2 * SGU_WIDTH, 2 * SGU_WIDTH + ATT_WIDTH, 2 * SGU_WIDTH + 2 * ATT_WIDTH,
               2 * SGU_WIDTH + 3 * ATT_WIDTH, 2 * SGU_WIDTH + 3 * ATT_WIDTH + D_MODEL, PROJ_WIDTH)
_A_GB = D_MODEL
_A_U = 2 * D_MODEL
_A_V = 2 * D_MODEL + SGU_WIDTH
PROJ_A_WIDTH = 2 * D_MODEL + 2 * SGU_WIDTH


def _params(sem, vmem=VMEM_LIMIT):
    return pltpu.CompilerParams(dimension_semantics=sem, vmem_limit_bytes=vmem)


def _resident(shape):
    nd = len(shape)
    return pl.BlockSpec(shape, lambda *_: (0,) * nd, pipeline_mode=pl.Buffered(1))


def _layer_norm(y, g, b):
    mu = jnp.mean(y, axis=-1, keepdims=True)
    yc = y - mu
    var = jnp.mean(yc * yc, axis=-1, keepdims=True)
    return yc * lax.rsqrt(var + LN_EPS) * g + b


def _split_proj_weights(w):
    sec = [w[:, :, _REF_SPLITS[k]:_REF_SPLITS[k + 1]] for k in range(7)]
    wa = jnp.concatenate([sec[5], sec[6], sec[0], sec[1]], axis=2).astype(BF16)
    wb = w[:, :, _REF_SPLITS[2]:_REF_SPLITS[5]].astype(BF16)
    return wa, wb


def _rotary_tables(positions):
    inv_freq = ROPE_THETA ** (-jnp.arange(0, ROT_DIM, 2, dtype=F32) / ROT_DIM)
    ang = positions.astype(F32).reshape(-1)[:, None] * inv_freq
    cos, sin = jnp.cos(ang), jnp.sin(ang)
    n = ang.shape[0]
    cosf = jnp.concatenate([cos, cos, jnp.ones((n, HEAD_DIM - ROT_DIM), F32)], axis=1)
    sinf = jnp.concatenate([-sin, sin, jnp.zeros((n, HEAD_DIM - ROT_DIM), F32)], axis=1)
    return cosf, sinf


def _proj_a_kernel(x_ref, w_ref, o_ref):
    j = pl.program_id(1)
    tn = o_ref.shape[1]

    def run(act):
        for c in range(tn // PROJ_CHUNK):
            cols = slice(c * PROJ_CHUNK, (c + 1) * PROJ_CHUNK)
            acc = jnp.dot(x_ref[...], w_ref[:, cols], preferred_element_type=F32)
            o_ref[:, cols] = act(acc).astype(o_ref.dtype)

    @pl.when(j < _A_U // tn)
    def _():
        run(jax.nn.sigmoid)

    @pl.when(j >= _A_U // tn)
    def _():
        run(jax.nn.gelu)


def _proj_a_call(xb, w):
    n, d = xb.shape
    tm = min(PROJ_TM, n)
    tn = PROJ_A_TN
    assert _A_U % tn == 0 and PROJ_A_WIDTH % tn == 0
    return pl.pallas_call(
        _proj_a_kernel,
        grid=(n // tm, PROJ_A_WIDTH // tn),
        in_specs=[
            pl.BlockSpec((tm, d), lambda i, j: (i, 0)),
            pl.BlockSpec((d, tn), lambda i, j: (0, j)),
        ],
        out_specs=pl.BlockSpec((tm, tn), lambda i, j: (i, j)),
        out_shape=jax.ShapeDtypeStruct((n, PROJ_A_WIDTH), BF16),
        compiler_params=_params(("parallel", "arbitrary")),
        name="proj_a",
    )(xb, w)


def _proj_b_kernel(x_ref, w_ref, cos_ref, sin_ref, o_ref):
    j = pl.program_id(1)
    heads_per_chunk = PROJ_CHUNK // HEAD_DIM
    lane = lax.broadcasted_iota(jnp.int32, cos_ref.shape, 1)

    def run(rotate):
        for c in range(PROJ_B_TN // PROJ_CHUNK):
            cols = slice(c * PROJ_CHUNK, (c + 1) * PROJ_CHUNK)
            acc = jnp.dot(x_ref[...], w_ref[:, cols], preferred_element_type=F32)
            for h in range(heads_per_chunk):
                xh = acc[:, h * HEAD_DIM:(h + 1) * HEAD_DIM]
                if rotate:
                    partner = jnp.where(lane < ROT_HALF, pltpu.roll(xh, HEAD_DIM - ROT_HALF, 1),
                                        pltpu.roll(xh, ROT_HALF, 1))
                    xh = xh * cos_ref[...] + partner * sin_ref[...]
                o_ref[c * heads_per_chunk + h] = xh

    @pl.when(j < 2)
    def _():
        run(True)

    @pl.when(j >= 2)
    def _():
        run(False)


def _proj_b_call(xb, w, cosf, sinf):
    n, d = xb.shape
    tm = min(PROJ_TM, n)
    slabs = PROJ_B_TN // HEAD_DIM
    return pl.pallas_call(
        _proj_b_kernel,
        grid=(n // tm, 3),
        in_specs=[
            pl.BlockSpec((tm, d), lambda i, j: (i, 0)),
            pl.BlockSpec((d, PROJ_B_TN), lambda i, j: (0, j)),
            pl.BlockSpec((tm, HEAD_DIM), lambda i, j: (i, 0)),
            pl.BlockSpec((tm, HEAD_DIM), lambda i, j: (i, 0)),
        ],
        out_specs=pl.BlockSpec((slabs, tm, HEAD_DIM), lambda i, j: (j, i, 0)),
        out_shape=jax.ShapeDtypeStruct((3 * slabs, n, HEAD_DIM), F32),
        compiler_params=_params(("parallel", "arbitrary")),
        name="proj_b",
    )(xb, w, cosf, sinf)


def _sgu_kernel(u_ref, v_ref, ws_ref, bs_ref, g_ref, b_ref, o_ref):
    vn = _layer_norm(v_ref[...].astype(F32), g_ref[...], b_ref[...]).astype(BF16)
    tm = o_ref.shape[0]
    for c in range(tm // CHUNK):
        rows = slice(c * CHUNK, (c + 1) * CHUNK)
        for g in range(SGU_GROUPS):
            cols = slice(g * SGU_GROUP_CH, (g + 1) * SGU_GROUP_CH)
            z = jnp.dot(ws_ref[g], vn[rows, cols], preferred_element_type=F32) + bs_ref[:, cols]
            o_ref[rows, cols] = (u_ref[rows, cols].astype(F32) * z).astype(o_ref.dtype)


def _sgu_call(proj_a, ws, bs, ln_g, ln_b):
    n = proj_a.shape[0]
    tm = min(SGU_TM, n)
    u_blk, v_blk = _A_U // SGU_WIDTH, _A_V // SGU_WIDTH
    return pl.pallas_call(
        _sgu_kernel,
        grid=(n // tm,),
        in_specs=[
            pl.BlockSpec((tm, SGU_WIDTH), lambda i: (i, u_blk)),
            pl.BlockSpec((tm, SGU_WIDTH), lambda i: (i, v_blk)),
            _resident(ws.shape),
            _resident(bs.shape),
            _resident(ln_g.shape),
            _resident(ln_b.shape),
        ],
        out_specs=pl.BlockSpec((tm, SGU_WIDTH), lambda i: (i, 0)),
        out_shape=jax.ShapeDtypeStruct((n, SGU_WIDTH), BF16),
        compiler_params=_params(("parallel",)),
        name="sgu",
    )(proj_a, proj_a, ws, bs, ln_g, ln_b)


def _attn_kernel(q_ref, kp_ref, k_ref, vp_ref, v_ref, o_ref, lse_ref, *, dil, span, nq, hps):
    slab_idx = pl.program_id(1)
    hc = pl.program_id(2)
    qi = lax.broadcasted_iota(jnp.int32, (QBLK, QBLK), 0)
    kj = lax.broadcasted_iota(jnp.int32, (QBLK, QBLK), 1)
    dist_prev = qi + QBLK - kj
    dist_cur = qi - kj
    in_prev = (dist_prev >= 0) & (dist_prev <= span)
    mask_cur = (dist_cur >= 0) & (dist_cur <= span)
    scale = HEAD_DIM ** -0.5
    nt = (((1,), (1,)), ((), ()))
    lane = lax.broadcasted_iota(jnp.int32, (QBLK, LANES), 1)

    @pl.when(hc == 0)
    def _():
        lse_ref[...] = jnp.zeros_like(lse_ref)

    def rows(r, blk):
        return pl.ds(blk * QBLK * dil + r, QBLK, stride=dil) if dil > 1 else pl.ds(blk * QBLK, QBLK)

    def one_class(r):
        for blk in range(nq):
            if blk == 0:
                mask_prev = in_prev & (slab_idx > 0)
            else:
                mask_prev = in_prev
            sel = rows(r, blk)
            lse_rows = lse_ref[sel, :]
            for h in range(hps):
                q = q_ref.at[h][sel, :].astype(BF16)
                if blk == 0:
                    kp = kp_ref.at[h][rows(r, 0), :].astype(BF16)
                    vp = vp_ref.at[h][rows(r, 0), :].astype(BF16)
                else:
                    kp = k_ref.at[h][rows(r, blk - 1), :].astype(BF16)
                    vp = v_ref.at[h][rows(r, blk - 1), :].astype(BF16)
                kc = k_ref.at[h][sel, :].astype(BF16)
                vc = v_ref.at[h][sel, :].astype(BF16)
                sp = lax.dot_general(q, kp, nt, preferred_element_type=F32) * scale
                sc = lax.dot_general(q, kc, nt, preferred_element_type=F32) * scale
                sp = jnp.where(mask_prev, sp, -jnp.inf)
                sc = jnp.where(mask_cur, sc, -jnp.inf)
                m = jnp.maximum(jnp.max(sp, axis=-1, keepdims=True), jnp.max(sc, axis=-1, keepdims=True))
                ep = jnp.exp(sp - m)
                ec = jnp.exp(sc - m)
                den = jnp.sum(ep, axis=-1, keepdims=True) + jnp.sum(ec, axis=-1, keepdims=True)
                o = jnp.dot(ep.astype(BF16), vp, preferred_element_type=F32)
                o = o + jnp.dot(ec.astype(BF16), vc, preferred_element_type=F32)
                o_ref.at[h][sel, :] = o / den
                lse_rows = jnp.where(lane == hc * hps + h, m + jnp.log(den), lse_rows)
            lse_ref[sel, :] = lse_rows

    if dil == 1:
        one_class(0)
    else:
        def body(r, carry):
            one_class(r)
            return carry
        lax.fori_loop(0, dil, body, 0)


def _attn_call(qkv, group, batch, seq):
    window, dil = DILATED_GROUPS[group]
    span = window // dil
    nq, hps = ATTN_STEP[dil]
    slab = nq * QBLK * dil
    prev = QBLK * dil
    assert seq % slab == 0 and span <= QBLK and HEADS_PER_GROUP % hps == 0
    slabs_per_seq = seq // slab
    prev_per_seq = seq // prev
    n = batch * seq
    hblocks = HEADS_PER_GROUP // hps

    def cur(section):
        base = (section * ATT_HEADS + group * HEADS_PER_GROUP) // hps
        return pl.BlockSpec((hps, slab, LANES), lambda b, s, hc: (base + hc, b * slabs_per_seq + s, 0))

    def before(section):
        base = (section * ATT_HEADS + group * HEADS_PER_GROUP) // hps
        return pl.BlockSpec(
            (hps, prev, LANES),
            lambda b, s, hc: (base + hc, b * prev_per_seq + jnp.maximum(s * nq - 1, 0), 0))

    return pl.pallas_call(
        functools.partial(_attn_kernel, dil=dil, span=span, nq=nq, hps=hps),
        grid=(batch, slabs_per_seq, hblocks),
        in_specs=[cur(0), before(1), cur(1), before(2), cur(2)],
        out_specs=[
            pl.BlockSpec((hps, slab, LANES), lambda b, s, hc: (hc, b * slabs_per_seq + s, 0)),
            pl.BlockSpec((slab, LANES), lambda b, s, hc: (b * slabs_per_seq + s, 0)),
        ],
        out_shape=[
            jax.ShapeDtypeStruct((HEADS_PER_GROUP, n, LANES), F32),
            jax.ShapeDtypeStruct((n, LANES), F32),
        ],
        compiler_params=_params(("parallel", "arbitrary", "arbitrary")),
        name=f"attn_d{dil}",
    )(qkv, qkv, qkv, qkv, qkv)


def _mix_kernel(x_ref, act_ref, o0_ref, o1_ref, o2_ref, l0_ref, l1_ref, l2_ref, sga_ref, sgb_ref,
                wa_ref, wb_ref, wo_ref, g_ref, b_ref, wrh_ref, wrl_ref, br_ref, x1_ref, x1g_ref, logit_ref):
    tm = x_ref.shape[0]
    o_refs = (o0_ref, o1_ref, o2_ref)
    l_refs = (l0_ref, l1_ref, l2_ref)
    for sub in range(tm // SUB_ROWS):
        rows = pl.ds(sub * SUB_ROWS, SUB_ROWS)
        a_out = jnp.dot(act_ref[rows, :], wa_ref[...], preferred_element_type=F32)
        lses = [l[rows, :] for l in l_refs]
        heads = []
        for h in range(HEADS_PER_GROUP):
            lh = [l[:, h:h + 1] for l in lses]
            m = jnp.maximum(jnp.maximum(lh[0], lh[1]), lh[2])
            e = [jnp.exp(v - m) for v in lh]
            tot = e[0] + e[1] + e[2]
            acc = (e[0] / tot) * o_refs[0][h, rows, :]
            acc = acc + (e[1] / tot) * o_refs[1][h, rows, :]
            acc = acc + (e[2] / tot) * o_refs[2][h, rows, :]
            heads.append(acc.astype(BF16))
        merged = jnp.concatenate(heads, axis=1)
        b_out = jnp.dot(merged, wb_ref[...], preferred_element_type=F32)
        mixed_in = sga_ref[rows, :].astype(F32) * a_out + sgb_ref[rows, :].astype(F32) * b_out
        mixed = jnp.dot(mixed_in.astype(BF16), wo_ref[...], preferred_element_type=F32)
        x1 = _layer_norm(ALPHA * x_ref[rows, :] + mixed, g_ref[...], b_ref[...])
        x1_ref[rows, :] = x1
        _store_gather_rows(x1g_ref, sub * SUB_ROWS, x1)
        x_hi = x1.astype(BF16)
        x_lo = (x1 - x_hi.astype(F32)).astype(BF16)
        logits = jnp.dot(x_hi, wrh_ref[...], preferred_element_type=F32)
        logits = logits + jnp.dot(x_lo, wrh_ref[...], preferred_element_type=F32)
        logits = logits + jnp.dot(x_hi, wrl_ref[...], preferred_element_type=F32)
        logit_ref[rows, :] = logits + br_ref[...]


def _mix_call(x, act, os_, lses, proj_a, wa, wb, wo, g, b, wrh, wrl, br):
    n = x.shape[0]
    tm = min(MIX_TM, n)
    row = lambda w: pl.BlockSpec((tm, w), lambda i: (i, 0))
    o_spec = pl.BlockSpec((HEADS_PER_GROUP, tm, LANES), lambda i: (0, i, 0))
    return pl.pallas_call(
        _mix_kernel,
        grid=(n // tm,),
        in_specs=[
            row(D_MODEL), row(SGU_WIDTH),
            o_spec, o_spec, o_spec,
            row(LANES), row(LANES), row(LANES),
            pl.BlockSpec((tm, D_MODEL), lambda i: (i, 0)),
            pl.BlockSpec((tm, D_MODEL), lambda i: (i, _A_GB // D_MODEL)),
            _resident(wa.shape), _resident(wb.shape), _resident(wo.shape),
            _resident(g.shape), _resident(b.shape),
            _resident(wrh.shape), _resident(wrl.shape), _resident(br.shape),
        ],
        out_specs=[row(D_MODEL), pl.BlockSpec((tm * ROW_PITCH, LANES), lambda i: (i, 0)), row(ROUTER_LANES)],
        out_shape=[jax.ShapeDtypeStruct((n, D_MODEL), F32), jax.ShapeDtypeStruct((n * ROW_PITCH, LANES), U32),
                   jax.ShapeDtypeStruct((n, ROUTER_LANES), F32)],
        compiler_params=_params(("parallel",)),
        name="mix",
    )(x, act, *os_, *lses, proj_a, proj_a, wa, wb, wo, g, b, wrh, wrl, br)


def _start_row(idx_ref, r, src_hbm, dst, sem, priority):
    src = src_hbm.at[pl.ds(idx_ref[0, r] * ROW_PITCH, SLABS)]
    pltpu.make_async_copy(src, dst.at[pl.ds(r * ROW_PITCH, SLABS)], sem).start(priority=priority)


def _start_row_gather(idx_ref, n_rows, src_hbm, dst, sem):
    unroll = 8
    def body(blk, carry):
        for j in range(unroll):
            _start_row(idx_ref, blk * unroll + j, src_hbm, dst, sem, j % 2)
        return carry
    lax.fori_loop(0, n_rows // unroll, body, 0)


ANCHOR_ROWS = 8


def _interleaved_row_gather(idx_ref, n_rows, src_hbm, dst, sem, n_parts):
    per = n_rows // n_parts
    def issue(part, after=None):
        if after is not None:
            dst[pl.ds(n_rows * ROW_PITCH, ANCHOR_ROWS), :] = pltpu.bitcast(after[0:ANCHOR_ROWS, 0:LANES], U32)
        for r in range(part * per, (part + 1) * per):
            _start_row(idx_ref, r, src_hbm, dst, sem, r % 2)
    return issue


def _wait_row_gather(n_rows, src_hbm, dst, sem):
    pltpu.make_async_copy(src_hbm.at[pl.ds(0, n_rows * SLABS)], dst.at[pl.ds(0, n_rows * SLABS)], sem).wait()


def _gathered_rows(buf, first_row, n_rows):
    lo, hi = [], []
    for k in range(SLABS):
        w = buf[pl.ds(first_row * ROW_PITCH + k, n_rows, stride=ROW_PITCH), :]
        lo.append(pltpu.bitcast(w << 16, F32))
        hi.append(pltpu.bitcast(w & HI_MASK, F32))
    return jnp.concatenate(lo + hi, axis=1)


def _pack_words(val, k):
    half = D_MODEL // 2
    lo = val[:, k * LANES:(k + 1) * LANES].astype(BF16).astype(F32)
    hi = val[:, half + k * LANES:half + (k + 1) * LANES].astype(BF16).astype(F32)
    return (pltpu.bitcast(lo, U32) >> 16) | pltpu.bitcast(hi, U32)


def _store_gather_rows(ref, first_row, val):
    n_rows = val.shape[0]
    for k in range(SLABS):
        ref[pl.ds(first_row * ROW_PITCH + k, n_rows, stride=ROW_PITCH), :] = _pack_words(val, k)
    ref[pl.ds(first_row * ROW_PITCH + SLABS, n_rows, stride=ROW_PITCH), :] = jnp.zeros((n_rows, LANES), U32)


def _expert_kernel(blk_e_ref, n_used_ref, run_start_ref, next_e_ref, has_next_ref,
                   tok_cur_ref, tok_next_ref, x_hbm, w1_hbm, w3_hbm, w2_hbm,
                   o_ref, xbuf, w1s, w3s, w2s, w1b, w3b, w2b, sem, wsem, *, layer):
    i = pl.program_id(0)
    n_used = n_used_ref[0]
    slot = i % 2

    def weight_copies(e):
        return (pltpu.make_async_copy(w1_hbm.at[layer, e], w1s, wsem.at[0]),
                pltpu.make_async_copy(w3_hbm.at[layer, e], w3s, wsem.at[1]),
                pltpu.make_async_copy(w2_hbm.at[layer, e], w2s, wsem.at[2]))

    @pl.when((i == 0) & (n_used > 0))
    def _():
        _start_row_gather(tok_cur_ref, MOE_BLK, x_hbm, xbuf.at[0], sem.at[0])
        for c in weight_copies(blk_e_ref[0]):
            c.start()

    @pl.when((i < n_used) & (run_start_ref[i] == 1))
    def _():
        for c in weight_copies(blk_e_ref[i]):
            c.wait()
        w1b[...] = w1s[...].astype(BF16)
        w3b[...] = w3s[...].astype(BF16)
        w2b[...] = w2s[...].astype(BF16)

        @pl.when(has_next_ref[i] == 1)
        def _():
            for c in weight_copies(next_e_ref[i]):
                c.start()

    @pl.when(i < n_used)
    def _():
        _wait_row_gather(MOE_BLK, x_hbm, xbuf.at[slot], sem.at[slot])
        n_parts = 2 * (EXPERT_FF // EXPERT_CHUNK) + D_MODEL // EXPERT_OUT_CHUNK
        issue = _interleaved_row_gather(tok_next_ref, MOE_BLK, x_hbm, xbuf.at[1 - slot], sem.at[1 - slot], n_parts)
        part = 0
        last = None
        xb = _gathered_rows(xbuf.at[slot], 0, MOE_BLK).astype(BF16)
        hid = []
        for c in range(EXPERT_FF // EXPERT_CHUNK):
            cols = slice(c * EXPERT_CHUNK, (c + 1) * EXPERT_CHUNK)
            issue(part, last)
            h1 = jnp.dot(xb, w1b[:, cols], preferred_element_type=F32)
            issue(part + 1, h1)
            h3 = jnp.dot(xb, w3b[:, cols], preferred_element_type=F32)
            part += 2
            last = h3
            hid.append((jax.nn.silu(h1) * h3).astype(BF16))
        hid = jnp.concatenate(hid, axis=1)
        ys = []
        for c in range(D_MODEL // EXPERT_OUT_CHUNK):
            cols = slice(c * EXPERT_OUT_CHUNK, (c + 1) * EXPERT_OUT_CHUNK)
            issue(part, last)
            part += 1
            last = jnp.dot(hid, w2b[:, cols], preferred_element_type=F32)
            ys.append(last)
        _store_gather_rows(o_ref, 0, jnp.concatenate(ys, axis=1))

    @pl.when(i == n_used - 1)
    def _():
        _wait_row_gather(MOE_BLK, x_hbm, xbuf.at[1 - slot], sem.at[1 - slot])

    @pl.when(i >= n_used)
    def _():
        o_ref[...] = jnp.zeros_like(o_ref)


def _expert_plan(blk_e, n_used):
    n_blocks = blk_e.shape[0]
    j = jnp.arange(n_blocks, dtype=jnp.int32)
    run_start = ((j == 0) | (blk_e != jnp.roll(blk_e, 1))) & (j < n_used[0])
    start_idx = jnp.where(run_start, j, n_blocks)
    at_or_after = lax.cummin(start_idx, axis=0, reverse=True)
    nxt = jnp.concatenate([at_or_after[1:], jnp.full((1,), n_blocks, jnp.int32)])
    has_next = nxt < n_blocks
    next_e = blk_e[jnp.minimum(nxt, n_blocks - 1)]
    return run_start.astype(jnp.int32), next_e.astype(jnp.int32), has_next.astype(jnp.int32)


def _expert_call(blk_e, n_used, buf_tok, x1g, w1, w3, w2, layer):
    n_blocks = blk_e.shape[0]
    d = D_MODEL
    tok3 = buf_tok.reshape(n_blocks, 1, MOE_BLK)
    run_start, next_e, has_next = _expert_plan(blk_e, n_used)
    grid_spec = pltpu.PrefetchScalarGridSpec(
        num_scalar_prefetch=5,
        grid=(n_blocks,),
        in_specs=[
            pl.BlockSpec((None, 1, MOE_BLK), lambda i, e, u, *_: (i, 0, 0), memory_space=pltpu.SMEM),
            pl.BlockSpec((None, 1, MOE_BLK),
                         lambda i, e, u, *_: (jnp.minimum(i + 1, jnp.maximum(u[0] - 1, 0)), 0, 0),
                         memory_space=pltpu.SMEM),
            pl.BlockSpec(memory_space=pl.ANY),
            pl.BlockSpec(memory_space=pl.ANY),
            pl.BlockSpec(memory_space=pl.ANY),
            pl.BlockSpec(memory_space=pl.ANY),
        ],
        out_specs=pl.BlockSpec((MOE_BLK * ROW_PITCH, LANES), lambda i, e, u, *_: (i, 0)),
        scratch_shapes=[pltpu.VMEM((2, MOE_BLK * ROW_PITCH + ANCHOR_ROWS, LANES), U32),
                        pltpu.VMEM((d, EXPERT_FF), F32), pltpu.VMEM((d, EXPERT_FF), F32),
                        pltpu.VMEM((EXPERT_FF, d), F32),
                        pltpu.VMEM((d, EXPERT_FF), BF16), pltpu.VMEM((d, EXPERT_FF), BF16),
                        pltpu.VMEM((EXPERT_FF, d), BF16),
                        pltpu.SemaphoreType.DMA((2,)), pltpu.SemaphoreType.DMA((3,))],
    )
    return pl.pallas_call(
        functools.partial(_expert_kernel, layer=layer),
        grid_spec=grid_spec,
        out_shape=jax.ShapeDtypeStruct((n_blocks * MOE_BLK * ROW_PITCH, LANES), U32),
        compiler_params=_params(("arbitrary",)),
        name="experts",
    )(blk_e, n_used, run_start, next_e, has_next, tok3, tok3, x1g, w1, w3, w2)


def _final_kernel(pos_cur_ref, pos_next_ref, x1_ref, p_ref, info_ref, yb_hbm, wpg_ref, wpp_ref, g_ref, b_ref,
                  x2_ref, x2b_ref, ybuf, sem):
    i = pl.program_id(0)
    n_steps = pl.num_programs(0)
    slot = i % 2
    tm = p_ref.shape[0]

    @pl.when(i == 0)
    def _():
        _start_row_gather(pos_cur_ref, TOP_K * tm, yb_hbm, ybuf.at[0], sem.at[0])

    _wait_row_gather(TOP_K * tm, yb_hbm, ybuf.at[slot], sem.at[slot])
    n_sub = tm // SUB_ROWS
    n_chunks = D_MODEL // FINAL_CHUNK
    issue = _interleaved_row_gather(pos_next_ref, TOP_K * tm, yb_hbm, ybuf.at[1 - slot], sem.at[1 - slot],
                                    n_sub * n_chunks)
    last = None
    for sub in range(n_sub):
        rows = pl.ds(sub * SUB_ROWS, SUB_ROWS)
        x1 = x1_ref[rows, :]
        x1b = x1.astype(BF16)
        pb = p_ref[rows, :].astype(BF16)
        pre = []
        for c in range(n_chunks):
            cols = slice(c * FINAL_CHUNK, (c + 1) * FINAL_CHUNK)
            issue(sub * n_chunks + c, last)
            gate = jax.nn.sigmoid(jnp.dot(x1b, wpg_ref[:, cols], preferred_element_type=F32))
            last = gate * jnp.dot(pb, wpp_ref[:, cols], preferred_element_type=F32)
            pre.append(last)
        pre = jnp.concatenate(pre, axis=1)
        info = info_ref[rows, :]
        y = (_gathered_rows(ybuf.at[slot], sub * SUB_ROWS, SUB_ROWS) * info[:, INFO_GATE:INFO_GATE + 1]
             + _gathered_rows(ybuf.at[slot], tm + sub * SUB_ROWS, SUB_ROWS) * info[:, INFO_GATE + 1:INFO_GATE + 2])
        x2 = _layer_norm(ALPHA * x1 + y + pre, g_ref[...], b_ref[...])
        x2_ref[rows, :] = x2
        x2b_ref[rows, :] = x2.astype(BF16)

    @pl.when(i == n_steps - 1)
    def _():
        _wait_row_gather(TOP_K * tm, yb_hbm, ybuf.at[1 - slot], sem.at[1 - slot])


def _final_call(pos, info, x1, p, yb, wpg, wpp, g, b):
    n = p.shape[0]
    d = D_MODEL
    tm = min(FINAL_TM, n)
    steps = n // tm
    pos3 = pos.reshape(steps, tm, TOP_K).transpose(0, 2, 1).reshape(steps, 1, TOP_K * tm)
    row = lambda w: pl.BlockSpec((tm, w), lambda i: (i, 0))
    return pl.pallas_call(
        _final_kernel,
        grid=(steps,),
        in_specs=[
            pl.BlockSpec((None, 1, TOP_K * tm), lambda i: (i, 0, 0), memory_space=pltpu.SMEM),
            pl.BlockSpec((None, 1, TOP_K * tm), lambda i: (jnp.minimum(i + 1, steps - 1), 0, 0),
                         memory_space=pltpu.SMEM),
            row(d),
            row(PLE_DIM),
            row(LANES),
            pl.BlockSpec(memory_space=pl.ANY),
            _resident(wpg.shape), _resident(wpp.shape), _resident(g.shape), _resident(b.shape),
        ],
        out_specs=[row(d), row(d)],
        out_shape=[jax.ShapeDtypeStruct((n, d), F32), jax.ShapeDtypeStruct((n, d), BF16)],
        scratch_shapes=[pltpu.VMEM((2, TOP_K * tm * ROW_PITCH + ANCHOR_ROWS, LANES), U32),
                        pltpu.SemaphoreType.DMA((2,))],
        compiler_params=_params(("arbitrary",)),
        name="final",
    )(pos3, pos3, x1, p, info, yb, wpg, wpp, g, b)


INFO_DEST = 0
INFO_GATE = 2
EXPERT_LANE0 = N_EXPERT_GROUPS
ROUTE_TM = 512
META_ROWS = 256


def _lane_min_index(mask, lane_f):
    return jnp.min(jnp.where(mask, lane_f, float(LANES)), axis=1, keepdims=True)


def _route_kernel(logit_ref, tri_ref, info_ref, meta_ref, stash, run, pstart, *, n_blocks):
    phase = pl.program_id(0)
    i = pl.program_id(1)
    tm = logit_ref.shape[0]
    lane = lax.broadcasted_iota(jnp.int32, (tm, LANES), 1)
    lane_f = lane.astype(F32)
    rows = pl.ds(pl.multiple_of(i * tm, tm), tm)

    @pl.when((phase == 0) & (i == 0))
    def _():
        run[...] = jnp.zeros_like(run)

    @pl.when(phase == 0)
    def _():
        x = logit_ref[...]
        xg = jnp.where(lane < N_EXPERT_GROUPS, x, -jnp.inf)
        mg = jnp.max(xg, axis=1, keepdims=True)
        g_sel = _lane_min_index(xg == mg, lane_f)
        p_g = 1.0 / jnp.sum(jnp.exp(xg - mg), axis=1, keepdims=True)
        lo = EXPERT_LANE0 + EXPERTS_PER_GROUP * g_sel
        xe = jnp.where((lane_f >= lo) & (lane_f < lo + EXPERTS_PER_GROUP), x, -jnp.inf)
        v1 = jnp.max(xe, axis=1, keepdims=True)
        i1 = _lane_min_index(xe == v1, lane_f)
        xe2 = jnp.where(lane_f == i1, -jnp.inf, xe)
        v2 = jnp.max(xe2, axis=1, keepdims=True)
        i2 = _lane_min_index(xe2 == v2, lane_f)
        e21 = jnp.exp(v2 - v1)
        den = 1.0 + e21
        gate1 = (1.0 / den) * p_g
        gate2 = (e21 / den) * p_g
        o1 = lane_f == i1
        o2 = lane_f == i2
        both = jnp.where(o1 | o2, 1.0, 0.0)
        before = jnp.dot(tri_ref[...], both.astype(BF16), preferred_element_type=F32) + run[0:1, :]
        rank1 = jnp.sum(jnp.where(o1, before, 0.0), axis=1, keepdims=True)
        rank2 = jnp.sum(jnp.where(o2, before, 0.0), axis=1, keepdims=True)
        run[0:1, :] = run[0:1, :] + jnp.sum(both, axis=0, keepdims=True)
        info = jnp.where(lane == INFO_DEST, rank1, 0.0)
        info = jnp.where(lane == INFO_DEST + 1, rank2, info)
        info = jnp.where(lane == INFO_GATE, gate1, info)
        info = jnp.where(lane == INFO_GATE + 1, gate2, info)
        info = jnp.where(lane == INFO_GATE + 2, i1, info)
        info = jnp.where(lane == INFO_GATE + 3, i2, info)
        stash[rows, :] = info

    @pl.when((phase == 1) & (i == 0))
    def _():
        lane8 = lax.broadcasted_iota(jnp.int32, run.shape, 1)
        counts = jnp.where((lane8 >= EXPERT_LANE0) & (lane8 < EXPERT_LANE0 + N_EXPERTS), run[...], 0.0)
        padded = jnp.floor((counts + (MOE_BLK - 1)) * (1.0 / MOE_BLK)) * MOE_BLK
        pend = padded
        shift = 1
        while shift < LANES:
            pend = pend + jnp.where(lane8 >= shift, pltpu.roll(pend, shift, 1), 0.0)
            shift *= 2
        pstart[...] = pend - padded
        blk_row = lax.broadcasted_iota(jnp.int32, meta_ref.shape, 0)
        lane_m = lax.broadcasted_iota(jnp.int32, meta_ref.shape, 1)
        is_e = (lane_m >= EXPERT_LANE0) & (lane_m < EXPERT_LANE0 + N_EXPERTS)
        ended = is_e & (pend[0:1, :] <= (blk_row * MOE_BLK).astype(F32))
        blk_e = jnp.minimum(jnp.sum(jnp.where(ended, 1.0, 0.0), axis=1, keepdims=True), N_EXPERTS - 1.0)
        n_used = jnp.max(pend[0:1, :], axis=1, keepdims=True) * (1.0 / MOE_BLK)
        meta_ref[...] = jnp.where(blk_row == n_blocks, n_used, blk_e) + jnp.zeros(meta_ref.shape, F32)

    @pl.when(phase == 1)
    def _():
        info = stash[rows, :]
        i1 = info[:, INFO_GATE + 2:INFO_GATE + 3]
        i2 = info[:, INFO_GATE + 3:INFO_GATE + 4]
        start = pstart[0:1, :]
        s1 = jnp.sum(jnp.where(lane_f == i1, start, 0.0), axis=1, keepdims=True)
        s2 = jnp.sum(jnp.where(lane_f == i2, start, 0.0), axis=1, keepdims=True)
        out = jnp.where(lane == INFO_DEST, info + s1, info)
        out = jnp.where(lane == INFO_DEST + 1, info + s2, out)
        info_ref[...] = out


def _route_call(logits, n_blocks):
    n = logits.shape[0]
    tm = min(ROUTE_TM, n)
    assert n_blocks < META_ROWS
    tri = jnp.tril(jnp.ones((tm, tm), F32), k=-1).astype(BF16)
    return pl.pallas_call(
        functools.partial(_route_kernel, n_blocks=n_blocks),
        grid=(2, n // tm),
        in_specs=[
            pl.BlockSpec((tm, LANES), lambda ph, i: (jnp.where(ph == 0, i, n // tm - 1), 0)),
            _resident(tri.shape),
        ],
        out_specs=[
            pl.BlockSpec((tm, LANES), lambda ph, i: (i * ph, 0)),
            pl.BlockSpec((META_ROWS, LANES), lambda ph, i: (0, 0)),
        ],
        out_shape=[jax.ShapeDtypeStruct((n, LANES), F32), jax.ShapeDtypeStruct((META_ROWS, LANES), F32)],
        scratch_shapes=[pltpu.VMEM((n, LANES), F32), pltpu.VMEM((8, LANES), F32), pltpu.VMEM((8, LANES), F32)],
        compiler_params=_params(("arbitrary", "arbitrary")),
        name="route",
    )(logits, tri)


def _route(logits):
    n = logits.shape[0]
    a = n * TOP_K
    n_blocks = (a + N_EXPERTS * (MOE_BLK - 1) + MOE_BLK - 1) // MOE_BLK
    info, meta = _route_call(logits, n_blocks)
    dest = info[:, INFO_DEST:INFO_DEST + TOP_K].astype(jnp.int32)
    blk_e = meta[:n_blocks, 0].astype(jnp.int32)
    n_used = meta[n_blocks:n_blocks + 1, 0].astype(jnp.int32)
    tok = jnp.arange(a, dtype=jnp.int32) // TOP_K
    buf_tok = jnp.zeros((n_blocks * MOE_BLK,), jnp.int32).at[dest.reshape(a)].set(tok)
    return blk_e, n_used, buf_tok, dest, info


def kernel(x, p, positions, w_in, w_s, b_s, ln_v_g, ln_v_b, w_a, w_b, w_o, ln1_g, ln1_b, w_grp, b_grp, w_rt, b_rt, w1, w3, w2, w_pg, w_pp, ln2_g, ln2_b):
    batch, seq, d = x.shape
    depth = w_in.shape[0]
    n = batch * seq
    assert d == D_MODEL and w_in.shape[2] == PROJ_WIDTH
    cosf, sinf = _rotary_tables(positions)
    causal = jnp.tril(jnp.ones((CHUNK, CHUNK), F32))
    xf = x.reshape(n, d)
    xb = xf.astype(BF16)
    pad_r = ROUTER_LANES - N_EXPERT_GROUPS - N_EXPERTS
    w_pa, w_pb = _split_proj_weights(w_in)

    for i in range(depth):
        proj_a = _proj_a_call(xb, w_pa[i])
        qkv = _proj_b_call(xb, w_pb[i], cosf, sinf)
        ws = (w_s[i] * causal).astype(BF16)
        bs = jnp.repeat(b_s[i].T, SGU_GROUP_CH, axis=1)
        act = _sgu_call(proj_a, ws, bs, ln_v_g[i][None, :], ln_v_b[i][None, :])
        att = [_attn_call(qkv, g, batch, seq) for g in range(N_DIL)]
        wr = jnp.concatenate([w_grp[i], w_rt[i], jnp.zeros((d, pad_r), F32)], axis=1)
        wr_hi = wr.astype(BF16)
        wr_lo = (wr - wr_hi.astype(F32)).astype(BF16)
        br = jnp.concatenate([b_grp[i], b_rt[i], jnp.zeros((pad_r,), F32)])[None, :]
        x1, x1g, logits = _mix_call(xf, act, [a[0] for a in att], [a[1] for a in att], proj_a,
                                w_a[i].astype(BF16), w_b[i].astype(BF16), w_o[i].astype(BF16),
                                ln1_g[i][None, :], ln1_b[i][None, :], wr_hi, wr_lo, br)
        blk_e, n_used, buf_tok, pos, info = _route(logits)
        yb = _expert_call(blk_e, n_used, buf_tok, x1g, w1, w3, w2, i)
        xf, xb = _final_call(pos, info, x1, p[i].reshape(n, PLE_DIM), yb, w_pg[i].astype(BF16), w_pp[i].astype(BF16),
                             ln2_g[i][None, :], ln2_b[i][None, :])
    return xf.reshape(batch, seq, d)
```

```python
import functools

import jax
import jax.numpy as jnp
import numpy as np
from jax import lax
from jax.experimental import pallas as pl
from jax.experimental.pallas import tpu as pltpu

F32 = jnp.float32
BF16 = jnp.bfloat16

D_MODEL = 2048
DEPTH_FOR_DEEPNORM = 4
SGU_WIDTH = 1024
SGU_GROUPS = 8
SGU_GROUP_CH = SGU_WIDTH // SGU_GROUPS
CHUNK = 128
HEAD_DIM = 128
HEADS_PER_GROUP = 4
DILATED_GROUPS = ((128, 1), (512, 4), (2048, 16))
N_DIL = len(DILATED_GROUPS)
ATT_HEADS = HEADS_PER_GROUP * N_DIL
ATT_WIDTH = ATT_HEADS * HEAD_DIM
QBLK = 128
ROT_DIM = HEAD_DIM // 4
ROT_HALF = ROT_DIM // 2
ROPE_THETA = 500000.0
PROJ_WIDTH = 2 * SGU_WIDTH + 3 * ATT_WIDTH + 2 * D_MODEL
N_EXPERT_GROUPS = 4
EXPERTS_PER_GROUP = 8
N_EXPERTS = N_EXPERT_GROUPS * EXPERTS_PER_GROUP
TOP_K = 2
EXPERT_FF = 512
MOE_BLK = 256
PLE_DIM = 256
ALPHA = (2 * DEPTH_FOR_DEEPNORM) ** 0.25
LN_EPS = 1e-5
ROUTER_LANES = 128

LANES = 128
U32 = jnp.uint32
SLABS = D_MODEL // (2 * LANES)
ROW_PITCH = SLABS + 1
HI_MASK = np.uint32(0xFFFF0000)

PROJ_TM = 1024
PROJ_A_TN = 1024
PROJ_B_TN = ATT_WIDTH
PROJ_CHUNK = 256
SGU_TM = 512
MIX_TM = 256
FINAL_TM = 256
SUB_ROWS = 128
EXPERT_CHUNK = 256
EXPERT_OUT_CHUNK = 512
FINAL_CHUNK = 512
ATTN_STEP = {1: (8, 4), 4: (2, 4), 16: (1, 2)}
VMEM_LIMIT = 56 * 1024 * 1024

_REF_SPLITS = (0, SGU_WIDTH, 2 * SGU_WIDTH, 2 * SGU_WIDTH + ATT_WIDTH, 2 * SGU_WIDTH + 2 * ATT_WIDTH,
               2 * SGU_WIDTH + 3 * ATT_WIDTH, 2 * SGU_WIDTH + 3 * ATT_WIDTH + D_MODEL, PROJ_WIDTH)
_A_GB = D_MODEL
_A_U = 2 * D_MODEL
_A_V = 2 * D_MODEL + SGU_WIDTH
PROJ_A_WIDTH = 2 * D_MODEL + 2 * SGU_WIDTH


def _params(sem, vmem=VMEM_LIMIT):
    return pltpu.CompilerParams(dimension_semantics=sem, vmem_limit_bytes=vmem)


def _resident(shape):
    nd = len(shape)
    return pl.BlockSpec(shape, lambda *_: (0,) * nd, pipeline_mode=pl.Buffered(1))


def _layer_norm(y, g, b):
    mu = jnp.mean(y, axis=-1, keepdims=True)
    yc = y - mu
    var = jnp.mean(yc * yc, axis=-1, keepdims=True)
    return yc * lax.rsqrt(var + LN_EPS) * g + b


def _split_proj_weights(w):
    sec = [w[:, :, _REF_SPLITS[k]:_REF_SPLITS[k + 1]] for k in range(7)]
    wa = jnp.concatenate([sec[5], sec[6], sec[0], sec[1]], axis=2).astype(BF16)
    wb = w[:, :, _REF_SPLITS[2]:_REF_SPLITS[5]].astype(BF16)
    return wa, wb


def _rotary_tables(positions):
    inv_freq = ROPE_THETA ** (-jnp.arange(0, ROT_DIM, 2, dtype=F32) / ROT_DIM)
    ang = positions.astype(F32).reshape(-1)[:, None] * inv_freq
    cos, sin = jnp.cos(ang), jnp.sin(ang)
    n = ang.shape[0]
    cosf = jnp.concatenate([cos, cos, jnp.ones((n, HEAD_DIM - ROT_DIM), F32)], axis=1)
    sinf = jnp.concatenate([-sin, sin, jnp.zeros((n, HEAD_DIM - ROT_DIM), F32)], axis=1)
    return cosf, sinf


def _proj_a_kernel(x_ref, w_ref, o_ref):
    j = pl.program_id(1)
    tn = o_ref.shape[1]

    def run(act):
        for c in range(tn // PROJ_CHUNK):
            cols = slice(c * PROJ_CHUNK, (c + 1) * PROJ_CHUNK)
            acc = jnp.dot(x_ref[...], w_ref[:, cols], preferred_element_type=F32)
            o_ref[:, cols] = act(acc).astype(o_ref.dtype)

    @pl.when(j < _A_U // tn)
    def _():
        run(jax.nn.sigmoid)

    @pl.when(j >= _A_U // tn)
    def _():
        run(jax.nn.gelu)


def _proj_a_call(xb, w):
    n, d = xb.shape
    tm = min(PROJ_TM, n)
    tn = PROJ_A_TN
    assert _A_U % tn == 0 and PROJ_A_WIDTH % tn == 0
    return pl.pallas_call(
        _proj_a_kernel,
        grid=(n // tm, PROJ_A_WIDTH // tn),
        in_specs=[
            pl.BlockSpec((tm, d), lambda i, j: (i, 0)),
            pl.BlockSpec((d, tn), lambda i, j: (0, j)),
        ],
        out_specs=pl.BlockSpec((tm, tn), lambda i, j: (i, j)),
        out_shape=jax.ShapeDtypeStruct((n, PROJ_A_WIDTH), BF16),
        compiler_params=_params(("parallel", "arbitrary")),
        name="proj_a",
    )(xb, w)


def _proj_b_kernel(x_ref, w_ref, cos_ref, sin_ref, o_ref):
    j = pl.program_id(1)
    heads_per_chunk = PROJ_CHUNK // HEAD_DIM
    lane = lax.broadcasted_iota(jnp.int32, cos_ref.shape, 1)

    def run(rotate):
        for c in range(PROJ_B_TN // PROJ_CHUNK):
            cols = slice(c * PROJ_CHUNK, (c + 1) * PROJ_CHUNK)
            acc = jnp.dot(x_ref[...], w_ref[:, cols], preferred_element_type=F32)
            for h in range(heads_per_chunk):
                xh = acc[:, h * HEAD_DIM:(h + 1) * HEAD_DIM]
                if rotate:
                    partner = jnp.where(lane < ROT_HALF, pltpu.roll(xh, HEAD_DIM - ROT_HALF, 1),
                                        pltpu.roll(xh, ROT_HALF, 1))
                    xh = xh * cos_ref[...] + partner * sin_ref[...]
                o_ref[c * heads_per_chunk + h] = xh

    @pl.when(j < 2)
    def _():
        run(True)

    @pl.when(j >= 2)
    def _():
        run(False)


def _proj_b_call(xb, w, cosf, sinf):
    n, d = xb.shape
    tm = min(PROJ_TM, n)
    slabs = PROJ_B_TN // HEAD_DIM
    return pl.pallas_call(
        _proj_b_kernel,
        grid=(n // tm, 3),
        in_specs=[
            pl.BlockSpec((tm, d), lambda i, j: (i, 0)),
            pl.BlockSpec((d, PROJ_B_TN), lambda i, j: (0, j)),
            pl.BlockSpec((tm, HEAD_DIM), lambda i, j: (i, 0)),
            pl.BlockSpec((tm, HEAD_DIM), lambda i, j: (i, 0)),
        ],
        out_specs=pl.BlockSpec((slabs, tm, HEAD_DIM), lambda i, j: (j, i, 0)),
        out_shape=jax.ShapeDtypeStruct((3 * slabs, n, HEAD_DIM), F32),
        compiler_params=_params(("parallel", "arbitrary")),
        name="proj_b",
    )(xb, w, cosf, sinf)


def _sgu_kernel(u_ref, v_ref, ws_ref, bs_ref, g_ref, b_ref, o_ref):
    vn = _layer_norm(v_ref[...].astype(F32), g_ref[...], b_ref[...]).astype(BF16)
    tm = o_ref.shape[0]
    for c in range(tm // CHUNK):
        rows = slice(c * CHUNK, (c + 1) * CHUNK)
        for g in range(SGU_GROUPS):
            cols = slice(g * SGU_GROUP_CH, (g + 1) * SGU_GROUP_CH)
            z = jnp.dot(ws_ref[g], vn[rows, cols], preferred_element_type=F32) + bs_ref[:, cols]
            o_ref[rows, cols] = (u_ref[rows, cols].astype(F32) * z).astype(o_ref.dtype)


def _sgu_call(proj_a, ws, bs, ln_g, ln_b):
    n = proj_a.shape[0]
    tm = min(SGU_TM, n)
    u_blk, v_blk = _A_U // SGU_WIDTH, _A_V // SGU_WIDTH
    return pl.pallas_call(
        _sgu_kernel,
        grid=(n // tm,),
        in_specs=[
            pl.BlockSpec((tm, SGU_WIDTH), lambda i: (i, u_blk)),
            pl.BlockSpec((tm, SGU_WIDTH), lambda i: (i, v_blk)),
            _resident(ws.shape),
            _resident(bs.shape),
            _resident(ln_g.shape),
            _resident(ln_b.shape),
        ],
        out_specs=pl.BlockSpec((tm, SGU_WIDTH), lambda i: (i, 0)),
        out_shape=jax.ShapeDtypeStruct((n, SGU_WIDTH), BF16),
        compiler_params=_params(("parallel",)),
        name="sgu",
    )(proj_a, proj_a, ws, bs, ln_g, ln_b)


def _attn_kernel(q_ref, kp_ref, k_ref, vp_ref, v_ref, o_ref, lse_ref, *, dil, span, nq, hps):
    slab_idx = pl.program_id(1)
    hc = pl.program_id(2)
    qi = lax.broadcasted_iota(jnp.int32, (QBLK, QBLK), 0)
    kj = lax.broadcasted_iota(jnp.int32, (QBLK, QBLK), 1)
    dist_prev = qi + QBLK - kj
    dist_cur = qi - kj
    in_prev = (dist_prev >= 0) & (dist_prev <= span)
    mask_cur = (dist_cur >= 0) & (dist_cur <= span)
    scale = HEAD_DIM ** -0.5
    nt = (((1,), (1,)), ((), ()))
    lane = lax.broadcasted_iota(jnp.int32, (QBLK, LANES), 1)
    ones = jnp.ones((QBLK, HEAD_DIM), BF16)

    @pl.when(hc == 0)
    def _():
        lse_ref[...] = jnp.zeros_like(lse_ref)

    def rows(r, blk):
        return pl.ds(blk * QBLK * dil + r, QBLK, stride=dil) if dil > 1 else pl.ds(blk * QBLK, QBLK)

    def one_class(r):
        for blk in range(nq):
            if blk == 0:
                mask_prev = in_prev & (slab_idx > 0)
            else:
                mask_prev = in_prev
            sel = rows(r, blk)
            lse_rows = lse_ref[sel, :]
            for h in range(hps):
                q = (q_ref.at[h][sel, :] * scale).astype(BF16)
                if blk == 0:
                    kp = kp_ref.at[h][rows(r, 0), :].astype(BF16)
                    vp = vp_ref.at[h][rows(r, 0), :].astype(BF16)
                else:
                    kp = k_ref.at[h][rows(r, blk - 1), :].astype(BF16)
                    vp = v_ref.at[h][rows(r, blk - 1), :].astype(BF16)
                kc = k_ref.at[h][sel, :].astype(BF16)
                vc = v_ref.at[h][sel, :].astype(BF16)
                sp = jnp.where(mask_prev, lax.dot_general(q, kp, nt, preferred_element_type=F32), -jnp.inf)
                sc = jnp.where(mask_cur, lax.dot_general(q, kc, nt, preferred_element_type=F32), -jnp.inf)
                m = jnp.max(jnp.maximum(sp, sc), axis=-1, keepdims=True)
                ep = jnp.exp((sp - m).astype(BF16))
                ec = jnp.exp((sc - m).astype(BF16))
                od = jnp.dot(ep, jnp.concatenate([vp, ones], axis=1), preferred_element_type=F32)
                od = od + jnp.dot(ec, jnp.concatenate([vc, ones], axis=1), preferred_element_type=F32)
                den = od[:, HEAD_DIM:]
                o_ref.at[h][sel, :] = od[:, :HEAD_DIM] / den
                lse_rows = jnp.where(lane == hc * hps + h, m + jnp.log(den), lse_rows)
            lse_ref[sel, :] = lse_rows

    if dil == 1:
        one_class(0)
    else:
        def body(r, carry):
            one_class(r)
            return carry
        lax.fori_loop(0, dil, body, 0)


def _attn_call(qkv, group, batch, seq):
    window, dil = DILATED_GROUPS[group]
    span = window // dil
    nq, hps = ATTN_STEP[dil]
    slab = nq * QBLK * dil
    prev = QBLK * dil
    assert seq % slab == 0 and span <= QBLK and HEADS_PER_GROUP % hps == 0
    slabs_per_seq = seq // slab
    prev_per_seq = seq // prev
    n = batch * seq
    hblocks = HEADS_PER_GROUP // hps

    def cur(section):
        base = (section * ATT_HEADS + group * HEADS_PER_GROUP) // hps
        return pl.BlockSpec((hps, slab, LANES), lambda b, s, hc: (base + hc, b * slabs_per_seq + s, 0))

    def before(section):
        base = (section * ATT_HEADS + group * HEADS_PER_GROUP) // hps
        return pl.BlockSpec(
            (hps, prev, LANES),
            lambda b, s, hc: (base + hc, b * prev_per_seq + jnp.maximum(s * nq - 1, 0), 0))

    return pl.pallas_call(
        functools.partial(_attn_kernel, dil=dil, span=span, nq=nq, hps=hps),
        grid=(batch, slabs_per_seq, hblocks),
        in_specs=[cur(0), before(1), cur(1), before(2), cur(2)],
        out_specs=[
            pl.BlockSpec((hps, slab, LANES), lambda b, s, hc: (hc, b * slabs_per_seq + s, 0)),
            pl.BlockSpec((slab, LANES), lambda b, s, hc: (b * slabs_per_seq + s, 0)),
        ],
        out_shape=[
            jax.ShapeDtypeStruct((HEADS_PER_GROUP, n, LANES), F32),
            jax.ShapeDtypeStruct((n, LANES), F32),
        ],
        compiler_params=_params(("parallel", "arbitrary", "arbitrary")),
        name=f"attn_d{dil}",
    )(qkv, qkv, qkv, qkv, qkv)


def _mix_kernel(x_ref, act_ref, o0_ref, o1_ref, o2_ref, l0_ref, l1_ref, l2_ref, sga_ref, sgb_ref,
                wa_ref, wb_ref, wo_ref, g_ref, b_ref, wrh_ref, wrl_ref, br_ref, x1_ref, x1g_ref, logit_ref):
    tm = x_ref.shape[0]
    o_refs = (o0_ref, o1_ref, o2_ref)
    l_refs = (l0_ref, l1_ref, l2_ref)
    for sub in range(tm // SUB_ROWS):
        rows = pl.ds(sub * SUB_ROWS, SUB_ROWS)
        a_out = jnp.dot(act_ref[rows, :], wa_ref[...], preferred_element_type=F32)
        lses = [l[rows, :] for l in l_refs]
        heads = []
        for h in range(HEADS_PER_GROUP):
            lh = [l[:, h:h + 1] for l in lses]
            m = jnp.maximum(jnp.maximum(lh[0], lh[1]), lh[2])
            e = [jnp.exp(v - m) for v in lh]
            tot = e[0] + e[1] + e[2]
            acc = (e[0] / tot) * o_refs[0][h, rows, :]
            acc = acc + (e[1] / tot) * o_refs[1][h, rows, :]
            acc = acc + (e[2] / tot) * o_refs[2][h, rows, :]
            heads.append(acc.astype(BF16))
        merged = jnp.concatenate(heads, axis=1)
        b_out = jnp.dot(merged, wb_ref[...], preferred_element_type=F32)
        mixed_in = sga_ref[rows, :].astype(F32) * a_out + sgb_ref[rows, :].astype(F32) * b_out
        mixed = jnp.dot(mixed_in.astype(BF16), wo_ref[...], preferred_element_type=F32)
        x1 = _layer_norm(ALPHA * x_ref[rows, :] + mixed, g_ref[...], b_ref[...])
        x1_ref[rows, :] = x1
        _store_gather_rows(x1g_ref, sub * SUB_ROWS, x1)
        x_hi = x1.astype(BF16)
        x_lo = (x1 - x_hi.astype(F32)).astype(BF16)
        logits = jnp.dot(x_hi, wrh_ref[...], preferred_element_type=F32)
        logits = logits + jnp.dot(x_lo, wrh_ref[...], preferred_element_type=F32)
        logits = logits + jnp.dot(x_hi, wrl_ref[...], preferred_element_type=F32)
        logit_ref[rows, :] = logits + br_ref[...]


def _mix_call(x, act, os_, lses, proj_a, wa, wb, wo, g, b, wrh, wrl, br):
    n = x.shape[0]
    tm = min(MIX_TM, n)
    row = lambda w: pl.BlockSpec((tm, w), lambda i: (i, 0))
    o_spec = pl.BlockSpec((HEADS_PER_GROUP, tm, LANES), lambda i: (0, i, 0))
    return pl.pallas_call(
        _mix_kernel,
        grid=(n // tm,),
        in_specs=[
            row(D_MODEL), row(SGU_WIDTH),
            o_spec, o_spec, o_spec,
            row(LANES), row(LANES), row(LANES),
            pl.BlockSpec((tm, D_MODEL), lambda i: (i, 0)),
            pl.BlockSpec((tm, D_MODEL), lambda i: (i, _A_GB // D_MODEL)),
            _resident(wa.shape), _resident(wb.shape), _resident(wo.shape),
            _resident(g.shape), _resident(b.shape),
            _resident(wrh.shape), _resident(wrl.shape), _resident(br.shape),
        ],
        out_specs=[row(D_MODEL), pl.BlockSpec((tm * ROW_PITCH, LANES), lambda i: (i, 0)), row(ROUTER_LANES)],
        out_shape=[jax.ShapeDtypeStruct((n, D_MODEL), F32), jax.ShapeDtypeStruct((n * ROW_PITCH, LANES), U32),
                   jax.ShapeDtypeStruct((n, ROUTER_LANES), F32)],
        compiler_params=_params(("parallel",)),
        name="mix",
    )(x, act, *os_, *lses, proj_a, proj_a, wa, wb, wo, g, b, wrh, wrl, br)


def _start_row(idx_ref, r, src_hbm, dst, sem, priority):
    src = src_hbm.at[pl.ds(idx_ref[0, r] * ROW_PITCH, SLABS)]
    pltpu.make_async_copy(src, dst.at[pl.ds(r * ROW_PITCH, SLABS)], sem).start(priority=priority)


def _start_row_gather(idx_ref, n_rows, src_hbm, dst, sem):
    unroll = 8
    def body(blk, carry):
        for j in range(unroll):
            _start_row(idx_ref, blk * unroll + j, src_hbm, dst, sem, j % 2)
        return carry
    lax.fori_loop(0, n_rows // unroll, body, 0)


ANCHOR_ROWS = 8


def _interleaved_row_gather(idx_ref, n_rows, src_hbm, dst, sem, n_parts):
    per = n_rows // n_parts
    def issue(part, after=None):
        if after is not None:
            dst[pl.ds(n_rows * ROW_PITCH, ANCHOR_ROWS), :] = pltpu.bitcast(after[0:ANCHOR_ROWS, 0:LANES], U32)
        for r in range(part * per, (part + 1) * per):
            _start_row(idx_ref, r, src_hbm, dst, sem, r % 2)
    return issue


def _wait_row_gather(n_rows, src_hbm, dst, sem):
    pltpu.make_async_copy(src_hbm.at[pl.ds(0, n_rows * SLABS)], dst.at[pl.ds(0, n_rows * SLABS)], sem).wait()


def _gathered_rows(buf, first_row, n_rows):
    lo, hi = [], []
    for k in range(SLABS):
        w = buf[pl.ds(first_row * ROW_PITCH + k, n_rows, stride=ROW_PITCH), :]
        lo.append(pltpu.bitcast(w << 16, F32))
        hi.append(pltpu.bitcast(w & HI_MASK, F32))
    return jnp.concatenate(lo + hi, axis=1)


def _pack_words(val, k):
    half = D_MODEL // 2
    lo = val[:, k * LANES:(k + 1) * LANES].astype(BF16).astype(F32)
    hi = val[:, half + k * LANES:half + (k + 1) * LANES].astype(BF16).astype(F32)
    return (pltpu.bitcast(lo, U32) >> 16) | pltpu.bitcast(hi, U32)


def _store_gather_rows(ref, first_row, val):
    n_rows = val.shape[0]
    for k in range(SLABS):
        ref[pl.ds(first_row * ROW_PITCH + k, n_rows, stride=ROW_PITCH), :] = _pack_words(val, k)
    ref[pl.ds(first_row * ROW_PITCH + SLABS, n_rows, stride=ROW_PITCH), :] = jnp.zeros((n_rows, LANES), U32)


def _expert_kernel(blk_e_ref, n_used_ref, run_start_ref, next_e_ref, has_next_ref,
                   tok_cur_ref, tok_next_ref, x_hbm, w1_hbm, w3_hbm, w2_hbm,
                   o_ref, xbuf, w1s, w3s, w2s, w1b, w3b, w2b, sem, wsem, *, layer):
    i = pl.program_id(0)
    n_used = n_used_ref[0]
    slot = i % 2

    def weight_copies(e):
        return (pltpu.make_async_copy(w1_hbm.at[layer, e], w1s, wsem.at[0]),
                pltpu.make_async_copy(w3_hbm.at[layer, e], w3s, wsem.at[1]),
                pltpu.make_async_copy(w2_hbm.at[layer, e], w2s, wsem.at[2]))

    @pl.when((i == 0) & (n_used > 0))
    def _():
        _start_row_gather(tok_cur_ref, MOE_BLK, x_hbm, xbuf.at[0], sem.at[0])
        for c in weight_copies(blk_e_ref[0]):
            c.start()

    @pl.when((i < n_used) & (run_start_ref[i] == 1))
    def _():
        for c in weight_copies(blk_e_ref[i]):
            c.wait()
        w1b[...] = w1s[...].astype(BF16)
        w3b[...] = w3s[...].astype(BF16)
        w2b[...] = w2s[...].astype(BF16)

        @pl.when(has_next_ref[i] == 1)
        def _():
            for c in weight_copies(next_e_ref[i]):
                c.start()

    @pl.when(i < n_used)
    def _():
        _wait_row_gather(MOE_BLK, x_hbm, xbuf.at[slot], sem.at[slot])
        n_parts = 2 * (EXPERT_FF // EXPERT_CHUNK) + D_MODEL // EXPERT_OUT_CHUNK
        issue = _interleaved_row_gather(tok_next_ref, MOE_BLK, x_hbm, xbuf.at[1 - slot], sem.at[1 - slot], n_parts)
        part = 0
        last = None
        xb = _gathered_rows(xbuf.at[slot], 0, MOE_BLK).astype(BF16)
        hid = []
        for c in range(EXPERT_FF // EXPERT_CHUNK):
            cols = slice(c * EXPERT_CHUNK, (c + 1) * EXPERT_CHUNK)
            issue(part, last)
            h1 = jnp.dot(xb, w1b[:, cols], preferred_element_type=F32)
            issue(part + 1, h1)
            h3 = jnp.dot(xb, w3b[:, cols], preferred_element_type=F32)
            part += 2
            last = h3
            hid.append((jax.nn.silu(h1) * h3).astype(BF16))
        hid = jnp.concatenate(hid, axis=1)
        ys = []
        for c in range(D_MODEL // EXPERT_OUT_CHUNK):
            cols = slice(c * EXPERT_OUT_CHUNK, (c + 1) * EXPERT_OUT_CHUNK)
            issue(part, last)
            part += 1
            last = jnp.dot(hid, w2b[:, cols], preferred_element_type=F32)
            ys.append(last)
        _store_gather_rows(o_ref, 0, jnp.concatenate(ys, axis=1))

    @pl.when(i == n_used - 1)
    def _():
        _wait_row_gather(MOE_BLK, x_hbm, xbuf.at[1 - slot], sem.at[1 - slot])

    @pl.when(i >= n_used)
    def _():
        o_ref[...] = jnp.zeros_like(o_ref)


def _expert_plan(blk_e, n_used):
    n_blocks = blk_e.shape[0]
    j = jnp.arange(n_blocks, dtype=jnp.int32)
    run_start = ((j == 0) | (blk_e != jnp.roll(blk_e, 1))) & (j < n_used[0])
    start_idx = jnp.where(run_start, j, n_blocks)
    at_or_after = lax.cummin(start_idx, axis=0, reverse=True)
    nxt = jnp.concatenate([at_or_after[1:], jnp.full((1,), n_blocks, jnp.int32)])
    has_next = nxt < n_blocks
    next_e = blk_e[jnp.minimum(nxt, n_blocks - 1)]
    return run_start.astype(jnp.int32), next_e.astype(jnp.int32), has_next.astype(jnp.int32)


def _expert_call(blk_e, n_used, buf_tok, x1g, w1, w3, w2, layer):
    n_blocks = blk_e.shape[0]
    d = D_MODEL
    tok3 = buf_tok.reshape(n_blocks, 1, MOE_BLK)
    run_start, next_e, has_next = _expert_plan(blk_e, n_used)
    grid_spec = pltpu.PrefetchScalarGridSpec(
        num_scalar_prefetch=5,
        grid=(n_blocks,),
        in_specs=[
            pl.BlockSpec((None, 1, MOE_BLK), lambda i, e, u, *_: (i, 0, 0), memory_space=pltpu.SMEM),
            pl.BlockSpec((None, 1, MOE_BLK),
                         lambda i, e, u, *_: (jnp.minimum(i + 1, jnp.maximum(u[0] - 1, 0)), 0, 0),
                         memory_space=pltpu.SMEM),
            pl.BlockSpec(memory_space=pl.ANY),
            pl.BlockSpec(memory_space=pl.ANY),
            pl.BlockSpec(memory_space=pl.ANY),
            pl.BlockSpec(memory_space=pl.ANY),
        ],
        out_specs=pl.BlockSpec((MOE_BLK * ROW_PITCH, LANES), lambda i, e, u, *_: (i, 0)),
        scratch_shapes=[pltpu.VMEM((2, MOE_BLK * ROW_PITCH + ANCHOR_ROWS, LANES), U32),
                        pltpu.VMEM((d, EXPERT_FF), F32), pltpu.VMEM((d, EXPERT_FF), F32),
                        pltpu.VMEM((EXPERT_FF, d), F32),
                        pltpu.VMEM((d, EXPERT_FF), BF16), pltpu.VMEM((d, EXPERT_FF), BF16),
                        pltpu.VMEM((EXPERT_FF, d), BF16),
                        pltpu.SemaphoreType.DMA((2,)), pltpu.SemaphoreType.DMA((3,))],
    )
    return pl.pallas_call(
        functools.partial(_expert_kernel, layer=layer),
        grid_spec=grid_spec,
        out_shape=jax.ShapeDtypeStruct((n_blocks * MOE_BLK * ROW_PITCH, LANES), U32),
        compiler_params=_params(("arbitrary",)),
        name="experts",
    )(blk_e, n_used, run_start, next_e, has_next, tok3, tok3, x1g, w1, w3, w2)


def _final_kernel(pos_cur_ref, pos_next_ref, x1_ref, p_ref, info_ref, yb_hbm, wpg_ref, wpp_ref, g_ref, b_ref,
                  x2_ref, x2b_ref, ybuf, sem):
    i = pl.program_id(0)
    n_steps = pl.num_programs(0)
    slot = i % 2
    tm = p_ref.shape[0]

    @pl.when(i == 0)
    def _():
        _start_row_gather(pos_cur_ref, TOP_K * tm, yb_hbm, ybuf.at[0], sem.at[0])

    _wait_row_gather(TOP_K * tm, yb_hbm, ybuf.at[slot], sem.at[slot])
    n_sub = tm // SUB_ROWS
    n_chunks = D_MODEL // FINAL_CHUNK
    issue = _interleaved_row_gather(pos_next_ref, TOP_K * tm, yb_hbm, ybuf.at[1 - slot], sem.at[1 - slot],
                                    n_sub * n_chunks)
    last = None
    for sub in range(n_sub):
        rows = pl.ds(sub * SUB_ROWS, SUB_ROWS)
        x1 = x1_ref[rows, :]
        x1b = x1.astype(BF16)
        pb = p_ref[rows, :].astype(BF16)
        pre = []
        for c in range(n_chunks):
            cols = slice(c * FINAL_CHUNK, (c + 1) * FINAL_CHUNK)
            issue(sub * n_chunks + c, last)
            gate = jax.nn.sigmoid(jnp.dot(x1b, wpg_ref[:, cols], preferred_element_type=F32))
            last = gate * jnp.dot(pb, wpp_ref[:, cols], preferred_element_type=F32)
            pre.append(last)
        pre = jnp.concatenate(pre, axis=1)
        info = info_ref[rows, :]
        y = (_gathered_rows(ybuf.at[slot], sub * SUB_ROWS, SUB_ROWS) * info[:, INFO_GATE:INFO_GATE + 1]
             + _gathered_rows(ybuf.at[slot], tm + sub * SUB_ROWS, SUB_ROWS) * info[:, INFO_GATE + 1:INFO_GATE + 2])
        x2 = _layer_norm(ALPHA * x1 + y + pre, g_ref[...], b_ref[...])
        x2_ref[rows, :] = x2
        x2b_ref[rows, :] = x2.astype(BF16)

    @pl.when(i == n_steps - 1)
    def _():
        _wait_row_gather(TOP_K * tm, yb_hbm, ybuf.at[1 - slot], sem.at[1 - slot])


def _final_call(pos, info, x1, p, yb, wpg, wpp, g, b):
    n = p.shape[0]
    d = D_MODEL
    tm = min(FINAL_TM, n)
    steps = n // tm
    pos3 = pos.reshape(steps, tm, TOP_K).transpose(0, 2, 1).reshape(steps, 1, TOP_K * tm)
    row = lambda w: pl.BlockSpec((tm, w), lambda i: (i, 0))
    return pl.pallas_call(
        _final_kernel,
        grid=(steps,),
        in_specs=[
            pl.BlockSpec((None, 1, TOP_K * tm), lambda i: (i, 0, 0), memory_space=pltpu.SMEM),
            pl.BlockSpec((None, 1, TOP_K * tm), lambda i: (jnp.minimum(i + 1, steps - 1), 0, 0),
                         memory_space=pltpu.SMEM),
            row(d),
            row(PLE_DIM),
            row(LANES),
            pl.BlockSpec(memory_space=pl.ANY),
            _resident(wpg.shape), _resident(wpp.shape), _resident(g.shape), _resident(b.shape),
        ],
        out_specs=[row(d), row(d)],
        out_shape=[jax.ShapeDtypeStruct((n, d), F32), jax.ShapeDtypeStruct((n, d), BF16)],
        scratch_shapes=[pltpu.VMEM((2, TOP_K * tm * ROW_PITCH + ANCHOR_ROWS, LANES), U32),
                        pltpu.SemaphoreType.DMA((2,))],
        compiler_params=_params(("arbitrary",)),
        name="final",
    )(pos3, pos3, x1, p, info, yb, wpg, wpp, g, b)


INFO_DEST = 0
INFO_GATE = 2
EXPERT_LANE0 = N_EXPERT_GROUPS
ROUTE_TM = 512
META_ROWS = 256


def _lane_min_index(mask, lane_f):
    return jnp.min(jnp.where(mask, lane_f, float(LANES)), axis=1, keepdims=True)


def _route_kernel(logit_ref, tri_ref, info_ref, meta_ref, stash, run, pstart, *, n_blocks):
    phase = pl.program_id(0)
    i = pl.program_id(1)
    tm = logit_ref.shape[0]
    lane = lax.broadcasted_iota(jnp.int32, (tm, LANES), 1)
    lane_f = lane.astype(F32)
    rows = pl.ds(pl.multiple_of(i * tm, tm), tm)

    @pl.when((phase == 0) & (i == 0))
    def _():
        run[...] = jnp.zeros_like(run)

    @pl.when(phase == 0)
    def _():
        x = logit_ref[...]
        xg = jnp.where(lane < N_EXPERT_GROUPS, x, -jnp.inf)
        mg = jnp.max(xg, axis=1, keepdims=True)
        g_sel = _lane_min_index(xg == mg, lane_f)
        p_g = 1.0 / jnp.sum(jnp.exp(xg - mg), axis=1, keepdims=True)
        lo = EXPERT_LANE0 + EXPERTS_PER_GROUP * g_sel
        xe = jnp.where((lane_f >= lo) & (lane_f < lo + EXPERTS_PER_GROUP), x, -jnp.inf)
        v1 = jnp.max(xe, axis=1, keepdims=True)
        i1 = _lane_min_index(xe == v1, lane_f)
        xe2 = jnp.where(lane_f == i1, -jnp.inf, xe)
        v2 = jnp.max(xe2, axis=1, keepdims=True)
        i2 = _lane_min_index(xe2 == v2, lane_f)
        e21 = jnp.exp(v2 - v1)
        den = 1.0 + e21
        gate1 = (1.0 / den) * p_g
        gate2 = (e21 / den) * p_g
        o1 = lane_f == i1
        o2 = lane_f == i2
        both = jnp.where(o1 | o2, 1.0, 0.0)
        before = jnp.dot(tri_ref[...], both.astype(BF16), preferred_element_type=F32) + run[0:1, :]
        rank1 = jnp.sum(jnp.where(o1, before, 0.0), axis=1, keepdims=True)
        rank2 = jnp.sum(jnp.where(o2, before, 0.0), axis=1, keepdims=True)
        run[0:1, :] = run[0:1, :] + jnp.sum(both, axis=0, keepdims=True)
        info = jnp.where(lane == INFO_DEST, rank1, 0.0)
        info = jnp.where(lane == INFO_DEST + 1, rank2, info)
        info = jnp.where(lane == INFO_GATE, gate1, info)
        info = jnp.where(lane == INFO_GATE + 1, gate2, info)
        info = jnp.where(lane == INFO_GATE + 2, i1, info)
        info = jnp.where(lane == INFO_GATE + 3, i2, info)
        stash[rows, :] = info

    @pl.when((phase == 1) & (i == 0))
    def _():
        lane8 = lax.broadcasted_iota(jnp.int32, run.shape, 1)
        counts = jnp.where((lane8 >= EXPERT_LANE0) & (lane8 < EXPERT_LANE0 + N_EXPERTS), run[...], 0.0)
        padded = jnp.floor((counts + (MOE_BLK - 1)) * (1.0 / MOE_BLK)) * MOE_BLK
        pend = padded
        shift = 1
        while shift < LANES:
            pend = pend + jnp.where(lane8 >= shift, pltpu.roll(pend, shift, 1), 0.0)
            shift *= 2
        pstart[...] = pend - padded
        blk_row = lax.broadcasted_iota(jnp.int32, meta_ref.shape, 0)
        lane_m = lax.broadcasted_iota(jnp.int32, meta_ref.shape, 1)
        is_e = (lane_m >= EXPERT_LANE0) & (lane_m < EXPERT_LANE0 + N_EXPERTS)
        ended = is_e & (pend[0:1, :] <= (blk_row * MOE_BLK).astype(F32))
        blk_e = jnp.minimum(jnp.sum(jnp.where(ended, 1.0, 0.0), axis=1, keepdims=True), N_EXPERTS - 1.0)
        n_used = jnp.max(pend[0:1, :], axis=1, keepdims=True) * (1.0 / MOE_BLK)
        meta_ref[...] = jnp.where(blk_row == n_blocks, n_used, blk_e) + jnp.zeros(meta_ref.shape, F32)

    @pl.when(phase == 1)
    def _():
        info = stash[rows, :]
        i1 = info[:, INFO_GATE + 2:INFO_GATE + 3]
        i2 = info[:, INFO_GATE + 3:INFO_GATE + 4]
        start = pstart[0:1, :]
        s1 = jnp.sum(jnp.where(lane_f == i1, start, 0.0), axis=1, keepdims=True)
        s2 = jnp.sum(jnp.where(lane_f == i2, start, 0.0), axis=1, keepdims=True)
        out = jnp.where(lane == INFO_DEST, info + s1, info)
        out = jnp.where(lane == INFO_DEST + 1, info + s2, out)
        info_ref[...] = out


def _route_call(logits, n_blocks):
    n = logits.shape[0]
    tm = min(ROUTE_TM, n)
    assert n_blocks < META_ROWS
    tri = jnp.tril(jnp.ones((tm, tm), F32), k=-1).astype(BF16)
    return pl.pallas_call(
        functools.partial(_route_kernel, n_blocks=n_blocks),
        grid=(2, n // tm),
        in_specs=[
            pl.BlockSpec((tm, LANES), lambda ph, i: (jnp.where(ph == 0, i, n // tm - 1), 0)),
            _resident(tri.shape),
        ],
        out_specs=[
            pl.BlockSpec((tm, LANES), lambda ph, i: (i * ph, 0)),
            pl.BlockSpec((META_ROWS, LANES), lambda ph, i: (0, 0)),
        ],
        out_shape=[jax.ShapeDtypeStruct((n, LANES), F32), jax.ShapeDtypeStruct((META_ROWS, LANES), F32)],
        scratch_shapes=[pltpu.VMEM((n, LANES), F32), pltpu.VMEM((8, LANES), F32), pltpu.VMEM((8, LANES), F32)],
        compiler_params=_params(("arbitrary", "arbitrary")),
        name="route",
    )(logits, tri)


def _route(logits):
    n = logits.shape[0]
    a = n * TOP_K
    n_blocks = (a + N_EXPERTS * (MOE_BLK - 1) + MOE_BLK - 1) // MOE_BLK
    info, meta = _route_call(logits, n_blocks)
    dest = info[:, INFO_DEST:INFO_DEST + TOP_K].astype(jnp.int32)
    blk_e = meta[:n_blocks, 0].astype(jnp.int32)
    n_used = meta[n_blocks:n_blocks + 1, 0].astype(jnp.int32)
    tok = jnp.arange(a, dtype=jnp.int32) // TOP_K
    buf_tok = jnp.zeros((n_blocks * MOE_BLK,), jnp.int32).at[dest.reshape(a)].set(tok)
    return blk_e, n_used, buf_tok, dest, info


def kernel(x, p, positions, w_in, w_s, b_s, ln_v_g, ln_v_b, w_a, w_b, w_o, ln1_g, ln1_b, w_grp, b_grp, w_rt, b_rt, w1, w3, w2, w_pg, w_pp, ln2_g, ln2_b):
    batch, seq, d = x.shape
    depth = w_in.shape[0]
    n = batch * seq
    assert d == D_MODEL and w_in.shape[2] == PROJ_WIDTH
    cosf, sinf = _rotary_tables(positions)
    causal = jnp.tril(jnp.ones((CHUNK, CHUNK), F32))
    xf = x.reshape(n, d)
    xb = xf.astype(BF16)
    pad_r = ROUTER_LANES - N_EXPERT_GROUPS - N_EXPERTS
    w_pa, w_pb = _split_proj_weights(w_in)

    for i in range(depth):
        proj_a = _proj_a_call(xb, w_pa[i])
        qkv = _proj_b_call(xb, w_pb[i], cosf, sinf)
        ws = (w_s[i] * causal).astype(BF16)
        bs = jnp.repeat(b_s[i].T, SGU_GROUP_CH, axis=1)
        act = _sgu_call(proj_a, ws, bs, ln_v_g[i][None, :], ln_v_b[i][None, :])
        att = [_attn_call(qkv, g, batch, seq) for g in range(N_DIL)]
        wr = jnp.concatenate([w_grp[i], w_rt[i], jnp.zeros((d, pad_r), F32)], axis=1)
        wr_hi = wr.astype(BF16)
        wr_lo = (wr - wr_hi.astype(F32)).astype(BF16)
        br = jnp.concatenate([b_grp[i], b_rt[i], jnp.zeros((pad_r,), F32)])[None, :]
        x1, x1g, logits = _mix_call(xf, act, [a[0] for a in att], [a[1] for a in att], proj_a,
                                w_a[i].astype(BF16), w_b[i].astype(BF16), w_o[i].astype(BF16),
                                ln1_g[i][None, :], ln1_b[i][None, :], wr_hi, wr_lo, br)
        blk_e, n_used, buf_tok, pos, info = _route(logits)
        yb = _expert_call(blk_e, n_used, buf_tok, x1g, w1, w3, w2, i)
        xf, xb = _final_call(pos, info, x1, p[i].reshape(n, PLE_DIM), yb, w_pg[i].astype(BF16), w_pp[i].astype(BF16),
                             ln2_g[i][None, :], ln2_b[i][None, :])
    return xf.reshape(batch, seq, d)
```

```python
import functools

import jax
import jax.numpy as jnp
import numpy as np
from jax import lax
from jax.experimental import pallas as pl
from jax.experimental.pallas import tpu as pltpu

F32 = jnp.float32
BF16 = jnp.bfloat16

D_MODEL = 2048
DEPTH_FOR_DEEPNORM = 4
SGU_WIDTH = 1024
SGU_GROUPS = 8
SGU_GROUP_CH = SGU_WIDTH // SGU_GROUPS
CHUNK = 128
HEAD_DIM = 128
HEADS_PER_GROUP = 4
DILATED_GROUPS = ((128, 1), (512, 4), (2048, 16))
N_DIL = len(DILATED_GROUPS)
ATT_HEADS = HEADS_PER_GROUP * N_DIL
ATT_WIDTH = ATT_HEADS * HEAD_DIM
QBLK = 128
ROT_DIM = HEAD_DIM // 4
ROT_HALF = ROT_DIM // 2
ROPE_THETA = 500000.0
PROJ_WIDTH = 2 * SGU_WIDTH + 3 * ATT_WIDTH + 2 * D_MODEL
N_EXPERT_GROUPS = 4
EXPERTS_PER_GROUP = 8
N_EXPERTS = N_EXPERT_GROUPS * EXPERTS_PER_GROUP
TOP_K = 2
EXPERT_FF = 512
MOE_BLK = 256
PLE_DIM = 256
ALPHA = (2 * DEPTH_FOR_DEEPNORM) ** 0.25
LN_EPS = 1e-5
ROUTER_LANES = 128

LANES = 128
U32 = jnp.uint32
SLABS = D_MODEL // (2 * LANES)
ROW_PITCH = SLABS + 1
HI_MASK = np.uint32(0xFFFF0000)

PROJ_TM = 1024
PROJ_A_TN = 1024
PROJ_B_TN = ATT_WIDTH
PROJ_CHUNK = 256
SGU_TM = 512
MIX_TM = 256
FINAL_TM = 256
SUB_ROWS = 128
ROW_PRIORITY = (0,)
WEIGHT_PRIORITY = 1
CAST_ROWS = 128
EXPERT_CHUNK = 256
EXPERT_OUT_CHUNK = 512
FINAL_CHUNK = 512
ATTN_STEP = {1: (8, 4), 4: (2, 4), 16: (1, 2)}
ATTN_CLASS_UNROLL = 2
VMEM_LIMIT = 56 * 1024 * 1024

_REF_SPLITS = (0, SGU_WIDTH, 2 * SGU_WIDTH, 2 * SGU_WIDTH + ATT_WIDTH, 2 * SGU_WIDTH + 2 * ATT_WIDTH,
               2 * SGU_WIDTH + 3 * ATT_WIDTH, 2 * SGU_WIDTH + 3 * ATT_WIDTH + D_MODEL, PROJ_WIDTH)
_A_GB = D_MODEL
_A_U = 2 * D_MODEL
_A_V = 2 * D_MODEL + SGU_WIDTH
PROJ_A_WIDTH = 2 * D_MODEL + 2 * SGU_WIDTH


def _params(sem, vmem=VMEM_LIMIT):
    return pltpu.CompilerParams(dimension_semantics=sem, vmem_limit_bytes=vmem)


def _resident(shape):
    nd = len(shape)
    return pl.BlockSpec(shape, lambda *_: (0,) * nd, pipeline_mode=pl.Buffered(1))


def _layer_norm(y, g, b):
    mu = jnp.mean(y, axis=-1, keepdims=True)
    yc = y - mu
    var = jnp.mean(yc * yc, axis=-1, keepdims=True)
    return yc * lax.rsqrt(var + LN_EPS) * g + b


def _split_proj_weights(w):
    sec = [w[:, :, _REF_SPLITS[k]:_REF_SPLITS[k + 1]] for k in range(7)]
    wa = jnp.concatenate([sec[5], sec[6], sec[0], sec[1]], axis=2).astype(BF16)
    wb = w[:, :, _REF_SPLITS[2]:_REF_SPLITS[5]].astype(BF16)
    return wa, wb


def _rotary_tables(positions):
    inv_freq = ROPE_THETA ** (-jnp.arange(0, ROT_DIM, 2, dtype=F32) / ROT_DIM)
    ang = positions.astype(F32).reshape(-1)[:, None] * inv_freq
    cos, sin = jnp.cos(ang), jnp.sin(ang)
    n = ang.shape[0]
    cosf = jnp.concatenate([cos, cos, jnp.ones((n, HEAD_DIM - ROT_DIM), F32)], axis=1)
    sinf = jnp.concatenate([-sin, sin, jnp.zeros((n, HEAD_DIM - ROT_DIM), F32)], axis=1)
    return cosf, sinf


def _proj_a_kernel(x_ref, w_ref, o_ref):
    j = pl.program_id(1)
    tn = o_ref.shape[1]

    def run(act):
        for c in range(tn // PROJ_CHUNK):
            cols = slice(c * PROJ_CHUNK, (c + 1) * PROJ_CHUNK)
            acc = jnp.dot(x_ref[...], w_ref[:, cols], preferred_element_type=F32)
            o_ref[:, cols] = act(acc).astype(o_ref.dtype)

    @pl.when(j < _A_U // tn)
    def _():
        run(jax.nn.sigmoid)

    @pl.when(j >= _A_U // tn)
    def _():
        run(jax.nn.gelu)


def _proj_a_call(xb, w):
    n, d = xb.shape
    tm = min(PROJ_TM, n)
    tn = PROJ_A_TN
    assert _A_U % tn == 0 and PROJ_A_WIDTH % tn == 0
    return pl.pallas_call(
        _proj_a_kernel,
        grid=(n // tm, PROJ_A_WIDTH // tn),
        in_specs=[
            pl.BlockSpec((tm, d), lambda i, j: (i, 0)),
            pl.BlockSpec((d, tn), lambda i, j: (0, j)),
        ],
        out_specs=pl.BlockSpec((tm, tn), lambda i, j: (i, j)),
        out_shape=jax.ShapeDtypeStruct((n, PROJ_A_WIDTH), BF16),
        compiler_params=_params(("parallel", "arbitrary")),
        name="proj_a",
    )(xb, w)


def _proj_b_kernel(x_ref, w_ref, cos_ref, sin_ref, o_ref):
    j = pl.program_id(1)
    heads_per_chunk = PROJ_CHUNK // HEAD_DIM
    lane = lax.broadcasted_iota(jnp.int32, cos_ref.shape, 1)

    def run(rotate):
        for c in range(PROJ_B_TN // PROJ_CHUNK):
            cols = slice(c * PROJ_CHUNK, (c + 1) * PROJ_CHUNK)
            acc = jnp.dot(x_ref[...], w_ref[:, cols], preferred_element_type=F32)
            for h in range(heads_per_chunk):
                xh = acc[:, h * HEAD_DIM:(h + 1) * HEAD_DIM]
                if rotate:
                    partner = jnp.where(lane < ROT_HALF, pltpu.roll(xh, HEAD_DIM - ROT_HALF, 1),
                                        pltpu.roll(xh, ROT_HALF, 1))
                    xh = xh * cos_ref[...] + partner * sin_ref[...]
                o_ref[c * heads_per_chunk + h] = xh

    @pl.when(j < 2)
    def _():
        run(True)

    @pl.when(j >= 2)
    def _():
        run(False)


def _proj_b_call(xb, w, cosf, sinf):
    n, d = xb.shape
    tm = min(PROJ_TM, n)
    slabs = PROJ_B_TN // HEAD_DIM
    return pl.pallas_call(
        _proj_b_kernel,
        grid=(n // tm, 3),
        in_specs=[
            pl.BlockSpec((tm, d), lambda i, j: (i, 0)),
            pl.BlockSpec((d, PROJ_B_TN), lambda i, j: (0, j)),
            pl.BlockSpec((tm, HEAD_DIM), lambda i, j: (i, 0)),
            pl.BlockSpec((tm, HEAD_DIM), lambda i, j: (i, 0)),
        ],
        out_specs=pl.BlockSpec((slabs, tm, HEAD_DIM), lambda i, j: (j, i, 0)),
        out_shape=jax.ShapeDtypeStruct((3 * slabs, n, HEAD_DIM), F32),
        compiler_params=_params(("parallel", "arbitrary")),
        name="proj_b",
    )(xb, w, cosf, sinf)


def _sgu_kernel(u_ref, v_ref, ws_ref, bs_ref, g_ref, b_ref, o_ref):
    vn = _layer_norm(v_ref[...].astype(F32), g_ref[...], b_ref[...]).astype(BF16)
    tm = o_ref.shape[0]
    for c in range(tm // CHUNK):
        rows = slice(c * CHUNK, (c + 1) * CHUNK)
        for g in range(SGU_GROUPS):
            cols = slice(g * SGU_GROUP_CH, (g + 1) * SGU_GROUP_CH)
            z = jnp.dot(ws_ref[g], vn[rows, cols], preferred_element_type=F32) + bs_ref[:, cols]
            o_ref[rows, cols] = (u_ref[rows, cols].astype(F32) * z).astype(o_ref.dtype)


def _sgu_call(proj_a, ws, bs, ln_g, ln_b):
    n = proj_a.shape[0]
    tm = min(SGU_TM, n)
    u_blk, v_blk = _A_U // SGU_WIDTH, _A_V // SGU_WIDTH
    return pl.pallas_call(
        _sgu_kernel,
        grid=(n // tm,),
        in_specs=[
            pl.BlockSpec((tm, SGU_WIDTH), lambda i: (i, u_blk)),
            pl.BlockSpec((tm, SGU_WIDTH), lambda i: (i, v_blk)),
            _resident(ws.shape),
            _resident(bs.shape),
            _resident(ln_g.shape),
            _resident(ln_b.shape),
        ],
        out_specs=pl.BlockSpec((tm, SGU_WIDTH), lambda i: (i, 0)),
        out_shape=jax.ShapeDtypeStruct((n, SGU_WIDTH), BF16),
        compiler_params=_params(("parallel",)),
        name="sgu",
    )(proj_a, proj_a, ws, bs, ln_g, ln_b)


def _attn_kernel(q_ref, kp_ref, k_ref, vp_ref, v_ref, o_ref, lse_ref, *, dil, span, nq, hps):
    slab_idx = pl.program_id(1)
    hc = pl.program_id(2)
    qi = lax.broadcasted_iota(jnp.int32, (QBLK, QBLK), 0)
    kj = lax.broadcasted_iota(jnp.int32, (QBLK, QBLK), 1)
    dist_prev = qi + QBLK - kj
    dist_cur = qi - kj
    in_prev = (dist_prev >= 0) & (dist_prev <= span)
    mask_cur = (dist_cur >= 0) & (dist_cur <= span)
    scale = HEAD_DIM ** -0.5
    nt = (((1,), (1,)), ((), ()))
    lane = lax.broadcasted_iota(jnp.int32, (QBLK, LANES), 1)
    ones = jnp.ones((QBLK, HEAD_DIM), BF16)

    @pl.when(hc == 0)
    def _():
        lse_ref[...] = jnp.zeros_like(lse_ref)

    def rows(r, blk):
        return pl.ds(blk * QBLK * dil + r, QBLK, stride=dil) if dil > 1 else pl.ds(blk * QBLK, QBLK)

    def one_class(r):
        for blk in range(nq):
            if blk == 0:
                mask_prev = in_prev & (slab_idx > 0)
            else:
                mask_prev = in_prev
            sel = rows(r, blk)
            lse_rows = lse_ref[sel, :]
            for h in range(hps):
                q = (q_ref.at[h][sel, :] * scale).astype(BF16)
                if blk == 0:
                    kp = kp_ref.at[h][rows(r, 0), :].astype(BF16)
                    vp = vp_ref.at[h][rows(r, 0), :].astype(BF16)
                else:
                    kp = k_ref.at[h][rows(r, blk - 1), :].astype(BF16)
                    vp = v_ref.at[h][rows(r, blk - 1), :].astype(BF16)
                kc = k_ref.at[h][sel, :].astype(BF16)
                vc = v_ref.at[h][sel, :].astype(BF16)
                sp = jnp.where(mask_prev, lax.dot_general(q, kp, nt, preferred_element_type=F32), -jnp.inf)
                sc = jnp.where(mask_cur, lax.dot_general(q, kc, nt, preferred_element_type=F32), -jnp.inf)
                m = jnp.max(jnp.maximum(sp, sc), axis=-1, keepdims=True)
                ep = jnp.exp((sp - m).astype(BF16))
                ec = jnp.exp((sc - m).astype(BF16))
                od = jnp.dot(ep, jnp.concatenate([vp, ones], axis=1), preferred_element_type=F32)
                od = od + jnp.dot(ec, jnp.concatenate([vc, ones], axis=1), preferred_element_type=F32)
                den = od[:, HEAD_DIM:]
                o_ref.at[h][sel, :] = od[:, :HEAD_DIM] / den
                lse_rows = jnp.where(lane == hc * hps + h, m + jnp.log(den), lse_rows)
            lse_ref[sel, :] = lse_rows

    if dil == 1:
        one_class(0)
    else:
        def body(r2, carry):
            for u in range(ATTN_CLASS_UNROLL):
                one_class(r2 * ATTN_CLASS_UNROLL + u)
            return carry
        lax.fori_loop(0, dil // ATTN_CLASS_UNROLL, body, 0)


def _attn_call(qkv, group, batch, seq):
    window, dil = DILATED_GROUPS[group]
    span = window // dil
    nq, hps = ATTN_STEP[dil]
    slab = nq * QBLK * dil
    prev = QBLK * dil
    assert seq % slab == 0 and span <= QBLK and HEADS_PER_GROUP % hps == 0
    slabs_per_seq = seq // slab
    prev_per_seq = seq // prev
    n = batch * seq
    hblocks = HEADS_PER_GROUP // hps

    def cur(section):
        base = (section * ATT_HEADS + group * HEADS_PER_GROUP) // hps
        return pl.BlockSpec((hps, slab, LANES), lambda b, s, hc: (base + hc, b * slabs_per_seq + s, 0))

    def before(section):
        base = (section * ATT_HEADS + group * HEADS_PER_GROUP) // hps
        return pl.BlockSpec(
            (hps, prev, LANES),
            lambda b, s, hc: (base + hc, b * prev_per_seq + jnp.maximum(s * nq - 1, 0), 0))

    return pl.pallas_call(
        functools.partial(_attn_kernel, dil=dil, span=span, nq=nq, hps=hps),
        grid=(batch, slabs_per_seq, hblocks),
        in_specs=[cur(0), before(1), cur(1), before(2), cur(2)],
        out_specs=[
            pl.BlockSpec((hps, slab, LANES), lambda b, s, hc: (hc, b * slabs_per_seq + s, 0)),
            pl.BlockSpec((slab, LANES), lambda b, s, hc: (b * slabs_per_seq + s, 0)),
        ],
        out_shape=[
            jax.ShapeDtypeStruct((HEADS_PER_GROUP, n, LANES), F32),
            jax.ShapeDtypeStruct((n, LANES), F32),
        ],
        compiler_params=_params(("parallel", "arbitrary", "arbitrary")),
        name=f"attn_d{dil}",
    )(qkv, qkv, qkv, qkv, qkv)


def _mix_kernel(x_ref, act_ref, o0_ref, o1_ref, o2_ref, l0_ref, l1_ref, l2_ref, sga_ref, sgb_ref,
                wa_ref, wb_ref, wo_ref, g_ref, b_ref, wrh_ref, wrl_ref, br_ref, x1_ref, x1g_ref, logit_ref):
    tm = x_ref.shape[0]
    o_refs = (o0_ref, o1_ref, o2_ref)
    l_refs = (l0_ref, l1_ref, l2_ref)
    for sub in range(tm // SUB_ROWS):
        rows = pl.ds(sub * SUB_ROWS, SUB_ROWS)
        a_out = jnp.dot(act_ref[rows, :], wa_ref[...], preferred_element_type=F32)
        lses = [l[rows, :] for l in l_refs]
        heads = []
        for h in range(HEADS_PER_GROUP):
            lh = [l[:, h:h + 1] for l in lses]
            m = jnp.maximum(jnp.maximum(lh[0], lh[1]), lh[2])
            e = [jnp.exp(v - m) for v in lh]
            tot = e[0] + e[1] + e[2]
            acc = (e[0] / tot) * o_refs[0][h, rows, :]
            acc = acc + (e[1] / tot) * o_refs[1][h, rows, :]
            acc = acc + (e[2] / tot) * o_refs[2][h, rows, :]
            heads.append(acc.astype(BF16))
        merged = jnp.concatenate(heads, axis=1)
        b_out = jnp.dot(merged, wb_ref[...], preferred_element_type=F32)
        mixed_in = sga_ref[rows, :].astype(F32) * a_out + sgb_ref[rows, :].astype(F32) * b_out
        mixed = jnp.dot(mixed_in.astype(BF16), wo_ref[...], preferred_element_type=F32)
        x1 = _layer_norm(ALPHA * x_ref[rows, :] + mixed, g_ref[...], b_ref[...])
        x1_ref[rows, :] = x1
        _store_gather_rows(x1g_ref, sub * SUB_ROWS, x1)
        x_hi = x1.astype(BF16)
        x_lo = (x1 - x_hi.astype(F32)).astype(BF16)
        logits = jnp.dot(x_hi, wrh_ref[...], preferred_element_type=F32)
        logits = logits + jnp.dot(x_lo, wrh_ref[...], preferred_element_type=F32)
        logits = logits + jnp.dot(x_hi, wrl_ref[...], preferred_element_type=F32)
        logit_ref[rows, :] = logits + br_ref[...]


def _mix_call(x, act, os_, lses, proj_a, wa, wb, wo, g, b, wrh, wrl, br):
    n = x.shape[0]
    tm = min(MIX_TM, n)
    row = lambda w: pl.BlockSpec((tm, w), lambda i: (i, 0))
    o_spec = pl.BlockSpec((HEADS_PER_GROUP, tm, LANES), lambda i: (0, i, 0))
    return pl.pallas_call(
        _mix_kernel,
        grid=(n // tm,),
        in_specs=[
            row(D_MODEL), row(SGU_WIDTH),
            o_spec, o_spec, o_spec,
            row(LANES), row(LANES), row(LANES),
            pl.BlockSpec((tm, D_MODEL), lambda i: (i, 0)),
            pl.BlockSpec((tm, D_MODEL), lambda i: (i, _A_GB // D_MODEL)),
            _resident(wa.shape), _resident(wb.shape), _resident(wo.shape),
            _resident(g.shape), _resident(b.shape),
            _resident(wrh.shape), _resident(wrl.shape), _resident(br.shape),
        ],
        out_specs=[row(D_MODEL), pl.BlockSpec((tm * ROW_PITCH, LANES), lambda i: (i, 0)), row(ROUTER_LANES)],
        out_shape=[jax.ShapeDtypeStruct((n, D_MODEL), F32), jax.ShapeDtypeStruct((n * ROW_PITCH, LANES), U32),
                   jax.ShapeDtypeStruct((n, ROUTER_LANES), F32)],
        compiler_params=_params(("parallel",)),
        name="mix",
    )(x, act, *os_, *lses, proj_a, proj_a, wa, wb, wo, g, b, wrh, wrl, br)


def _start_row(idx_ref, r, src_hbm, dst, sem, priority):
    src = src_hbm.at[pl.ds(idx_ref[0, r] * ROW_PITCH, SLABS)]
    pltpu.make_async_copy(src, dst.at[pl.ds(r * ROW_PITCH, SLABS)], sem).start(priority=priority)


def _start_row_gather(idx_ref, n_rows, src_hbm, dst, sem, priorities=(0, 1)):
    unroll = 8
    def body(blk, carry):
        for j in range(unroll):
            _start_row(idx_ref, blk * unroll + j, src_hbm, dst, sem, priorities[j % len(priorities)])
        return carry
    lax.fori_loop(0, n_rows // unroll, body, 0)


ANCHOR_ROWS = 8


def _interleaved_row_gather(idx_ref, n_rows, src_hbm, dst, sem, n_parts, priorities=(0, 1)):
    per = n_rows // n_parts
    def issue(part, after=None):
        if after is not None:
            dst[pl.ds(n_rows * ROW_PITCH, ANCHOR_ROWS), :] = pltpu.bitcast(after[0:ANCHOR_ROWS, 0:LANES], U32)
        for r in range(part * per, (part + 1) * per):
            _start_row(idx_ref, r, src_hbm, dst, sem, priorities[r % len(priorities)])
    return issue


def _wait_row_gather(n_rows, src_hbm, dst, sem):
    pltpu.make_async_copy(src_hbm.at[pl.ds(0, n_rows * SLABS)], dst.at[pl.ds(0, n_rows * SLABS)], sem).wait()


def _gathered_rows(buf, first_row, n_rows):
    lo, hi = [], []
    for k in range(SLABS):
        w = buf[pl.ds(first_row * ROW_PITCH + k, n_rows, stride=ROW_PITCH), :]
        lo.append(pltpu.bitcast(w << 16, F32))
        hi.append(pltpu.bitcast(w & HI_MASK, F32))
    return jnp.concatenate(lo + hi, axis=1)


def _pack_words(val, k):
    half = D_MODEL // 2
    lo = val[:, k * LANES:(k + 1) * LANES].astype(BF16).astype(F32)
    hi = val[:, half + k * LANES:half + (k + 1) * LANES].astype(BF16).astype(F32)
    return (pltpu.bitcast(lo, U32) >> 16) | pltpu.bitcast(hi, U32)


def _store_gather_rows(ref, first_row, val):
    n_rows = val.shape[0]
    for k in range(SLABS):
        ref[pl.ds(first_row * ROW_PITCH + k, n_rows, stride=ROW_PITCH), :] = _pack_words(val, k)
    ref[pl.ds(first_row * ROW_PITCH + SLABS, n_rows, stride=ROW_PITCH), :] = jnp.zeros((n_rows, LANES), U32)


def _expert_kernel(blk_e_ref, n_used_ref, run_start_ref, next_e_ref, has_next_ref,
                   tok_cur_ref, tok_next_ref, x_hbm, w1_hbm, w3_hbm, w2_hbm,
                   o_ref, xbuf, w1s, w3s, w2s, w1b, w3b, w2b, sem, wsem, *, layer):
    i = pl.program_id(0)
    n_used = n_used_ref[0]
    slot = i % 2

    def weight_copies(e):
        return (pltpu.make_async_copy(w1_hbm.at[layer, e], w1s, wsem.at[0]),
                pltpu.make_async_copy(w3_hbm.at[layer, e], w3s, wsem.at[1]),
                pltpu.make_async_copy(w2_hbm.at[layer, e], w2s, wsem.at[2]))

    @pl.when((i == 0) & (n_used > 0))
    def _():
        _start_row_gather(tok_cur_ref, MOE_BLK, x_hbm, xbuf.at[0], sem.at[0], ROW_PRIORITY)
        for c in weight_copies(blk_e_ref[0]):
            c.start(priority=WEIGHT_PRIORITY)

    @pl.when((i < n_used) & (run_start_ref[i] == 1))
    def _():
        for c in weight_copies(blk_e_ref[i]):
            c.wait()
        for src, dst in ((w1s, w1b), (w3s, w3b), (w2s, w2b)):
            for r0 in range(0, src.shape[0], CAST_ROWS):
                dst[pl.ds(r0, CAST_ROWS), :] = src[pl.ds(r0, CAST_ROWS), :].astype(BF16)

        @pl.when(has_next_ref[i] == 1)
        def _():
            for c in weight_copies(next_e_ref[i]):
                c.start(priority=WEIGHT_PRIORITY)

    @pl.when(i < n_used)
    def _():
        _wait_row_gather(MOE_BLK, x_hbm, xbuf.at[slot], sem.at[slot])
        n_parts = 2 * (EXPERT_FF // EXPERT_CHUNK) + D_MODEL // EXPERT_OUT_CHUNK
        issue = _interleaved_row_gather(tok_next_ref, MOE_BLK, x_hbm, xbuf.at[1 - slot], sem.at[1 - slot], n_parts,
                                        ROW_PRIORITY)
        part = 0
        last = None
        xb = _gathered_rows(xbuf.at[slot], 0, MOE_BLK).astype(BF16)
        hid = []
        for c in range(EXPERT_FF // EXPERT_CHUNK):
            cols = slice(c * EXPERT_CHUNK, (c + 1) * EXPERT_CHUNK)
            issue(part, last)
            h1 = jnp.dot(xb, w1b[:, cols], preferred_element_type=F32)
            issue(part + 1, h1)
            h3 = jnp.dot(xb, w3b[:, cols], preferred_element_type=F32)
            part += 2
            last = h3
            hid.append((jax.nn.silu(h1) * h3).astype(BF16))
        hid = jnp.concatenate(hid, axis=1)
        ys = []
        for c in range(D_MODEL // EXPERT_OUT_CHUNK):
            cols = slice(c * EXPERT_OUT_CHUNK, (c + 1) * EXPERT_OUT_CHUNK)
            issue(part, last)
            part += 1
            last = jnp.dot(hid, w2b[:, cols], preferred_element_type=F32)
            ys.append(last)
        _store_gather_rows(o_ref, 0, jnp.concatenate(ys, axis=1))

    @pl.when(i == n_used - 1)
    def _():
        _wait_row_gather(MOE_BLK, x_hbm, xbuf.at[1 - slot], sem.at[1 - slot])

    @pl.when(i >= n_used)
    def _():
        o_ref[...] = jnp.zeros_like(o_ref)


def _expert_plan(blk_e, n_used):
    n_blocks = blk_e.shape[0]
    j = jnp.arange(n_blocks, dtype=jnp.int32)
    run_start = ((j == 0) | (blk_e != jnp.roll(blk_e, 1))) & (j < n_used[0])
    start_idx = jnp.where(run_start, j, n_blocks)
    at_or_after = lax.cummin(start_idx, axis=0, reverse=True)
    nxt = jnp.concatenate([at_or_after[1:], jnp.full((1,), n_blocks, jnp.int32)])
    has_next = nxt < n_blocks
    next_e = blk_e[jnp.minimum(nxt, n_blocks - 1)]
    return run_start.astype(jnp.int32), next_e.astype(jnp.int32), has_next.astype(jnp.int32)


def _expert_call(blk_e, n_used, buf_tok, x1g, w1, w3, w2, layer):
    n_blocks = blk_e.shape[0]
    d = D_MODEL
    tok3 = buf_tok.reshape(n_blocks, 1, MOE_BLK)
    run_start, next_e, has_next = _expert_plan(blk_e, n_used)
    grid_spec = pltpu.PrefetchScalarGridSpec(
        num_scalar_prefetch=5,
        grid=(n_blocks,),
        in_specs=[
            pl.BlockSpec((None, 1, MOE_BLK), lambda i, e, u, *_: (i, 0, 0), memory_space=pltpu.SMEM),
            pl.BlockSpec((None, 1, MOE_BLK),
                         lambda i, e, u, *_: (jnp.minimum(i + 1, jnp.maximum(u[0] - 1, 0)), 0, 0),
                         memory_space=pltpu.SMEM),
            pl.BlockSpec(memory_space=pl.ANY),
            pl.BlockSpec(memory_space=pl.ANY),
            pl.BlockSpec(memory_space=pl.ANY),
            pl.BlockSpec(memory_space=pl.ANY),
        ],
        out_specs=pl.BlockSpec((MOE_BLK * ROW_PITCH, LANES), lambda i, e, u, *_: (i, 0)),
        scratch_shapes=[pltpu.VMEM((2, MOE_BLK * ROW_PITCH + ANCHOR_ROWS, LANES), U32),
                        pltpu.VMEM((d, EXPERT_FF), F32), pltpu.VMEM((d, EXPERT_FF), F32),
                        pltpu.VMEM((EXPERT_FF, d), F32),
                        pltpu.VMEM((d, EXPERT_FF), BF16), pltpu.VMEM((d, EXPERT_FF), BF16),
                        pltpu.VMEM((EXPERT_FF, d), BF16),
                        pltpu.SemaphoreType.DMA((2,)), pltpu.SemaphoreType.DMA((3,))],
    )
    return pl.pallas_call(
        functools.partial(_expert_kernel, layer=layer),
        grid_spec=grid_spec,
        out_shape=jax.ShapeDtypeStruct((n_blocks * MOE_BLK * ROW_PITCH, LANES), U32),
        compiler_params=_params(("arbitrary",)),
        name="experts",
    )(blk_e, n_used, run_start, next_e, has_next, tok3, tok3, x1g, w1, w3, w2)


def _final_kernel(pos_cur_ref, pos_next_ref, x1_ref, p_ref, info_ref, yb_hbm, wpg_ref, wpp_ref, g_ref, b_ref,
                  x2_ref, x2b_ref, ybuf, sem):
    i = pl.program_id(0)
    n_steps = pl.num_programs(0)
    slot = i % 2
    tm = p_ref.shape[0]

    @pl.when(i == 0)
    def _():
        _start_row_gather(pos_cur_ref, TOP_K * tm, yb_hbm, ybuf.at[0], sem.at[0])

    _wait_row_gather(TOP_K * tm, yb_hbm, ybuf.at[slot], sem.at[slot])
    n_sub = tm // SUB_ROWS
    n_chunks = D_MODEL // FINAL_CHUNK
    issue = _interleaved_row_gather(pos_next_ref, TOP_K * tm, yb_hbm, ybuf.at[1 - slot], sem.at[1 - slot],
                                    n_sub * n_chunks)
    last = None
    for sub in range(n_sub):
        rows = pl.ds(sub * SUB_ROWS, SUB_ROWS)
        x1 = x1_ref[rows, :]
        x1b = x1.astype(BF16)
        pb = p_ref[rows, :].astype(BF16)
        pre = []
        for c in range(n_chunks):
            cols = slice(c * FINAL_CHUNK, (c + 1) * FINAL_CHUNK)
            issue(sub * n_chunks + c, last)
            gate = jax.nn.sigmoid(jnp.dot(x1b, wpg_ref[:, cols], preferred_element_type=F32))
            last = gate * jnp.dot(pb, wpp_ref[:, cols], preferred_element_type=F32)
            pre.append(last)
        pre = jnp.concatenate(pre, axis=1)
        info = info_ref[rows, :]
        y = (_gathered_rows(ybuf.at[slot], sub * SUB_ROWS, SUB_ROWS) * info[:, INFO_GATE:INFO_GATE + 1]
             + _gathered_rows(ybuf.at[slot], tm + sub * SUB_ROWS, SUB_ROWS) * info[:, INFO_GATE + 1:INFO_GATE + 2])
        x2 = _layer_norm(ALPHA * x1 + y + pre, g_ref[...], b_ref[...])
        x2_ref[rows, :] = x2
        x2b_ref[rows, :] = x2.astype(BF16)

    @pl.when(i == n_steps - 1)
    def _():
        _wait_row_gather(TOP_K * tm, yb_hbm, ybuf.at[1 - slot], sem.at[1 - slot])


def _final_call(pos, info, x1, p, yb, wpg, wpp, g, b):
    n = p.shape[0]
    d = D_MODEL
    tm = min(FINAL_TM, n)
    steps = n // tm
    pos3 = pos.reshape(steps, tm, TOP_K).transpose(0, 2, 1).reshape(steps, 1, TOP_K * tm)
    row = lambda w: pl.BlockSpec((tm, w), lambda i: (i, 0))
    return pl.pallas_call(
        _final_kernel,
        grid=(steps,),
        in_specs=[
            pl.BlockSpec((None, 1, TOP_K * tm), lambda i: (i, 0, 0), memory_space=pltpu.SMEM),
            pl.BlockSpec((None, 1, TOP_K * tm), lambda i: (jnp.minimum(i + 1, steps - 1), 0, 0),
                         memory_space=pltpu.SMEM),
            row(d),
            row(PLE_DIM),
            row(LANES),
            pl.BlockSpec(memory_space=pl.ANY),
            _resident(wpg.shape), _resident(wpp.shape), _resident(g.shape), _resident(b.shape),
        ],
        out_specs=[row(d), row(d)],
        out_shape=[jax.ShapeDtypeStruct((n, d), F32), jax.ShapeDtypeStruct((n, d), BF16)],
        scratch_shapes=[pltpu.VMEM((2, TOP_K * tm * ROW_PITCH + ANCHOR_ROWS, LANES), U32),
                        pltpu.SemaphoreType.DMA((2,))],
        compiler_params=_params(("arbitrary",)),
        name="final",
    )(pos3, pos3, x1, p, info, yb, wpg, wpp, g, b)


INFO_DEST = 0
INFO_GATE = 2
EXPERT_LANE0 = N_EXPERT_GROUPS
ROUTE_TM = 512
META_ROWS = 256


def _lane_min_index(mask, lane_f):
    return jnp.min(jnp.where(mask, lane_f, float(LANES)), axis=1, keepdims=True)


def _route_kernel(logit_ref, tri_ref, info_ref, meta_ref, stash, run, pstart, *, n_blocks):
    phase = pl.program_id(0)
    i = pl.program_id(1)
    tm = logit_ref.shape[0]
    lane = lax.broadcasted_iota(jnp.int32, (tm, LANES), 1)
    lane_f = lane.astype(F32)
    rows = pl.ds(pl.multiple_of(i * tm, tm), tm)

    @pl.when((phase == 0) & (i == 0))
    def _():
        run[...] = jnp.zeros_like(run)

    @pl.when(phase == 0)
    def _():
        x = logit_ref[...]
        xg = jnp.where(lane < N_EXPERT_GROUPS, x, -jnp.inf)
        mg = jnp.max(xg, axis=1, keepdims=True)
        g_sel = _lane_min_index(xg == mg, lane_f)
        p_g = 1.0 / jnp.sum(jnp.exp(xg - mg), axis=1, keepdims=True)
        lo = EXPERT_LANE0 + EXPERTS_PER_GROUP * g_sel
        xe = jnp.where((lane_f >= lo) & (lane_f < lo + EXPERTS_PER_GROUP), x, -jnp.inf)
        v1 = jnp.max(xe, axis=1, keepdims=True)
        i1 = _lane_min_index(xe == v1, lane_f)
        xe2 = jnp.where(lane_f == i1, -jnp.inf, xe)
        v2 = jnp.max(xe2, axis=1, keepdims=True)
        i2 = _lane_min_index(xe2 == v2, lane_f)
        e21 = jnp.exp(v2 - v1)
        den = 1.0 + e21
        gate1 = (1.0 / den) * p_g
        gate2 = (e21 / den) * p_g
        o1 = lane_f == i1
        o2 = lane_f == i2
        both = jnp.where(o1 | o2, 1.0, 0.0)
        before = jnp.dot(tri_ref[...], both.astype(BF16), preferred_element_type=F32) + run[0:1, :]
        rank1 = jnp.sum(jnp.where(o1, before, 0.0), axis=1, keepdims=True)
        rank2 = jnp.sum(jnp.where(o2, before, 0.0), axis=1, keepdims=True)
        run[0:1, :] = run[0:1, :] + jnp.sum(both, axis=0, keepdims=True)
        info = jnp.where(lane == INFO_DEST, rank1, 0.0)
        info = jnp.where(lane == INFO_DEST + 1, rank2, info)
        info = jnp.where(lane == INFO_GATE, gate1, info)
        info = jnp.where(lane == INFO_GATE + 1, gate2, info)
        info = jnp.where(lane == INFO_GATE + 2, i1, info)
        info = jnp.where(lane == INFO_GATE + 3, i2, info)
        stash[rows, :] = info

    @pl.when((phase == 1) & (i == 0))
    def _():
        lane8 = lax.broadcasted_iota(jnp.int32, run.shape, 1)
        counts = jnp.where((lane8 >= EXPERT_LANE0) & (lane8 < EXPERT_LANE0 + N_EXPERTS), run[...], 0.0)
        padded = jnp.floor((counts + (MOE_BLK - 1)) * (1.0 / MOE_BLK)) * MOE_BLK
        pend = padded
        shift = 1
        while shift < LANES:
            pend = pend + jnp.where(lane8 >= shift, pltpu.roll(pend, shift, 1), 0.0)
            shift *= 2
        pstart[...] = pend - padded
        blk_row = lax.broadcasted_iota(jnp.int32, meta_ref.shape, 0)
        lane_m = lax.broadcasted_iota(jnp.int32, meta_ref.shape, 1)
        is_e = (lane_m >= EXPERT_LANE0) & (lane_m < EXPERT_LANE0 + N_EXPERTS)
        ended = is_e & (pend[0:1, :] <= (blk_row * MOE_BLK).astype(F32))
        blk_e = jnp.minimum(jnp.sum(jnp.where(ended, 1.0, 0.0), axis=1, keepdims=True), N_EXPERTS - 1.0)
        n_used = jnp.max(pend[0:1, :], axis=1, keepdims=True) * (1.0 / MOE_BLK)
        meta_ref[...] = jnp.where(blk_row == n_blocks, n_used, blk_e) + jnp.zeros(meta_ref.shape, F32)

    @pl.when(phase == 1)
    def _():
        info = stash[rows, :]
        i1 = info[:, INFO_GATE + 2:INFO_GATE + 3]
        i2 = info[:, INFO_GATE + 3:INFO_GATE + 4]
        start = pstart[0:1, :]
        s1 = jnp.sum(jnp.where(lane_f == i1, start, 0.0), axis=1, keepdims=True)
        s2 = jnp.sum(jnp.where(lane_f == i2, start, 0.0), axis=1, keepdims=True)
        out = jnp.where(lane == INFO_DEST, info + s1, info)
        out = jnp.where(lane == INFO_DEST + 1, info + s2, out)
        info_ref[...] = out


def _route_call(logits, n_blocks):
    n = logits.shape[0]
    tm = min(ROUTE_TM, n)
    assert n_blocks < META_ROWS
    tri = jnp.tril(jnp.ones((tm, tm), F32), k=-1).astype(BF16)
    return pl.pallas_call(
        functools.partial(_route_kernel, n_blocks=n_blocks),
        grid=(2, n // tm),
        in_specs=[
            pl.BlockSpec((tm, LANES), lambda ph, i: (jnp.where(ph == 0, i, n // tm - 1), 0)),
            _resident(tri.shape),
        ],
        out_specs=[
            pl.BlockSpec((tm, LANES), lambda ph, i: (i * ph, 0)),
            pl.BlockSpec((META_ROWS, LANES), lambda ph, i: (0, 0)),
        ],
        out_shape=[jax.ShapeDtypeStruct((n, LANES), F32), jax.ShapeDtypeStruct((META_ROWS, LANES), F32)],
        scratch_shapes=[pltpu.VMEM((n, LANES), F32), pltpu.VMEM((8, LANES), F32), pltpu.VMEM((8, LANES), F32)],
        compiler_params=_params(("arbitrary", "arbitrary")),
        name="route",
    )(logits, tri)


def _route(logits):
    n = logits.shape[0]
    a = n * TOP_K
    n_blocks = (a + N_EXPERTS * (MOE_BLK - 1) + MOE_BLK - 1) // MOE_BLK
    info, meta = _route_call(logits, n_blocks)
    dest = info[:, INFO_DEST:INFO_DEST + TOP_K].astype(jnp.int32)
    blk_e = meta[:n_blocks, 0].astype(jnp.int32)
    n_used = meta[n_blocks:n_blocks + 1, 0].astype(jnp.int32)
    tok = jnp.arange(a, dtype=jnp.int32) // TOP_K
    buf_tok = jnp.zeros((n_blocks * MOE_BLK,), jnp.int32).at[dest.reshape(a)].set(tok)
    return blk_e, n_used, buf_tok, dest, info


def kernel(x, p, positions, w_in, w_s, b_s, ln_v_g, ln_v_b, w_a, w_b, w_o, ln1_g, ln1_b, w_grp, b_grp, w_rt, b_rt, w1, w3, w2, w_pg, w_pp, ln2_g, ln2_b):
    batch, seq, d = x.shape
    depth = w_in.shape[0]
    n = batch * seq
    assert d == D_MODEL and w_in.shape[2] == PROJ_WIDTH
    cosf, sinf = _rotary_tables(positions)
    causal = jnp.tril(jnp.ones((CHUNK, CHUNK), F32))
    xf = x.reshape(n, d)
    xb = xf.astype(BF16)
    pad_r = ROUTER_LANES - N_EXPERT_GROUPS - N_EXPERTS
    w_pa, w_pb = _split_proj_weights(w_in)

    for i in range(depth):
        proj_a = _proj_a_call(xb, w_pa[i])
        qkv = _proj_b_call(xb, w_pb[i], cosf, sinf)
        ws = (w_s[i] * causal).astype(BF16)
        bs = jnp.repeat(b_s[i].T, SGU_GROUP_CH, axis=1)
        act = _sgu_call(proj_a, ws, bs, ln_v_g[i][None, :], ln_v_b[i][None, :])
        att = [_attn_call(qkv, g, batch, seq) for g in range(N_DIL)]
        wr = jnp.concatenate([w_grp[i], w_rt[i], jnp.zeros((d, pad_r), F32)], axis=1)
        wr_hi = wr.astype(BF16)
        wr_lo = (wr - wr_hi.astype(F32)).astype(BF16)
        br = jnp.concatenate([b_grp[i], b_rt[i], jnp.zeros((pad_r,), F32)])[None, :]
        x1, x1g, logits = _mix_call(xf, act, [a[0] for a in att], [a[1] for a in att], proj_a,
                                w_a[i].astype(BF16), w_b[i].astype(BF16), w_o[i].astype(BF16),
                                ln1_g[i][None, :], ln1_b[i][None, :], wr_hi, wr_lo, br)
        blk_e, n_used, buf_tok, pos, info = _route(logits)
        yb = _expert_call(blk_e, n_used, buf_tok, x1g, w1, w3, w2, i)
        xf, xb = _final_call(pos, info, x1, p[i].reshape(n, PLE_DIM), yb, w_pg[i].astype(BF16), w_pp[i].astype(BF16),
                             ln2_g[i][None, :], ln2_b[i][None, :])
    return xf.reshape(batch, seq, d)
```

```python
import functools

import jax
import jax.numpy as jnp
import numpy as np
from jax import lax
from jax.experimental import pallas as pl
from jax.experimental.pallas import tpu as pltpu

F32 = jnp.float32
BF16 = jnp.bfloat16

D_MODEL = 2048
DEPTH_FOR_DEEPNORM = 4
SGU_WIDTH = 1024
SGU_GROUPS = 8
SGU_GROUP_CH = SGU_WIDTH // SGU_GROUPS
CHUNK = 128
HEAD_DIM = 128
HEADS_PER_GROUP = 4
DILATED_GROUPS = ((128, 1), (512, 4), (2048, 16))
N_DIL = len(DILATED_GROUPS)
ATT_HEADS = HEADS_PER_GROUP * N_DIL
ATT_WIDTH = ATT_HEADS * HEAD_DIM
QBLK = 128
ROT_DIM = HEAD_DIM // 4
ROT_HALF = ROT_DIM // 2
ROPE_THETA = 500000.0
PROJ_WIDTH = 2 * SGU_WIDTH + 3 * ATT_WIDTH + 2 * D_MODEL
N_EXPERT_GROUPS = 4
EXPERTS_PER_GROUP = 8
N_EXPERTS = N_EXPERT_GROUPS * EXPERTS_PER_GROUP
TOP_K = 2
EXPERT_FF = 512
MOE_BLK = 256
PLE_DIM = 256
ALPHA = (2 * DEPTH_FOR_DEEPNORM) ** 0.25
LN_EPS = 1e-5
ROUTER_LANES = 128

LANES = 128
U32 = jnp.uint32
SLABS = D_MODEL // (2 * LANES)
ROW_PITCH = SLABS + 1
HI_MASK = np.uint32(0xFFFF0000)

PROJ_TM = 1024
PROJ_A_TN = 1024
PROJ_B_TN = ATT_WIDTH
PROJ_CHUNK = 256
SGU_TM = 512
MIX_TM = 256
FINAL_TM = 256
SUB_ROWS = 128
GATHER_AHEAD = 2
GATHER_SLOTS = GATHER_AHEAD + 1
ROW_PRIORITY = (0,)
WEIGHT_PRIORITY = 1
CAST_ROWS = 128
EXPERT_CHUNK = 256
EXPERT_OUT_CHUNK = 512
FINAL_CHUNK = 512
ATTN_STEP = {1: (8, 4), 4: (2, 4), 16: (1, 2)}
ATTN_CLASS_UNROLL = 2
VMEM_LIMIT = 56 * 1024 * 1024

_REF_SPLITS = (0, SGU_WIDTH, 2 * SGU_WIDTH, 2 * SGU_WIDTH + ATT_WIDTH, 2 * SGU_WIDTH + 2 * ATT_WIDTH,
               2 * SGU_WIDTH + 3 * ATT_WIDTH, 2 * SGU_WIDTH + 3 * ATT_WIDTH + D_MODEL, PROJ_WIDTH)
_A_GB = D_MODEL
_A_U = 2 * D_MODEL
_A_V = 2 * D_MODEL + SGU_WIDTH
PROJ_A_WIDTH = 2 * D_MODEL + 2 * SGU_WIDTH


def _params(sem, vmem=VMEM_LIMIT):
    return pltpu.CompilerParams(dimension_semantics=sem, vmem_limit_bytes=vmem)


def _resident(shape):
    nd = len(shape)
    return pl.BlockSpec(shape, lambda *_: (0,) * nd, pipeline_mode=pl.Buffered(1))


def _layer_norm(y, g, b):
    mu = jnp.mean(y, axis=-1, keepdims=True)
    yc = y - mu
    var = jnp.mean(yc * yc, axis=-1, keepdims=True)
    return yc * lax.rsqrt(var + LN_EPS) * g + b


def _split_proj_weights(w):
    sec = [w[:, :, _REF_SPLITS[k]:_REF_SPLITS[k + 1]] for k in range(7)]
    wa = jnp.concatenate([sec[5], sec[6], sec[0], sec[1]], axis=2).astype(BF16)
    wb = w[:, :, _REF_SPLITS[2]:_REF_SPLITS[5]].astype(BF16)
    return wa, wb


def _rotary_tables(positions):
    inv_freq = ROPE_THETA ** (-jnp.arange(0, ROT_DIM, 2, dtype=F32) / ROT_DIM)
    ang = positions.astype(F32).reshape(-1)[:, None] * inv_freq
    cos, sin = jnp.cos(ang), jnp.sin(ang)
    n = ang.shape[0]
    cosf = jnp.concatenate([cos, cos, jnp.ones((n, HEAD_DIM - ROT_DIM), F32)], axis=1)
    sinf = jnp.concatenate([-sin, sin, jnp.zeros((n, HEAD_DIM - ROT_DIM), F32)], axis=1)
    return cosf, sinf


def _proj_a_kernel(x_ref, w_ref, o_ref):
    j = pl.program_id(1)
    tn = o_ref.shape[1]

    def run(act):
        for c in range(tn // PROJ_CHUNK):
            cols = slice(c * PROJ_CHUNK, (c + 1) * PROJ_CHUNK)
            acc = jnp.dot(x_ref[...], w_ref[:, cols], preferred_element_type=F32)
            o_ref[:, cols] = act(acc).astype(o_ref.dtype)

    @pl.when(j < _A_U // tn)
    def _():
        run(jax.nn.sigmoid)

    @pl.when(j >= _A_U // tn)
    def _():
        run(jax.nn.gelu)


def _proj_a_call(xb, w):
    n, d = xb.shape
    tm = min(PROJ_TM, n)
    tn = PROJ_A_TN
    assert _A_U % tn == 0 and PROJ_A_WIDTH % tn == 0
    return pl.pallas_call(
        _proj_a_kernel,
        grid=(n // tm, PROJ_A_WIDTH // tn),
        in_specs=[
            pl.BlockSpec((tm, d), lambda i, j: (i, 0)),
            pl.BlockSpec((d, tn), lambda i, j: (0, j)),
        ],
        out_specs=pl.BlockSpec((tm, tn), lambda i, j: (i, j)),
        out_shape=jax.ShapeDtypeStruct((n, PROJ_A_WIDTH), BF16),
        compiler_params=_params(("parallel", "arbitrary")),
        name="proj_a",
    )(xb, w)


def _proj_b_kernel(x_ref, w_ref, cos_ref, sin_ref, o_ref):
    j = pl.program_id(1)
    heads_per_chunk = PROJ_CHUNK // HEAD_DIM
    lane = lax.broadcasted_iota(jnp.int32, cos_ref.shape, 1)

    def run(rotate):
        for c in range(PROJ_B_TN // PROJ_CHUNK):
            cols = slice(c * PROJ_CHUNK, (c + 1) * PROJ_CHUNK)
            acc = jnp.dot(x_ref[...], w_ref[:, cols], preferred_element_type=F32)
            for h in range(heads_per_chunk):
                xh = acc[:, h * HEAD_DIM:(h + 1) * HEAD_DIM]
                if rotate:
                    partner = jnp.where(lane < ROT_HALF, pltpu.roll(xh, HEAD_DIM - ROT_HALF, 1),
                                        pltpu.roll(xh, ROT_HALF, 1))
                    xh = xh * cos_ref[...] + partner * sin_ref[...]
                o_ref[c * heads_per_chunk + h] = xh

    @pl.when(j < 2)
    def _():
        run(True)

    @pl.when(j >= 2)
    def _():
        run(False)


def _proj_b_call(xb, w, cosf, sinf):
    n, d = xb.shape
    tm = min(PROJ_TM, n)
    slabs = PROJ_B_TN // HEAD_DIM
    return pl.pallas_call(
        _proj_b_kernel,
        grid=(n // tm, 3),
        in_specs=[
            pl.BlockSpec((tm, d), lambda i, j: (i, 0)),
            pl.BlockSpec((d, PROJ_B_TN), lambda i, j: (0, j)),
            pl.BlockSpec((tm, HEAD_DIM), lambda i, j: (i, 0)),
            pl.BlockSpec((tm, HEAD_DIM), lambda i, j: (i, 0)),
        ],
        out_specs=pl.BlockSpec((slabs, tm, HEAD_DIM), lambda i, j: (j, i, 0)),
        out_shape=jax.ShapeDtypeStruct((3 * slabs, n, HEAD_DIM), F32),
        compiler_params=_params(("parallel", "arbitrary")),
        name="proj_b",
    )(xb, w, cosf, sinf)


def _sgu_kernel(u_ref, v_ref, ws_ref, bs_ref, g_ref, b_ref, o_ref):
    vn = _layer_norm(v_ref[...].astype(F32), g_ref[...], b_ref[...]).astype(BF16)
    tm = o_ref.shape[0]
    for c in range(tm // CHUNK):
        rows = slice(c * CHUNK, (c + 1) * CHUNK)
        for g in range(SGU_GROUPS):
            cols = slice(g * SGU_GROUP_CH, (g + 1) * SGU_GROUP_CH)
            z = jnp.dot(ws_ref[g], vn[rows, cols], preferred_element_type=F32) + bs_ref[:, cols]
            o_ref[rows, cols] = (u_ref[rows, cols].astype(F32) * z).astype(o_ref.dtype)


def _sgu_call(proj_a, ws, bs, ln_g, ln_b):
    n = proj_a.shape[0]
    tm = min(SGU_TM, n)
    u_blk, v_blk = _A_U // SGU_WIDTH, _A_V // SGU_WIDTH
    return pl.pallas_call(
        _sgu_kernel,
        grid=(n // tm,),
        in_specs=[
            pl.BlockSpec((tm, SGU_WIDTH), lambda i: (i, u_blk)),
            pl.BlockSpec((tm, SGU_WIDTH), lambda i: (i, v_blk)),
            _resident(ws.shape),
            _resident(bs.shape),
            _resident(ln_g.shape),
            _resident(ln_b.shape),
        ],
        out_specs=pl.BlockSpec((tm, SGU_WIDTH), lambda i: (i, 0)),
        out_shape=jax.ShapeDtypeStruct((n, SGU_WIDTH), BF16),
        compiler_params=_params(("parallel",)),
        name="sgu",
    )(proj_a, proj_a, ws, bs, ln_g, ln_b)


def _attn_kernel(q_ref, kp_ref, k_ref, vp_ref, v_ref, o_ref, lse_ref, *, dil, span, nq, hps):
    slab_idx = pl.program_id(1)
    hc = pl.program_id(2)
    qi = lax.broadcasted_iota(jnp.int32, (QBLK, QBLK), 0)
    kj = lax.broadcasted_iota(jnp.int32, (QBLK, QBLK), 1)
    dist_prev = qi + QBLK - kj
    dist_cur = qi - kj
    in_prev = (dist_prev >= 0) & (dist_prev <= span)
    mask_cur = (dist_cur >= 0) & (dist_cur <= span)
    scale = HEAD_DIM ** -0.5
    nt = (((1,), (1,)), ((), ()))
    lane = lax.broadcasted_iota(jnp.int32, (QBLK, LANES), 1)
    ones = jnp.ones((QBLK, HEAD_DIM), BF16)

    @pl.when(hc == 0)
    def _():
        lse_ref[...] = jnp.zeros_like(lse_ref)

    def rows(r, blk):
        return pl.ds(blk * QBLK * dil + r, QBLK, stride=dil) if dil > 1 else pl.ds(blk * QBLK, QBLK)

    def one_class(r):
        for blk in range(nq):
            if blk == 0:
                mask_prev = in_prev & (slab_idx > 0)
            else:
                mask_prev = in_prev
            sel = rows(r, blk)
            lse_rows = lse_ref[sel, :]
            for h in range(hps):
                q = (q_ref.at[h][sel, :] * scale).astype(BF16)
                if blk == 0:
                    kp = kp_ref.at[h][rows(r, 0), :].astype(BF16)
                    vp = vp_ref.at[h][rows(r, 0), :].astype(BF16)
                else:
                    kp = k_ref.at[h][rows(r, blk - 1), :].astype(BF16)
                    vp = v_ref.at[h][rows(r, blk - 1), :].astype(BF16)
                kc = k_ref.at[h][sel, :].astype(BF16)
                vc = v_ref.at[h][sel, :].astype(BF16)
                sp = jnp.where(mask_prev, lax.dot_general(q, kp, nt, preferred_element_type=F32), -jnp.inf)
                sc = jnp.where(mask_cur, lax.dot_general(q, kc, nt, preferred_element_type=F32), -jnp.inf)
                m = jnp.max(jnp.maximum(sp, sc), axis=-1, keepdims=True)
                ep = jnp.exp((sp - m).astype(BF16))
                ec = jnp.exp((sc - m).astype(BF16))
                od = jnp.dot(ep, jnp.concatenate([vp, ones], axis=1), preferred_element_type=F32)
                od = od + jnp.dot(ec, jnp.concatenate([vc, ones], axis=1), preferred_element_type=F32)
                den = od[:, HEAD_DIM:]
                o_ref.at[h][sel, :] = od[:, :HEAD_DIM] / den
                lse_rows = jnp.where(lane == hc * hps + h, m + jnp.log(den), lse_rows)
            lse_ref[sel, :] = lse_rows

    if dil == 1:
        one_class(0)
    else:
        def body(r2, carry):
            for u in range(ATTN_CLASS_UNROLL):
                one_class(r2 * ATTN_CLASS_UNROLL + u)
            return carry
        lax.fori_loop(0, dil // ATTN_CLASS_UNROLL, body, 0)


def _attn_call(qkv, group, batch, seq):
    window, dil = DILATED_GROUPS[group]
    span = window // dil
    nq, hps = ATTN_STEP[dil]
    slab = nq * QBLK * dil
    prev = QBLK * dil
    assert seq % slab == 0 and span <= QBLK and HEADS_PER_GROUP % hps == 0
    slabs_per_seq = seq // slab
    prev_per_seq = seq // prev
    n = batch * seq
    hblocks = HEADS_PER_GROUP // hps

    def cur(section):
        base = (section * ATT_HEADS + group * HEADS_PER_GROUP) // hps
        return pl.BlockSpec((hps, slab, LANES), lambda b, s, hc: (base + hc, b * slabs_per_seq + s, 0))

    def before(section):
        base = (section * ATT_HEADS + group * HEADS_PER_GROUP) // hps
        return pl.BlockSpec(
            (hps, prev, LANES),
            lambda b, s, hc: (base + hc, b * prev_per_seq + jnp.maximum(s * nq - 1, 0), 0))

    return pl.pallas_call(
        functools.partial(_attn_kernel, dil=dil, span=span, nq=nq, hps=hps),
        grid=(batch, slabs_per_seq, hblocks),
        in_specs=[cur(0), before(1), cur(1), before(2), cur(2)],
        out_specs=[
            pl.BlockSpec((hps, slab, LANES), lambda b, s, hc: (hc, b * slabs_per_seq + s, 0)),
            pl.BlockSpec((slab, LANES), lambda b, s, hc: (b * slabs_per_seq + s, 0)),
        ],
        out_shape=[
            jax.ShapeDtypeStruct((HEADS_PER_GROUP, n, LANES), F32),
            jax.ShapeDtypeStruct((n, LANES), F32),
        ],
        compiler_params=_params(("parallel", "arbitrary", "arbitrary")),
        name=f"attn_d{dil}",
    )(qkv, qkv, qkv, qkv, qkv)


def _mix_kernel(x_ref, act_ref, o0_ref, o1_ref, o2_ref, l0_ref, l1_ref, l2_ref, sga_ref, sgb_ref,
                wa_ref, wb_ref, wo_ref, g_ref, b_ref, wrh_ref, wrl_ref, br_ref, x1_ref, x1g_ref, logit_ref):
    tm = x_ref.shape[0]
    o_refs = (o0_ref, o1_ref, o2_ref)
    l_refs = (l0_ref, l1_ref, l2_ref)
    for sub in range(tm // SUB_ROWS):
        rows = pl.ds(sub * SUB_ROWS, SUB_ROWS)
        a_out = jnp.dot(act_ref[rows, :], wa_ref[...], preferred_element_type=F32)
        lses = [l[rows, :] for l in l_refs]
        heads = []
        for h in range(HEADS_PER_GROUP):
            lh = [l[:, h:h + 1] for l in lses]
            m = jnp.maximum(jnp.maximum(lh[0], lh[1]), lh[2])
            e = [jnp.exp(v - m) for v in lh]
            tot = e[0] + e[1] + e[2]
            acc = (e[0] / tot) * o_refs[0][h, rows, :]
            acc = acc + (e[1] / tot) * o_refs[1][h, rows, :]
            acc = acc + (e[2] / tot) * o_refs[2][h, rows, :]
            heads.append(acc.astype(BF16))
        merged = jnp.concatenate(heads, axis=1)
        b_out = jnp.dot(merged, wb_ref[...], preferred_element_type=F32)
        mixed_in = sga_ref[rows, :].astype(F32) * a_out + sgb_ref[rows, :].astype(F32) * b_out
        mixed = jnp.dot(mixed_in.astype(BF16), wo_ref[...], preferred_element_type=F32)
        x1 = _layer_norm(ALPHA * x_ref[rows, :] + mixed, g_ref[...], b_ref[...])
        x1_ref[rows, :] = x1
        _store_gather_rows(x1g_ref, sub * SUB_ROWS, x1)
        x_hi = x1.astype(BF16)
        x_lo = (x1 - x_hi.astype(F32)).astype(BF16)
        logits = jnp.dot(x_hi, wrh_ref[...], preferred_element_type=F32)
        logits = logits + jnp.dot(x_lo, wrh_ref[...], preferred_element_type=F32)
        logits = logits + jnp.dot(x_hi, wrl_ref[...], preferred_element_type=F32)
        logit_ref[rows, :] = logits + br_ref[...]


def _mix_call(x, act, os_, lses, proj_a, wa, wb, wo, g, b, wrh, wrl, br):
    n = x.shape[0]
    tm = min(MIX_TM, n)
    row = lambda w: pl.BlockSpec((tm, w), lambda i: (i, 0))
    o_spec = pl.BlockSpec((HEADS_PER_GROUP, tm, LANES), lambda i: (0, i, 0))
    return pl.pallas_call(
        _mix_kernel,
        grid=(n // tm,),
        in_specs=[
            row(D_MODEL), row(SGU_WIDTH),
            o_spec, o_spec, o_spec,
            row(LANES), row(LANES), row(LANES),
            pl.BlockSpec((tm, D_MODEL), lambda i: (i, 0)),
            pl.BlockSpec((tm, D_MODEL), lambda i: (i, _A_GB // D_MODEL)),
            _resident(wa.shape), _resident(wb.shape), _resident(wo.shape),
            _resident(g.shape), _resident(b.shape),
            _resident(wrh.shape), _resident(wrl.shape), _resident(br.shape),
        ],
        out_specs=[row(D_MODEL), pl.BlockSpec((tm * ROW_PITCH, LANES), lambda i: (i, 0)), row(ROUTER_LANES)],
        out_shape=[jax.ShapeDtypeStruct((n, D_MODEL), F32), jax.ShapeDtypeStruct((n * ROW_PITCH, LANES), U32),
                   jax.ShapeDtypeStruct((n, ROUTER_LANES), F32)],
        compiler_params=_params(("parallel",)),
        name="mix",
    )(x, act, *os_, *lses, proj_a, proj_a, wa, wb, wo, g, b, wrh, wrl, br)


def _start_row(idx_ref, r, src_hbm, dst, sem, priority):
    src = src_hbm.at[pl.ds(idx_ref[0, r] * ROW_PITCH, SLABS)]
    pltpu.make_async_copy(src, dst.at[pl.ds(r * ROW_PITCH, SLABS)], sem).start(priority=priority)


def _start_row_gather(idx_ref, n_rows, src_hbm, dst, sem, priorities=(0, 1)):
    unroll = 8
    def body(blk, carry):
        for j in range(unroll):
            _start_row(idx_ref, blk * unroll + j, src_hbm, dst, sem, priorities[j % len(priorities)])
        return carry
    lax.fori_loop(0, n_rows // unroll, body, 0)


ANCHOR_ROWS = 8


def _interleaved_row_gather(idx_ref, n_rows, src_hbm, dst, sem, n_parts, priorities=(0, 1)):
    per = n_rows // n_parts
    def issue(part, after=None):
        if after is not None:
            dst[pl.ds(n_rows * ROW_PITCH, ANCHOR_ROWS), :] = pltpu.bitcast(after[0:ANCHOR_ROWS, 0:LANES], U32)
        for r in range(part * per, (part + 1) * per):
            _start_row(idx_ref, r, src_hbm, dst, sem, priorities[r % len(priorities)])
    return issue


def _wait_row_gather(n_rows, src_hbm, dst, sem):
    pltpu.make_async_copy(src_hbm.at[pl.ds(0, n_rows * SLABS)], dst.at[pl.ds(0, n_rows * SLABS)], sem).wait()


def _gathered_rows(buf, first_row, n_rows):
    lo, hi = [], []
    for k in range(SLABS):
        w = buf[pl.ds(first_row * ROW_PITCH + k, n_rows, stride=ROW_PITCH), :]
        lo.append(pltpu.bitcast(w << 16, F32))
        hi.append(pltpu.bitcast(w & HI_MASK, F32))
    return jnp.concatenate(lo + hi, axis=1)


def _pack_words(val, k):
    half = D_MODEL // 2
    lo = val[:, k * LANES:(k + 1) * LANES].astype(BF16).astype(F32)
    hi = val[:, half + k * LANES:half + (k + 1) * LANES].astype(BF16).astype(F32)
    return (pltpu.bitcast(lo, U32) >> 16) | pltpu.bitcast(hi, U32)


def _store_gather_rows(ref, first_row, val):
    n_rows = val.shape[0]
    for k in range(SLABS):
        ref[pl.ds(first_row * ROW_PITCH + k, n_rows, stride=ROW_PITCH), :] = _pack_words(val, k)
    ref[pl.ds(first_row * ROW_PITCH + SLABS, n_rows, stride=ROW_PITCH), :] = jnp.zeros((n_rows, LANES), U32)


def _expert_kernel(blk_e_ref, n_used_ref, run_start_ref, next_e_ref, has_next_ref,
                   tok0_ref, tok1_ref, tok2_ref, x_hbm, w1_hbm, w3_hbm, w2_hbm,
                   o_ref, xbuf, w1s, w3s, w2s, w1b, w3b, w2b, sem, wsem, *, layer):
    i = pl.program_id(0)
    n_used = n_used_ref[0]
    slot = i % GATHER_SLOTS
    ahead_slot = (i + GATHER_AHEAD) % GATHER_SLOTS

    def weight_copies(e):
        return (pltpu.make_async_copy(w1_hbm.at[layer, e], w1s, wsem.at[0]),
                pltpu.make_async_copy(w3_hbm.at[layer, e], w3s, wsem.at[1]),
                pltpu.make_async_copy(w2_hbm.at[layer, e], w2s, wsem.at[2]))

    @pl.when((i == 0) & (n_used > 0))
    def _():
        _start_row_gather(tok0_ref, MOE_BLK, x_hbm, xbuf.at[0], sem.at[0], ROW_PRIORITY)
        _start_row_gather(tok1_ref, MOE_BLK, x_hbm, xbuf.at[1], sem.at[1], ROW_PRIORITY)
        for c in weight_copies(blk_e_ref[0]):
            c.start(priority=WEIGHT_PRIORITY)

    @pl.when((i < n_used) & (run_start_ref[i] == 1))
    def _():
        for c in weight_copies(blk_e_ref[i]):
            c.wait()
        for src, dst in ((w1s, w1b), (w3s, w3b), (w2s, w2b)):
            for r0 in range(0, src.shape[0], CAST_ROWS):
                dst[pl.ds(r0, CAST_ROWS), :] = src[pl.ds(r0, CAST_ROWS), :].astype(BF16)

        @pl.when(has_next_ref[i] == 1)
        def _():
            for c in weight_copies(next_e_ref[i]):
                c.start(priority=WEIGHT_PRIORITY)

    @pl.when(i < n_used)
    def _():
        _wait_row_gather(MOE_BLK, x_hbm, xbuf.at[slot], sem.at[slot])
        n_parts = 2 * (EXPERT_FF // EXPERT_CHUNK) + D_MODEL // EXPERT_OUT_CHUNK
        issue = _interleaved_row_gather(tok2_ref, MOE_BLK, x_hbm, xbuf.at[ahead_slot], sem.at[ahead_slot], n_parts,
                                        ROW_PRIORITY)
        part = 0
        last = None
        xb = _gathered_rows(xbuf.at[slot], 0, MOE_BLK).astype(BF16)
        hid = []
        for c in range(EXPERT_FF // EXPERT_CHUNK):
            cols = slice(c * EXPERT_CHUNK, (c + 1) * EXPERT_CHUNK)
            issue(part, last)
            h1 = jnp.dot(xb, w1b[:, cols], preferred_element_type=F32)
            issue(part + 1, h1)
            h3 = jnp.dot(xb, w3b[:, cols], preferred_element_type=F32)
            part += 2
            last = h3
            hid.append((jax.nn.silu(h1) * h3).astype(BF16))
        hid = jnp.concatenate(hid, axis=1)
        ys = []
        for c in range(D_MODEL // EXPERT_OUT_CHUNK):
            cols = slice(c * EXPERT_OUT_CHUNK, (c + 1) * EXPERT_OUT_CHUNK)
            issue(part, last)
            part += 1
            last = jnp.dot(hid, w2b[:, cols], preferred_element_type=F32)
            ys.append(last)
        _store_gather_rows(o_ref, 0, jnp.concatenate(ys, axis=1))

    @pl.when(i == n_used - 1)
    def _():
        for k in range(1, GATHER_SLOTS):
            other = (i + k) % GATHER_SLOTS
            _wait_row_gather(MOE_BLK, x_hbm, xbuf.at[other], sem.at[other])

    @pl.when(i >= n_used)
    def _():
        o_ref[...] = jnp.zeros_like(o_ref)


def _expert_plan(blk_e, n_used):
    n_blocks = blk_e.shape[0]
    j = jnp.arange(n_blocks, dtype=jnp.int32)
    run_start = ((j == 0) | (blk_e != jnp.roll(blk_e, 1))) & (j < n_used[0])
    start_idx = jnp.where(run_start, j, n_blocks)
    at_or_after = lax.cummin(start_idx, axis=0, reverse=True)
    nxt = jnp.concatenate([at_or_after[1:], jnp.full((1,), n_blocks, jnp.int32)])
    has_next = nxt < n_blocks
    next_e = blk_e[jnp.minimum(nxt, n_blocks - 1)]
    return run_start.astype(jnp.int32), next_e.astype(jnp.int32), has_next.astype(jnp.int32)


def _expert_call(blk_e, n_used, buf_tok, x1g, w1, w3, w2, layer):
    n_blocks = blk_e.shape[0]
    d = D_MODEL
    tok3 = buf_tok.reshape(n_blocks, 1, MOE_BLK)
    run_start, next_e, has_next = _expert_plan(blk_e, n_used)

    def tok_spec(k):
        return pl.BlockSpec((None, 1, MOE_BLK),
                            lambda i, e, u, *_: (jnp.minimum(i + k, jnp.maximum(u[0] - 1, 0)), 0, 0),
                            memory_space=pltpu.SMEM)

    assert GATHER_AHEAD == 2
    grid_spec = pltpu.PrefetchScalarGridSpec(
        num_scalar_prefetch=5,
        grid=(n_blocks,),
        in_specs=[
            tok_spec(0), tok_spec(1), tok_spec(2),
            pl.BlockSpec(memory_space=pl.ANY),
            pl.BlockSpec(memory_space=pl.ANY),
            pl.BlockSpec(memory_space=pl.ANY),
            pl.BlockSpec(memory_space=pl.ANY),
        ],
        out_specs=pl.BlockSpec((MOE_BLK * ROW_PITCH, LANES), lambda i, e, u, *_: (i, 0)),
        scratch_shapes=[pltpu.VMEM((GATHER_SLOTS, MOE_BLK * ROW_PITCH + ANCHOR_ROWS, LANES), U32),
                        pltpu.VMEM((d, EXPERT_FF), F32), pltpu.VMEM((d, EXPERT_FF), F32),
                        pltpu.VMEM((EXPERT_FF, d), F32),
                        pltpu.VMEM((d, EXPERT_FF), BF16), pltpu.VMEM((d, EXPERT_FF), BF16),
                        pltpu.VMEM((EXPERT_FF, d), BF16),
                        pltpu.SemaphoreType.DMA((GATHER_SLOTS,)), pltpu.SemaphoreType.DMA((3,))],
    )
    return pl.pallas_call(
        functools.partial(_expert_kernel, layer=layer),
        grid_spec=grid_spec,
        out_shape=jax.ShapeDtypeStruct((n_blocks * MOE_BLK * ROW_PITCH, LANES), U32),
        compiler_params=_params(("arbitrary",)),
        name="experts",
    )(blk_e, n_used, run_start, next_e, has_next, tok3, tok3, tok3, x1g, w1, w3, w2)


def _final_kernel(pos0_ref, pos1_ref, pos2_ref, x1_ref, p_ref, info_ref, yb_hbm, wpg_ref, wpp_ref, g_ref, b_ref,
                  x2_ref, x2b_ref, ybuf, sem):
    i = pl.program_id(0)
    n_steps = pl.num_programs(0)
    slot = i % GATHER_SLOTS
    ahead_slot = (i + GATHER_AHEAD) % GATHER_SLOTS
    tm = p_ref.shape[0]

    @pl.when(i == 0)
    def _():
        _start_row_gather(pos0_ref, TOP_K * tm, yb_hbm, ybuf.at[0], sem.at[0])
        _start_row_gather(pos1_ref, TOP_K * tm, yb_hbm, ybuf.at[1], sem.at[1])

    _wait_row_gather(TOP_K * tm, yb_hbm, ybuf.at[slot], sem.at[slot])
    n_sub = tm // SUB_ROWS
    n_chunks = D_MODEL // FINAL_CHUNK
    issue = _interleaved_row_gather(pos2_ref, TOP_K * tm, yb_hbm, ybuf.at[ahead_slot], sem.at[ahead_slot],
                                    n_sub * n_chunks)
    last = None
    for sub in range(n_sub):
        rows = pl.ds(sub * SUB_ROWS, SUB_ROWS)
        x1 = x1_ref[rows, :]
        x1b = x1.astype(BF16)
        pb = p_ref[rows, :].astype(BF16)
        pre = []
        for c in range(n_chunks):
            cols = slice(c * FINAL_CHUNK, (c + 1) * FINAL_CHUNK)
            issue(sub * n_chunks + c, last)
            gate = jax.nn.sigmoid(jnp.dot(x1b, wpg_ref[:, cols], preferred_element_type=F32))
            last = gate * jnp.dot(pb, wpp_ref[:, cols], preferred_element_type=F32)
            pre.append(last)
        pre = jnp.concatenate(pre, axis=1)
        info = info_ref[rows, :]
        y = (_gathered_rows(ybuf.at[slot], sub * SUB_ROWS, SUB_ROWS) * info[:, INFO_GATE:INFO_GATE + 1]
             + _gathered_rows(ybuf.at[slot], tm + sub * SUB_ROWS, SUB_ROWS) * info[:, INFO_GATE + 1:INFO_GATE + 2])
        x2 = _layer_norm(ALPHA * x1 + y + pre, g_ref[...], b_ref[...])
        x2_ref[rows, :] = x2
        x2b_ref[rows, :] = x2.astype(BF16)

    @pl.when(i == n_steps - 1)
    def _():
        for k in range(1, GATHER_SLOTS):
            other = (i + k) % GATHER_SLOTS
            _wait_row_gather(TOP_K * tm, yb_hbm, ybuf.at[other], sem.at[other])


def _final_call(pos, info, x1, p, yb, wpg, wpp, g, b):
    n = p.shape[0]
    d = D_MODEL
    tm = min(FINAL_TM, n)
    steps = n // tm
    pos3 = pos.reshape(steps, tm, TOP_K).transpose(0, 2, 1).reshape(steps, 1, TOP_K * tm)
    row = lambda w: pl.BlockSpec((tm, w), lambda i: (i, 0))

    def pos_spec(k):
        return pl.BlockSpec((None, 1, TOP_K * tm), lambda i: (jnp.minimum(i + k, steps - 1), 0, 0),
                            memory_space=pltpu.SMEM)

    assert GATHER_AHEAD == 2
    return pl.pallas_call(
        _final_kernel,
        grid=(steps,),
        in_specs=[
            pos_spec(0), pos_spec(1), pos_spec(2),
            row(d),
            row(PLE_DIM),
            row(LANES),
            pl.BlockSpec(memory_space=pl.ANY),
            _resident(wpg.shape), _resident(wpp.shape), _resident(g.shape), _resident(b.shape),
        ],
        out_specs=[row(d), row(d)],
        out_shape=[jax.ShapeDtypeStruct((n, d), F32), jax.ShapeDtypeStruct((n, d), BF16)],
        scratch_shapes=[pltpu.VMEM((GATHER_SLOTS, TOP_K * tm * ROW_PITCH + ANCHOR_ROWS, LANES), U32),
                        pltpu.SemaphoreType.DMA((GATHER_SLOTS,))],
        compiler_params=_params(("arbitrary",)),
        name="final",
    )(pos3, pos3, pos3, x1, p, info, yb, wpg, wpp, g, b)


INFO_DEST = 0
INFO_GATE = 2
EXPERT_LANE0 = N_EXPERT_GROUPS
ROUTE_TM = 512
META_ROWS = 256


def _lane_min_index(mask, lane_f):
    return jnp.min(jnp.where(mask, lane_f, float(LANES)), axis=1, keepdims=True)


def _route_kernel(logit_ref, tri_ref, info_ref, meta_ref, stash, run, pstart, *, n_blocks):
    phase = pl.program_id(0)
    i = pl.program_id(1)
    tm = logit_ref.shape[0]
    lane = lax.broadcasted_iota(jnp.int32, (tm, LANES), 1)
    lane_f = lane.astype(F32)
    rows = pl.ds(pl.multiple_of(i * tm, tm), tm)

    @pl.when((phase == 0) & (i == 0))
    def _():
        run[...] = jnp.zeros_like(run)

    @pl.when(phase == 0)
    def _():
        x = logit_ref[...]
        xg = jnp.where(lane < N_EXPERT_GROUPS, x, -jnp.inf)
        mg = jnp.max(xg, axis=1, keepdims=True)
        g_sel = _lane_min_index(xg == mg, lane_f)
        p_g = 1.0 / jnp.sum(jnp.exp(xg - mg), axis=1, keepdims=True)
        lo = EXPERT_LANE0 + EXPERTS_PER_GROUP * g_sel
        xe = jnp.where((lane_f >= lo) & (lane_f < lo + EXPERTS_PER_GROUP), x, -jnp.inf)
        v1 = jnp.max(xe, axis=1, keepdims=True)
        i1 = _lane_min_index(xe == v1, lane_f)
        xe2 = jnp.where(lane_f == i1, -jnp.inf, xe)
        v2 = jnp.max(xe2, axis=1, keepdims=True)
        i2 = _lane_min_index(xe2 == v2, lane_f)
        e21 = jnp.exp(v2 - v1)
        den = 1.0 + e21
        gate1 = (1.0 / den) * p_g
        gate2 = (e21 / den) * p_g
        o1 = lane_f == i1
        o2 = lane_f == i2
        both = jnp.where(o1 | o2, 1.0, 0.0)
        before = jnp.dot(tri_ref[...], both.astype(BF16), preferred_element_type=F32) + run[0:1, :]
        rank1 = jnp.sum(jnp.where(o1, before, 0.0), axis=1, keepdims=True)
        rank2 = jnp.sum(jnp.where(o2, before, 0.0), axis=1, keepdims=True)
        run[0:1, :] = run[0:1, :] + jnp.sum(both, axis=0, keepdims=True)
        info = jnp.where(lane == INFO_DEST, rank1, 0.0)
        info = jnp.where(lane == INFO_DEST + 1, rank2, info)
        info = jnp.where(lane == INFO_GATE, gate1, info)
        info = jnp.where(lane == INFO_GATE + 1, gate2, info)
        info = jnp.where(lane == INFO_GATE + 2, i1, info)
        info = jnp.where(lane == INFO_GATE + 3, i2, info)
        stash[rows, :] = info

    @pl.when((phase == 1) & (i == 0))
    def _():
        lane8 = lax.broadcasted_iota(jnp.int32, run.shape, 1)
        counts = jnp.where((lane8 >= EXPERT_LANE0) & (lane8 < EXPERT_LANE0 + N_EXPERTS), run[...], 0.0)
        padded = jnp.floor((counts + (MOE_BLK - 1)) * (1.0 / MOE_BLK)) * MOE_BLK
        pend = padded
        shift = 1
        while shift < LANES:
            pend = pend + jnp.where(lane8 >= shift, pltpu.roll(pend, shift, 1), 0.0)
            shift *= 2
        pstart[...] = pend - padded
        blk_row = lax.broadcasted_iota(jnp.int32, meta_ref.shape, 0)
        lane_m = lax.broadcasted_iota(jnp.int32, meta_ref.shape, 1)
        is_e = (lane_m >= EXPERT_LANE0) & (lane_m < EXPERT_LANE0 + N_EXPERTS)
        ended = is_e & (pend[0:1, :] <= (blk_row * MOE_BLK).astype(F32))
        blk_e = jnp.minimum(jnp.sum(jnp.where(ended, 1.0, 0.0), axis=1, keepdims=True), N_EXPERTS - 1.0)
        n_used = jnp.max(pend[0:1, :], axis=1, keepdims=True) * (1.0 / MOE_BLK)
        meta_ref[...] = jnp.where(blk_row == n_blocks, n_used, blk_e) + jnp.zeros(meta_ref.shape, F32)

    @pl.when(phase == 1)
    def _():
        info = stash[rows, :]
        i1 = info[:, INFO_GATE + 2:INFO_GATE + 3]
        i2 = info[:, INFO_GATE + 3:INFO_GATE + 4]
        start = pstart[0:1, :]
        s1 = jnp.sum(jnp.where(lane_f == i1, start, 0.0), axis=1, keepdims=True)
        s2 = jnp.sum(jnp.where(lane_f == i2, start, 0.0), axis=1, keepdims=True)
        out = jnp.where(lane == INFO_DEST, info + s1, info)
        out = jnp.where(lane == INFO_DEST + 1, info + s2, out)
        info_ref[...] = out


def _route_call(logits, n_blocks):
    n = logits.shape[0]
    tm = min(ROUTE_TM, n)
    assert n_blocks < META_ROWS
    tri = jnp.tril(jnp.ones((tm, tm), F32), k=-1).astype(BF16)
    return pl.pallas_call(
        functools.partial(_route_kernel, n_blocks=n_blocks),
        grid=(2, n // tm),
        in_specs=[
            pl.BlockSpec((tm, LANES), lambda ph, i: (jnp.where(ph == 0, i, n // tm - 1), 0)),
            _resident(tri.shape),
        ],
        out_specs=[
            pl.BlockSpec((tm, LANES), lambda ph, i: (i * ph, 0)),
            pl.BlockSpec((META_ROWS, LANES), lambda ph, i: (0, 0)),
        ],
        out_shape=[jax.ShapeDtypeStruct((n, LANES), F32), jax.ShapeDtypeStruct((META_ROWS, LANES), F32)],
        scratch_shapes=[pltpu.VMEM((n, LANES), F32), pltpu.VMEM((8, LANES), F32), pltpu.VMEM((8, LANES), F32)],
        compiler_params=_params(("arbitrary", "arbitrary")),
        name="route",
    )(logits, tri)


def _route(logits):
    n = logits.shape[0]
    a = n * TOP_K
    n_blocks = (a + N_EXPERTS * (MOE_BLK - 1) + MOE_BLK - 1) // MOE_BLK
    info, meta = _route_call(logits, n_blocks)
    dest = info[:, INFO_DEST:INFO_DEST + TOP_K].astype(jnp.int32)
    blk_e = meta[:n_blocks, 0].astype(jnp.int32)
    n_used = meta[n_blocks:n_blocks + 1, 0].astype(jnp.int32)
    tok = jnp.arange(a, dtype=jnp.int32) // TOP_K
    buf_tok = jnp.zeros((n_blocks * MOE_BLK,), jnp.int32).at[dest.reshape(a)].set(tok)
    return blk_e, n_used, buf_tok, dest, info


def kernel(x, p, positions, w_in, w_s, b_s, ln_v_g, ln_v_b, w_a, w_b, w_o, ln1_g, ln1_b, w_grp, b_grp, w_rt, b_rt, w1, w3, w2, w_pg, w_pp, ln2_g, ln2_b):
    batch, seq, d = x.shape
    depth = w_in.shape[0]
    n = batch * seq
    assert d == D_MODEL and w_in.shape[2] == PROJ_WIDTH
    cosf, sinf = _rotary_tables(positions)
    causal = jnp.tril(jnp.ones((CHUNK, CHUNK), F32))
    xf = x.reshape(n, d)
    xb = xf.astype(BF16)
    pad_r = ROUTER_LANES - N_EXPERT_GROUPS - N_EXPERTS
    w_pa, w_pb = _split_proj_weights(w_in)

    for i in range(depth):
        proj_a = _proj_a_call(xb, w_pa[i])
        qkv = _proj_b_call(xb, w_pb[i], cosf, sinf)
        ws = (w_s[i] * causal).astype(BF16)
        bs = jnp.repeat(b_s[i].T, SGU_GROUP_CH, axis=1)
        act = _sgu_call(proj_a, ws, bs, ln_v_g[i][None, :], ln_v_b[i][None, :])
        att = [_attn_call(qkv, g, batch, seq) for g in range(N_DIL)]
        wr = jnp.concatenate([w_grp[i], w_rt[i], jnp.zeros((d, pad_r), F32)], axis=1)
        wr_hi = wr.astype(BF16)
        wr_lo = (wr - wr_hi.astype(F32)).astype(BF16)
        br = jnp.concatenate([b_grp[i], b_rt[i], jnp.zeros((pad_r,), F32)])[None, :]
        x1, x1g, logits = _mix_call(xf, act, [a[0] for a in att], [a[1] for a in att], proj_a,
                                w_a[i].astype(BF16), w_b[i].astype(BF16), w_o[i].astype(BF16),
                                ln1_g[i][None, :], ln1_b[i][None, :], wr_hi, wr_lo, br)
        blk_e, n_used, buf_tok, pos, info = _route(logits)
        yb = _expert_call(blk_e, n_used, buf_tok, x1g, w1, w3, w2, i)
        xf, xb = _final_call(pos, info, x1, p[i].reshape(n, PLE_DIM), yb, w_pg[i].astype(BF16), w_pp[i].astype(BF16),
                             ln2_g[i][None, :], ln2_b[i][None, :])
    return xf.reshape(batch, seq, d)
```

```python
import functools

import jax
import jax.numpy as jnp
import numpy as np
from jax import lax
from jax.experimental import pallas as pl
from jax.experimental.pallas import tpu as pltpu

F32 = jnp.float32
BF16 = jnp.bfloat16

D_MODEL = 2048
DEPTH_FOR_DEEPNORM = 4
SGU_WIDTH = 1024
SGU_GROUPS = 8
SGU_GROUP_CH = SGU_WIDTH // SGU_GROUPS
CHUNK = 128
HEAD_DIM = 128
HEADS_PER_GROUP = 4
DILATED_GROUPS = ((128, 1), (512, 4), (2048, 16))
N_DIL = len(DILATED_GROUPS)
ATT_HEADS = HEADS_PER_GROUP * N_DIL
ATT_WIDTH = ATT_HEADS * HEAD_DIM
QBLK = 128
ROT_DIM = HEAD_DIM // 4
ROT_HALF = ROT_DIM // 2
ROPE_THETA = 500000.0
PROJ_WIDTH = 2 * SGU_WIDTH + 3 * ATT_WIDTH + 2 * D_MODEL
N_EXPERT_GROUPS = 4
EXPERTS_PER_GROUP = 8
N_EXPERTS = N_EXPERT_GROUPS * EXPERTS_PER_GROUP
TOP_K = 2
EXPERT_FF = 512
MOE_BLK = 256
PLE_DIM = 256
ALPHA = (2 * DEPTH_FOR_DEEPNORM) ** 0.25
LN_EPS = 1e-5
ROUTER_LANES = 128

LANES = 128
U32 = jnp.uint32
SLABS = D_MODEL // (2 * LANES)
ROW_PITCH = SLABS
HI_MASK = np.uint32(0xFFFF0000)

PROJ_TM = 1024
PROJ_A_TN = 1024
PROJ_B_TN = ATT_WIDTH
PROJ_CHUNK = 256
SGU_TM = 512
MIX_TM = 256
FINAL_TM = 256
SUB_ROWS = 128
GATHER_AHEAD = 2
GATHER_SLOTS = GATHER_AHEAD + 1
ROW_PRIORITY = (0,)
WEIGHT_PRIORITY = 1
CAST_ROWS = 128
EXPERT_CHUNK = 256
EXPERT_OUT_CHUNK = 512
FINAL_CHUNK = 512
ATTN_STEP = {1: (8, 4), 4: (2, 4), 16: (1, 2)}
ATTN_CLASS_UNROLL = 2
VMEM_LIMIT = 56 * 1024 * 1024

_REF_SPLITS = (0, SGU_WIDTH, 2 * SGU_WIDTH, 2 * SGU_WIDTH + ATT_WIDTH, 2 * SGU_WIDTH + 2 * ATT_WIDTH,
               2 * SGU_WIDTH + 3 * ATT_WIDTH, 2 * SGU_WIDTH + 3 * ATT_WIDTH + D_MODEL, PROJ_WIDTH)
_A_GB = D_MODEL
_A_U = 2 * D_MODEL
_A_V = 2 * D_MODEL + SGU_WIDTH
PROJ_A_WIDTH = 2 * D_MODEL + 2 * SGU_WIDTH


def _params(sem, vmem=VMEM_LIMIT):
    return pltpu.CompilerParams(dimension_semantics=sem, vmem_limit_bytes=vmem)


def _resident(shape):
    nd = len(shape)
    return pl.BlockSpec(shape, lambda *_: (0,) * nd, pipeline_mode=pl.Buffered(1))


def _layer_norm(y, g, b):
    mu = jnp.mean(y, axis=-1, keepdims=True)
    yc = y - mu
    var = jnp.mean(yc * yc, axis=-1, keepdims=True)
    return yc * lax.rsqrt(var + LN_EPS) * g + b


def _split_proj_weights(w):
    sec = [w[:, :, _REF_SPLITS[k]:_REF_SPLITS[k + 1]] for k in range(7)]
    wa = jnp.concatenate([sec[5], sec[6], sec[0], sec[1]], axis=2).astype(BF16)
    wb = w[:, :, _REF_SPLITS[2]:_REF_SPLITS[5]].astype(BF16)
    return wa, wb


def _rotary_tables(positions):
    inv_freq = ROPE_THETA ** (-jnp.arange(0, ROT_DIM, 2, dtype=F32) / ROT_DIM)
    ang = positions.astype(F32).reshape(-1)[:, None] * inv_freq
    cos, sin = jnp.cos(ang), jnp.sin(ang)
    n = ang.shape[0]
    cosf = jnp.concatenate([cos, cos, jnp.ones((n, HEAD_DIM - ROT_DIM), F32)], axis=1)
    sinf = jnp.concatenate([-sin, sin, jnp.zeros((n, HEAD_DIM - ROT_DIM), F32)], axis=1)
    return cosf, sinf


def _proj_a_kernel(x_ref, w_ref, o_ref):
    j = pl.program_id(1)
    tn = o_ref.shape[1]

    def run(act):
        for c in range(tn // PROJ_CHUNK):
            cols = slice(c * PROJ_CHUNK, (c + 1) * PROJ_CHUNK)
            acc = jnp.dot(x_ref[...], w_ref[:, cols], preferred_element_type=F32)
            o_ref[:, cols] = act(acc).astype(o_ref.dtype)

    @pl.when(j < _A_U // tn)
    def _():
        run(jax.nn.sigmoid)

    @pl.when(j >= _A_U // tn)
    def _():
        run(jax.nn.gelu)


def _proj_a_call(xb, w):
    n, d = xb.shape
    tm = min(PROJ_TM, n)
    tn = PROJ_A_TN
    assert _A_U % tn == 0 and PROJ_A_WIDTH % tn == 0
    return pl.pallas_call(
        _proj_a_kernel,
        grid=(n // tm, PROJ_A_WIDTH // tn),
        in_specs=[
            pl.BlockSpec((tm, d), lambda i, j: (i, 0)),
            pl.BlockSpec((d, tn), lambda i, j: (0, j)),
        ],
        out_specs=pl.BlockSpec((tm, tn), lambda i, j: (i, j)),
        out_shape=jax.ShapeDtypeStruct((n, PROJ_A_WIDTH), BF16),
        compiler_params=_params(("parallel", "arbitrary")),
        name="proj_a",
    )(xb, w)


def _proj_b_kernel(x_ref, w_ref, cos_ref, sin_ref, o_ref):
    j = pl.program_id(1)
    heads_per_chunk = PROJ_CHUNK // HEAD_DIM
    lane = lax.broadcasted_iota(jnp.int32, cos_ref.shape, 1)

    def run(rotate):
        for c in range(PROJ_B_TN // PROJ_CHUNK):
            cols = slice(c * PROJ_CHUNK, (c + 1) * PROJ_CHUNK)
            acc = jnp.dot(x_ref[...], w_ref[:, cols], preferred_element_type=F32)
            for h in range(heads_per_chunk):
                xh = acc[:, h * HEAD_DIM:(h + 1) * HEAD_DIM]
                if rotate:
                    partner = jnp.where(lane < ROT_HALF, pltpu.roll(xh, HEAD_DIM - ROT_HALF, 1),
                                        pltpu.roll(xh, ROT_HALF, 1))
                    xh = xh * cos_ref[...] + partner * sin_ref[...]
                o_ref[c * heads_per_chunk + h] = xh

    @pl.when(j < 2)
    def _():
        run(True)

    @pl.when(j >= 2)
    def _():
        run(False)


def _proj_b_call(xb, w, cosf, sinf):
    n, d = xb.shape
    tm = min(PROJ_TM, n)
    slabs = PROJ_B_TN // HEAD_DIM
    return pl.pallas_call(
        _proj_b_kernel,
        grid=(n // tm, 3),
        in_specs=[
            pl.BlockSpec((tm, d), lambda i, j: (i, 0)),
            pl.BlockSpec((d, PROJ_B_TN), lambda i, j: (0, j)),
            pl.BlockSpec((tm, HEAD_DIM), lambda i, j: (i, 0)),
            pl.BlockSpec((tm, HEAD_DIM), lambda i, j: (i, 0)),
        ],
        out_specs=pl.BlockSpec((slabs, tm, HEAD_DIM), lambda i, j: (j, i, 0)),
        out_shape=jax.ShapeDtypeStruct((3 * slabs, n, HEAD_DIM), F32),
        compiler_params=_params(("parallel", "arbitrary")),
        name="proj_b",
    )(xb, w, cosf, sinf)


def _sgu_kernel(u_ref, v_ref, ws_ref, bs_ref, g_ref, b_ref, o_ref):
    vn = _layer_norm(v_ref[...].astype(F32), g_ref[...], b_ref[...]).astype(BF16)
    tm = o_ref.shape[0]
    for c in range(tm // CHUNK):
        rows = slice(c * CHUNK, (c + 1) * CHUNK)
        for g in range(SGU_GROUPS):
            cols = slice(g * SGU_GROUP_CH, (g + 1) * SGU_GROUP_CH)
            z = jnp.dot(ws_ref[g], vn[rows, cols], preferred_element_type=F32) + bs_ref[:, cols]
            o_ref[rows, cols] = (u_ref[rows, cols].astype(F32) * z).astype(o_ref.dtype)


def _sgu_call(proj_a, ws, bs, ln_g, ln_b):
    n = proj_a.shape[0]
    tm = min(SGU_TM, n)
    u_blk, v_blk = _A_U // SGU_WIDTH, _A_V // SGU_WIDTH
    return pl.pallas_call(
        _sgu_kernel,
        grid=(n // tm,),
        in_specs=[
            pl.BlockSpec((tm, SGU_WIDTH), lambda i: (i, u_blk)),
            pl.BlockSpec((tm, SGU_WIDTH), lambda i: (i, v_blk)),
            _resident(ws.shape),
            _resident(bs.shape),
            _resident(ln_g.shape),
            _resident(ln_b.shape),
        ],
        out_specs=pl.BlockSpec((tm, SGU_WIDTH), lambda i: (i, 0)),
        out_shape=jax.ShapeDtypeStruct((n, SGU_WIDTH), BF16),
        compiler_params=_params(("parallel",)),
        name="sgu",
    )(proj_a, proj_a, ws, bs, ln_g, ln_b)


def _attn_kernel(q_ref, kp_ref, k_ref, vp_ref, v_ref, o_ref, lse_ref, *, dil, span, nq, hps):
    slab_idx = pl.program_id(1)
    hc = pl.program_id(2)
    qi = lax.broadcasted_iota(jnp.int32, (QBLK, QBLK), 0)
    kj = lax.broadcasted_iota(jnp.int32, (QBLK, QBLK), 1)
    dist_prev = qi + QBLK - kj
    dist_cur = qi - kj
    in_prev = (dist_prev >= 0) & (dist_prev <= span)
    mask_cur = (dist_cur >= 0) & (dist_cur <= span)
    scale = HEAD_DIM ** -0.5
    nt = (((1,), (1,)), ((), ()))
    lane = lax.broadcasted_iota(jnp.int32, (QBLK, LANES), 1)
    ones = jnp.ones((QBLK, HEAD_DIM), BF16)

    @pl.when(hc == 0)
    def _():
        lse_ref[...] = jnp.zeros_like(lse_ref)

    def rows(r, blk):
        return pl.ds(blk * QBLK * dil + r, QBLK, stride=dil) if dil > 1 else pl.ds(blk * QBLK, QBLK)

    def one_class(r):
        for blk in range(nq):
            if blk == 0:
                mask_prev = in_prev & (slab_idx > 0)
            else:
                mask_prev = in_prev
            sel = rows(r, blk)
            lse_rows = lse_ref[sel, :]
            for h in range(hps):
                q = (q_ref.at[h][sel, :] * scale).astype(BF16)
                if blk == 0:
                    kp = kp_ref.at[h][rows(r, 0), :].astype(BF16)
                    vp = vp_ref.at[h][rows(r, 0), :].astype(BF16)
                else:
                    kp = k_ref.at[h][rows(r, blk - 1), :].astype(BF16)
                    vp = v_ref.at[h][rows(r, blk - 1), :].astype(BF16)
                kc = k_ref.at[h][sel, :].astype(BF16)
                vc = v_ref.at[h][sel, :].astype(BF16)
                sp = jnp.where(mask_prev, lax.dot_general(q, kp, nt, preferred_element_type=F32), -jnp.inf)
                sc = jnp.where(mask_cur, lax.dot_general(q, kc, nt, preferred_element_type=F32), -jnp.inf)
                m = jnp.max(jnp.maximum(sp, sc), axis=-1, keepdims=True)
                ep = jnp.exp((sp - m).astype(BF16))
                ec = jnp.exp((sc - m).astype(BF16))
                od = jnp.dot(ep, jnp.concatenate([vp, ones], axis=1), preferred_element_type=F32)
                od = od + jnp.dot(ec, jnp.concatenate([vc, ones], axis=1), preferred_element_type=F32)
                den = od[:, HEAD_DIM:]
                o_ref.at[h][sel, :] = od[:, :HEAD_DIM] / den
                lse_rows = jnp.where(lane == hc * hps + h, m + jnp.log(den), lse_rows)
            lse_ref[sel, :] = lse_rows

    if dil == 1:
        one_class(0)
    else:
        def body(r2, carry):
            for u in range(ATTN_CLASS_UNROLL):
                one_class(r2 * ATTN_CLASS_UNROLL + u)
            return carry
        lax.fori_loop(0, dil // ATTN_CLASS_UNROLL, body, 0)


def _attn_call(qkv, group, batch, seq):
    window, dil = DILATED_GROUPS[group]
    span = window // dil
    nq, hps = ATTN_STEP[dil]
    slab = nq * QBLK * dil
    prev = QBLK * dil
    assert seq % slab == 0 and span <= QBLK and HEADS_PER_GROUP % hps == 0
    slabs_per_seq = seq // slab
    prev_per_seq = seq // prev
    n = batch * seq
    hblocks = HEADS_PER_GROUP // hps

    def cur(section):
        base = (section * ATT_HEADS + group * HEADS_PER_GROUP) // hps
        return pl.BlockSpec((hps, slab, LANES), lambda b, s, hc: (base + hc, b * slabs_per_seq + s, 0))

    def before(section):
        base = (section * ATT_HEADS + group * HEADS_PER_GROUP) // hps
        return pl.BlockSpec(
            (hps, prev, LANES),
            lambda b, s, hc: (base + hc, b * prev_per_seq + jnp.maximum(s * nq - 1, 0), 0))

    return pl.pallas_call(
        functools.partial(_attn_kernel, dil=dil, span=span, nq=nq, hps=hps),
        grid=(batch, slabs_per_seq, hblocks),
        in_specs=[cur(0), before(1), cur(1), before(2), cur(2)],
        out_specs=[
            pl.BlockSpec((hps, slab, LANES), lambda b, s, hc: (hc, b * slabs_per_seq + s, 0)),
            pl.BlockSpec((slab, LANES), lambda b, s, hc: (b * slabs_per_seq + s, 0)),
        ],
        out_shape=[
            jax.ShapeDtypeStruct((HEADS_PER_GROUP, n, LANES), F32),
            jax.ShapeDtypeStruct((n, LANES), F32),
        ],
        compiler_params=_params(("parallel", "arbitrary", "arbitrary")),
        name=f"attn_d{dil}",
    )(qkv, qkv, qkv, qkv, qkv)


def _mix_kernel(x_ref, act_ref, o0_ref, o1_ref, o2_ref, l0_ref, l1_ref, l2_ref, sga_ref, sgb_ref,
                wa_ref, wb_ref, wo_ref, g_ref, b_ref, wrh_ref, wrl_ref, br_ref, x1_ref, x1g_ref, logit_ref):
    tm = x_ref.shape[0]
    o_refs = (o0_ref, o1_ref, o2_ref)
    l_refs = (l0_ref, l1_ref, l2_ref)
    for sub in range(tm // SUB_ROWS):
        rows = pl.ds(sub * SUB_ROWS, SUB_ROWS)
        a_out = jnp.dot(act_ref[rows, :], wa_ref[...], preferred_element_type=F32)
        lses = [l[rows, :] for l in l_refs]
        heads = []
        for h in range(HEADS_PER_GROUP):
            lh = [l[:, h:h + 1] for l in lses]
            m = jnp.maximum(jnp.maximum(lh[0], lh[1]), lh[2])
            e = [jnp.exp(v - m) for v in lh]
            tot = e[0] + e[1] + e[2]
            acc = (e[0] / tot) * o_refs[0][h, rows, :]
            acc = acc + (e[1] / tot) * o_refs[1][h, rows, :]
            acc = acc + (e[2] / tot) * o_refs[2][h, rows, :]
            heads.append(acc.astype(BF16))
        merged = jnp.concatenate(heads, axis=1)
        b_out = jnp.dot(merged, wb_ref[...], preferred_element_type=F32)
        mixed_in = sga_ref[rows, :].astype(F32) * a_out + sgb_ref[rows, :].astype(F32) * b_out
        mixed = jnp.dot(mixed_in.astype(BF16), wo_ref[...], preferred_element_type=F32)
        x1 = _layer_norm(ALPHA * x_ref[rows, :] + mixed, g_ref[...], b_ref[...])
        x1_ref[rows, :] = x1
        _store_gather_rows(x1g_ref, sub * SUB_ROWS, x1)
        x_hi = x1.astype(BF16)
        x_lo = (x1 - x_hi.astype(F32)).astype(BF16)
        logits = jnp.dot(x_hi, wrh_ref[...], preferred_element_type=F32)
        logits = logits + jnp.dot(x_lo, wrh_ref[...], preferred_element_type=F32)
        logits = logits + jnp.dot(x_hi, wrl_ref[...], preferred_element_type=F32)
        logit_ref[rows, :] = logits + br_ref[...]


def _mix_call(x, act, os_, lses, proj_a, wa, wb, wo, g, b, wrh, wrl, br):
    n = x.shape[0]
    tm = min(MIX_TM, n)
    row = lambda w: pl.BlockSpec((tm, w), lambda i: (i, 0))
    o_spec = pl.BlockSpec((HEADS_PER_GROUP, tm, LANES), lambda i: (0, i, 0))
    return pl.pallas_call(
        _mix_kernel,
        grid=(n // tm,),
        in_specs=[
            row(D_MODEL), row(SGU_WIDTH),
            o_spec, o_spec, o_spec,
            row(LANES), row(LANES), row(LANES),
            pl.BlockSpec((tm, D_MODEL), lambda i: (i, 0)),
            pl.BlockSpec((tm, D_MODEL), lambda i: (i, _A_GB // D_MODEL)),
            _resident(wa.shape), _resident(wb.shape), _resident(wo.shape),
            _resident(g.shape), _resident(b.shape),
            _resident(wrh.shape), _resident(wrl.shape), _resident(br.shape),
        ],
        out_specs=[row(D_MODEL), pl.BlockSpec((tm * ROW_PITCH, LANES), lambda i: (i, 0)), row(ROUTER_LANES)],
        out_shape=[jax.ShapeDtypeStruct((n, D_MODEL), F32), jax.ShapeDtypeStruct((n * ROW_PITCH, LANES), U32),
                   jax.ShapeDtypeStruct((n, ROUTER_LANES), F32)],
        compiler_params=_params(("parallel",)),
        name="mix",
    )(x, act, *os_, *lses, proj_a, proj_a, wa, wb, wo, g, b, wrh, wrl, br)


def _start_row(idx_ref, r, src_hbm, dst, sem, priority):
    src = src_hbm.at[pl.ds(pl.multiple_of(idx_ref[0, r] * ROW_PITCH, ROW_PITCH), SLABS)]
    dst_row = r * ROW_PITCH if isinstance(r, int) else pl.multiple_of(r * ROW_PITCH, ROW_PITCH)
    pltpu.make_async_copy(src, dst.at[pl.ds(dst_row, SLABS)], sem).start(priority=priority)


def _start_row_gather(idx_ref, n_rows, src_hbm, dst, sem, priorities=(0, 1)):
    unroll = 8
    def body(blk, carry):
        for j in range(unroll):
            _start_row(idx_ref, blk * unroll + j, src_hbm, dst, sem, priorities[j % len(priorities)])
        return carry
    lax.fori_loop(0, n_rows // unroll, body, 0)


ANCHOR_ROWS = 8


def _interleaved_row_gather(idx_ref, n_rows, src_hbm, dst, sem, n_parts, priorities=(0, 1)):
    per = n_rows // n_parts
    def issue(part, after=None):
        if after is not None:
            dst[pl.ds(n_rows * ROW_PITCH, ANCHOR_ROWS), :] = pltpu.bitcast(after[0:ANCHOR_ROWS, 0:LANES], U32)
        for r in range(part * per, (part + 1) * per):
            _start_row(idx_ref, r, src_hbm, dst, sem, priorities[r % len(priorities)])
    return issue


def _wait_row_gather(n_rows, src_hbm, dst, sem):
    pltpu.make_async_copy(src_hbm.at[pl.ds(0, n_rows * SLABS)], dst.at[pl.ds(0, n_rows * SLABS)], sem).wait()


def _gathered_rows(buf, first_row, n_rows):
    lo, hi = [], []
    for k in range(SLABS):
        w = buf[pl.ds(first_row * ROW_PITCH + k, n_rows, stride=ROW_PITCH), :]
        lo.append(pltpu.bitcast(w << 16, F32))
        hi.append(pltpu.bitcast(w & HI_MASK, F32))
    return jnp.concatenate(lo + hi, axis=1)


def _pack_words(val, k):
    half = D_MODEL // 2
    lo = val[:, k * LANES:(k + 1) * LANES].astype(BF16).astype(F32)
    hi = val[:, half + k * LANES:half + (k + 1) * LANES].astype(BF16).astype(F32)
    return (pltpu.bitcast(lo, U32) >> 16) | pltpu.bitcast(hi, U32)


def _store_gather_rows(ref, first_row, val):
    n_rows = val.shape[0]
    for k in range(SLABS):
        ref[pl.ds(first_row * ROW_PITCH + k, n_rows, stride=ROW_PITCH), :] = _pack_words(val, k)
    if ROW_PITCH > SLABS:
        ref[pl.ds(first_row * ROW_PITCH + SLABS, n_rows, stride=ROW_PITCH), :] = jnp.zeros((n_rows, LANES), U32)


def _expert_kernel(blk_e_ref, n_used_ref, run_start_ref, next_e_ref, has_next_ref,
                   tok0_ref, tok1_ref, tok2_ref, x_hbm, w1_hbm, w3_hbm, w2_hbm,
                   o_ref, xbuf, w1s, w3s, w2s, w1b, w3b, w2b, sem, wsem, *, layer):
    i = pl.program_id(0)
    n_used = n_used_ref[0]
    slot = i % GATHER_SLOTS
    ahead_slot = (i + GATHER_AHEAD) % GATHER_SLOTS

    def weight_copies(e):
        return (pltpu.make_async_copy(w1_hbm.at[layer, e], w1s, wsem.at[0]),
                pltpu.make_async_copy(w3_hbm.at[layer, e], w3s, wsem.at[1]),
                pltpu.make_async_copy(w2_hbm.at[layer, e], w2s, wsem.at[2]))

    @pl.when((i == 0) & (n_used > 0))
    def _():
        _start_row_gather(tok0_ref, MOE_BLK, x_hbm, xbuf.at[0], sem.at[0], ROW_PRIORITY)
        _start_row_gather(tok1_ref, MOE_BLK, x_hbm, xbuf.at[1], sem.at[1], ROW_PRIORITY)
        for c in weight_copies(blk_e_ref[0]):
            c.start(priority=WEIGHT_PRIORITY)

    @pl.when((i < n_used) & (run_start_ref[i] == 1))
    def _():
        for c in weight_copies(blk_e_ref[i]):
            c.wait()
        for src, dst in ((w1s, w1b), (w3s, w3b), (w2s, w2b)):
            for r0 in range(0, src.shape[0], CAST_ROWS):
                dst[pl.ds(r0, CAST_ROWS), :] = src[pl.ds(r0, CAST_ROWS), :].astype(BF16)

        @pl.when(has_next_ref[i] == 1)
        def _():
            for c in weight_copies(next_e_ref[i]):
                c.start(priority=WEIGHT_PRIORITY)

    @pl.when(i < n_used)
    def _():
        _wait_row_gather(MOE_BLK, x_hbm, xbuf.at[slot], sem.at[slot])
        n_parts = 2 * (EXPERT_FF // EXPERT_CHUNK) + D_MODEL // EXPERT_OUT_CHUNK
        issue = _interleaved_row_gather(tok2_ref, MOE_BLK, x_hbm, xbuf.at[ahead_slot], sem.at[ahead_slot], n_parts,
                                        ROW_PRIORITY)
        part = 0
        last = None
        xb = _gathered_rows(xbuf.at[slot], 0, MOE_BLK).astype(BF16)
        hid = []
        for c in range(EXPERT_FF // EXPERT_CHUNK):
            cols = slice(c * EXPERT_CHUNK, (c + 1) * EXPERT_CHUNK)
            issue(part, last)
            h1 = jnp.dot(xb, w1b[:, cols], preferred_element_type=F32)
            issue(part + 1, h1)
            h3 = jnp.dot(xb, w3b[:, cols], preferred_element_type=F32)
            part += 2
            last = h3
            hid.append((jax.nn.silu(h1) * h3).astype(BF16))
        hid = jnp.concatenate(hid, axis=1)
        ys = []
        for c in range(D_MODEL // EXPERT_OUT_CHUNK):
            cols = slice(c * EXPERT_OUT_CHUNK, (c + 1) * EXPERT_OUT_CHUNK)
            issue(part, last)
            part += 1
            last = jnp.dot(hid, w2b[:, cols], preferred_element_type=F32)
            ys.append(last)
        _store_gather_rows(o_ref, 0, jnp.concatenate(ys, axis=1))

    @pl.when(i == n_used - 1)
    def _():
        for k in range(1, GATHER_SLOTS):
            other = (i + k) % GATHER_SLOTS
            _wait_row_gather(MOE_BLK, x_hbm, xbuf.at[other], sem.at[other])

    @pl.when(i >= n_used)
    def _():
        o_ref[...] = jnp.zeros_like(o_ref)


def _expert_plan(blk_e, n_used):
    n_blocks = blk_e.shape[0]
    j = jnp.arange(n_blocks, dtype=jnp.int32)
    run_start = ((j == 0) | (blk_e != jnp.roll(blk_e, 1))) & (j < n_used[0])
    start_idx = jnp.where(run_start, j, n_blocks)
    at_or_after = lax.cummin(start_idx, axis=0, reverse=True)
    nxt = jnp.concatenate([at_or_after[1:], jnp.full((1,), n_blocks, jnp.int32)])
    has_next = nxt < n_blocks
    next_e = blk_e[jnp.minimum(nxt, n_blocks - 1)]
    return run_start.astype(jnp.int32), next_e.astype(jnp.int32), has_next.astype(jnp.int32)


def _expert_call(blk_e, n_used, buf_tok, x1g, w1, w3, w2, layer):
    n_blocks = blk_e.shape[0]
    d = D_MODEL
    tok3 = buf_tok.reshape(n_blocks, 1, MOE_BLK)
    run_start, next_e, has_next = _expert_plan(blk_e, n_used)

    def tok_spec(k):
        return pl.BlockSpec((None, 1, MOE_BLK),
                            lambda i, e, u, *_: (jnp.minimum(i + k, jnp.maximum(u[0] - 1, 0)), 0, 0),
                            memory_space=pltpu.SMEM)

    assert GATHER_AHEAD == 2
    grid_spec = pltpu.PrefetchScalarGridSpec(
        num_scalar_prefetch=5,
        grid=(n_blocks,),
        in_specs=[
            tok_spec(0), tok_spec(1), tok_spec(2),
            pl.BlockSpec(memory_space=pl.ANY),
            pl.BlockSpec(memory_space=pl.ANY),
            pl.BlockSpec(memory_space=pl.ANY),
            pl.BlockSpec(memory_space=pl.ANY),
        ],
        out_specs=pl.BlockSpec((MOE_BLK * ROW_PITCH, LANES), lambda i, e, u, *_: (i, 0)),
        scratch_shapes=[pltpu.VMEM((GATHER_SLOTS, MOE_BLK * ROW_PITCH + ANCHOR_ROWS, LANES), U32),
                        pltpu.VMEM((d, EXPERT_FF), F32), pltpu.VMEM((d, EXPERT_FF), F32),
                        pltpu.VMEM((EXPERT_FF, d), F32),
                        pltpu.VMEM((d, EXPERT_FF), BF16), pltpu.VMEM((d, EXPERT_FF), BF16),
                        pltpu.VMEM((EXPERT_FF, d), BF16),
                        pltpu.SemaphoreType.DMA((GATHER_SLOTS,)), pltpu.SemaphoreType.DMA((3,))],
    )
    return pl.pallas_call(
        functools.partial(_expert_kernel, layer=layer),
        grid_spec=grid_spec,
        out_shape=jax.ShapeDtypeStruct((n_blocks * MOE_BLK * ROW_PITCH, LANES), U32),
        compiler_params=_params(("arbitrary",)),
        name="experts",
    )(blk_e, n_used, run_start, next_e, has_next, tok3, tok3, tok3, x1g, w1, w3, w2)


def _final_kernel(pos0_ref, pos1_ref, pos2_ref, x1_ref, p_ref, info_ref, yb_hbm, wpg_ref, wpp_ref, g_ref, b_ref,
                  x2_ref, x2b_ref, ybuf, sem):
    i = pl.program_id(0)
    n_steps = pl.num_programs(0)
    slot = i % GATHER_SLOTS
    ahead_slot = (i + GATHER_AHEAD) % GATHER_SLOTS
    tm = p_ref.shape[0]

    @pl.when(i == 0)
    def _():
        _start_row_gather(pos0_ref, TOP_K * tm, yb_hbm, ybuf.at[0], sem.at[0])
        _start_row_gather(pos1_ref, TOP_K * tm, yb_hbm, ybuf.at[1], sem.at[1])

    _wait_row_gather(TOP_K * tm, yb_hbm, ybuf.at[slot], sem.at[slot])
    n_sub = tm // SUB_ROWS
    n_chunks = D_MODEL // FINAL_CHUNK
    issue = _interleaved_row_gather(pos2_ref, TOP_K * tm, yb_hbm, ybuf.at[ahead_slot], sem.at[ahead_slot],
                                    n_sub * n_chunks)
    last = None
    for sub in range(n_sub):
        rows = pl.ds(sub * SUB_ROWS, SUB_ROWS)
        x1 = x1_ref[rows, :]
        x1b = x1.astype(BF16)
        pb = p_ref[rows, :].astype(BF16)
        pre = []
        for c in range(n_chunks):
            cols = slice(c * FINAL_CHUNK, (c + 1) * FINAL_CHUNK)
            issue(sub * n_chunks + c, last)
            gate = jax.nn.sigmoid(jnp.dot(x1b, wpg_ref[:, cols], preferred_element_type=F32))
            last = gate * jnp.dot(pb, wpp_ref[:, cols], preferred_element_type=F32)
            pre.append(last)
        pre = jnp.concatenate(pre, axis=1)
        info = info_ref[rows, :]
        y = (_gathered_rows(ybuf.at[slot], sub * SUB_ROWS, SUB_ROWS) * info[:, INFO_GATE:INFO_GATE + 1]
             + _gathered_rows(ybuf.at[slot], tm + sub * SUB_ROWS, SUB_ROWS) * info[:, INFO_GATE + 1:INFO_GATE + 2])
        x2 = _layer_norm(ALPHA * x1 + y + pre, g_ref[...], b_ref[...])
        x2_ref[rows, :] = x2
        x2b_ref[rows, :] = x2.astype(BF16)

    @pl.when(i == n_steps - 1)
    def _():
        for k in range(1, GATHER_SLOTS):
            other = (i + k) % GATHER_SLOTS
            _wait_row_gather(TOP_K * tm, yb_hbm, ybuf.at[other], sem.at[other])


def _final_call(pos, info, x1, p, yb, wpg, wpp, g, b):
    n = p.shape[0]
    d = D_MODEL
    tm = min(FINAL_TM, n)
    steps = n // tm
    pos3 = pos.reshape(steps, tm, TOP_K).transpose(0, 2, 1).reshape(steps, 1, TOP_K * tm)
    row = lambda w: pl.BlockSpec((tm, w), lambda i: (i, 0))

    def pos_spec(k):
        return pl.BlockSpec((None, 1, TOP_K * tm), lambda i: (jnp.minimum(i + k, steps - 1), 0, 0),
                            memory_space=pltpu.SMEM)

    assert GATHER_AHEAD == 2
    return pl.pallas_call(
        _final_kernel,
        grid=(steps,),
        in_specs=[
            pos_spec(0), pos_spec(1), pos_spec(2),
            row(d),
            row(PLE_DIM),
            row(LANES),
            pl.BlockSpec(memory_space=pl.ANY),
            _resident(wpg.shape), _resident(wpp.shape), _resident(g.shape), _resident(b.shape),
        ],
        out_specs=[row(d), row(d)],
        out_shape=[jax.ShapeDtypeStruct((n, d), F32), jax.ShapeDtypeStruct((n, d), BF16)],
        scratch_shapes=[pltpu.VMEM((GATHER_SLOTS, TOP_K * tm * ROW_PITCH + ANCHOR_ROWS, LANES), U32),
                        pltpu.SemaphoreType.DMA((GATHER_SLOTS,))],
        compiler_params=_params(("arbitrary",)),
        name="final",
    )(pos3, pos3, pos3, x1, p, info, yb, wpg, wpp, g, b)


INFO_DEST = 0
INFO_GATE = 2
EXPERT_LANE0 = N_EXPERT_GROUPS
ROUTE_TM = 512
META_ROWS = 256


def _lane_min_index(mask, lane_f):
    return jnp.min(jnp.where(mask, lane_f, float(LANES)), axis=1, keepdims=True)


def _route_kernel(logit_ref, tri_ref, info_ref, meta_ref, stash, run, pstart, *, n_blocks):
    phase = pl.program_id(0)
    i = pl.program_id(1)
    tm = logit_ref.shape[0]
    lane = lax.broadcasted_iota(jnp.int32, (tm, LANES), 1)
    lane_f = lane.astype(F32)
    rows = pl.ds(pl.multiple_of(i * tm, tm), tm)

    @pl.when((phase == 0) & (i == 0))
    def _():
        run[...] = jnp.zeros_like(run)

    @pl.when(phase == 0)
    def _():
        x = logit_ref[...]
        xg = jnp.where(lane < N_EXPERT_GROUPS, x, -jnp.inf)
        mg = jnp.max(xg, axis=1, keepdims=True)
        g_sel = _lane_min_index(xg == mg, lane_f)
        p_g = 1.0 / jnp.sum(jnp.exp(xg - mg), axis=1, keepdims=True)
        lo = EXPERT_LANE0 + EXPERTS_PER_GROUP * g_sel
        xe = jnp.where((lane_f >= lo) & (lane_f < lo + EXPERTS_PER_GROUP), x, -jnp.inf)
        v1 = jnp.max(xe, axis=1, keepdims=True)
        i1 = _lane_min_index(xe == v1, lane_f)
        xe2 = jnp.where(lane_f == i1, -jnp.inf, xe)
        v2 = jnp.max(xe2, axis=1, keepdims=True)
        i2 = _lane_min_index(xe2 == v2, lane_f)
        e21 = jnp.exp(v2 - v1)
        den = 1.0 + e21
        gate1 = (1.0 / den) * p_g
        gate2 = (e21 / den) * p_g
        o1 = lane_f == i1
        o2 = lane_f == i2
        both = jnp.where(o1 | o2, 1.0, 0.0)
        before = jnp.dot(tri_ref[...], both.astype(BF16), preferred_element_type=F32) + run[0:1, :]
        rank1 = jnp.sum(jnp.where(o1, before, 0.0), axis=1, keepdims=True)
        rank2 = jnp.sum(jnp.where(o2, before, 0.0), axis=1, keepdims=True)
        run[0:1, :] = run[0:1, :] + jnp.sum(both, axis=0, keepdims=True)
        info = jnp.where(lane == INFO_DEST, rank1, 0.0)
        info = jnp.where(lane == INFO_DEST + 1, rank2, info)
        info = jnp.where(lane == INFO_GATE, gate1, info)
        info = jnp.where(lane == INFO_GATE + 1, gate2, info)
        info = jnp.where(lane == INFO_GATE + 2, i1, info)
        info = jnp.where(lane == INFO_GATE + 3, i2, info)
        stash[rows, :] = info

    @pl.when((phase == 1) & (i == 0))
    def _():
        lane8 = lax.broadcasted_iota(jnp.int32, run.shape, 1)
        counts = jnp.where((lane8 >= EXPERT_LANE0) & (lane8 < EXPERT_LANE0 + N_EXPERTS), run[...], 0.0)
        padded = jnp.floor((counts + (MOE_BLK - 1)) * (1.0 / MOE_BLK)) * MOE_BLK
        pend = padded
        shift = 1
        while shift < LANES:
            pend = pend + jnp.where(lane8 >= shift, pltpu.roll(pend, shift, 1), 0.0)
            shift *= 2
        pstart[...] = pend - padded
        blk_row = lax.broadcasted_iota(jnp.int32, meta_ref.shape, 0)
        lane_m = lax.broadcasted_iota(jnp.int32, meta_ref.shape, 1)
        is_e = (lane_m >= EXPERT_LANE0) & (lane_m < EXPERT_LANE0 + N_EXPERTS)
        ended = is_e & (pend[0:1, :] <= (blk_row * MOE_BLK).astype(F32))
        blk_e = jnp.minimum(jnp.sum(jnp.where(ended, 1.0, 0.0), axis=1, keepdims=True), N_EXPERTS - 1.0)
        n_used = jnp.max(pend[0:1, :], axis=1, keepdims=True) * (1.0 / MOE_BLK)
        meta_ref[...] = jnp.where(blk_row == n_blocks, n_used, blk_e) + jnp.zeros(meta_ref.shape, F32)

    @pl.when(phase == 1)
    def _():
        info = stash[rows, :]
        i1 = info[:, INFO_GATE + 2:INFO_GATE + 3]
        i2 = info[:, INFO_GATE + 3:INFO_GATE + 4]
        start = pstart[0:1, :]
        s1 = jnp.sum(jnp.where(lane_f == i1, start, 0.0), axis=1, keepdims=True)
        s2 = jnp.sum(jnp.where(lane_f == i2, start, 0.0), axis=1, keepdims=True)
        out = jnp.where(lane == INFO_DEST, info + s1, info)
        out = jnp.where(lane == INFO_DEST + 1, info + s2, out)
        info_ref[...] = out


def _route_call(logits, n_blocks):
    n = logits.shape[0]
    tm = min(ROUTE_TM, n)
    assert n_blocks < META_ROWS
    tri = jnp.tril(jnp.ones((tm, tm), F32), k=-1).astype(BF16)
    return pl.pallas_call(
        functools.partial(_route_kernel, n_blocks=n_blocks),
        grid=(2, n // tm),
        in_specs=[
            pl.BlockSpec((tm, LANES), lambda ph, i: (jnp.where(ph == 0, i, n // tm - 1), 0)),
            _resident(tri.shape),
        ],
        out_specs=[
            pl.BlockSpec((tm, LANES), lambda ph, i: (i * ph, 0)),
            pl.BlockSpec((META_ROWS, LANES), lambda ph, i: (0, 0)),
        ],
        out_shape=[jax.ShapeDtypeStruct((n, LANES), F32), jax.ShapeDtypeStruct((META_ROWS, LANES), F32)],
        scratch_shapes=[pltpu.VMEM((n, LANES), F32), pltpu.VMEM((8, LANES), F32), pltpu.VMEM((8, LANES), F32)],
        compiler_params=_params(("arbitrary", "arbitrary")),
        name="route",
    )(logits, tri)


def _route(logits):
    n = logits.shape[0]
    a = n * TOP_K
    n_blocks = (a + N_EXPERTS * (MOE_BLK - 1) + MOE_BLK - 1) // MOE_BLK
    info, meta = _route_call(logits, n_blocks)
    dest = info[:, INFO_DEST:INFO_DEST + TOP_K].astype(jnp.int32)
    blk_e = meta[:n_blocks, 0].astype(jnp.int32)
    n_used = meta[n_blocks:n_blocks + 1, 0].astype(jnp.int32)
    tok = jnp.arange(a, dtype=jnp.int32) // TOP_K
    buf_tok = jnp.zeros((n_blocks * MOE_BLK,), jnp.int32).at[dest.reshape(a)].set(tok)
    return blk_e, n_used, buf_tok, dest, info


def kernel(x, p, positions, w_in, w_s, b_s, ln_v_g, ln_v_b, w_a, w_b, w_o, ln1_g, ln1_b, w_grp, b_grp, w_rt, b_rt, w1, w3, w2, w_pg, w_pp, ln2_g, ln2_b):
    batch, seq, d = x.shape
    depth = w_in.shape[0]
    n = batch * seq
    assert d == D_MODEL and w_in.shape[2] == PROJ_WIDTH
    cosf, sinf = _rotary_tables(positions)
    causal = jnp.tril(jnp.ones((CHUNK, CHUNK), F32))
    xf = x.reshape(n, d)
    xb = xf.astype(BF16)
    pad_r = ROUTER_LANES - N_EXPERT_GROUPS - N_EXPERTS
    w_pa, w_pb = _split_proj_weights(w_in)

    for i in range(depth):
        proj_a = _proj_a_call(xb, w_pa[i])
        qkv = _proj_b_call(xb, w_pb[i], cosf, sinf)
        ws = (w_s[i] * causal).astype(BF16)
        bs = jnp.repeat(b_s[i].T, SGU_GROUP_CH, axis=1)
        act = _sgu_call(proj_a, ws, bs, ln_v_g[i][None, :], ln_v_b[i][None, :])
        att = [_attn_call(qkv, g, batch, seq) for g in range(N_DIL)]
        wr = jnp.concatenate([w_grp[i], w_rt[i], jnp.zeros((d, pad_r), F32)], axis=1)
        wr_hi = wr.astype(BF16)
        wr_lo = (wr - wr_hi.astype(F32)).astype(BF16)
        br = jnp.concatenate([b_grp[i], b_rt[i], jnp.zeros((pad_r,), F32)])[None, :]
        x1, x1g, logits = _mix_call(xf, act, [a[0] for a in att], [a[1] for a in att], proj_a,
                                w_a[i].astype(BF16), w_b[i].astype(BF16), w_o[i].astype(BF16),
                                ln1_g[i][None, :], ln1_b[i][None, :], wr_hi, wr_lo, br)
        blk_e, n_used, buf_tok, pos, info = _route(logits)
        yb = _expert_call(blk_e, n_used, buf_tok, x1g, w1, w3, w2, i)
        xf, xb = _final_call(pos, info, x1, p[i].reshape(n, PLE_DIM), yb, w_pg[i].astype(BF16), w_pp[i].astype(BF16),
                             ln2_g[i][None, :], ln2_b[i][None, :])
    return xf.reshape(batch, seq, d)
```

```python
import functools

import jax
import jax.numpy as jnp
import numpy as np
from jax import lax
from jax.experimental import pallas as pl
from jax.experimental.pallas import tpu as pltpu

F32 = jnp.float32
BF16 = jnp.bfloat16

D_MODEL = 2048
DEPTH_FOR_DEEPNORM = 4
SGU_WIDTH = 1024
SGU_GROUPS = 8
SGU_GROUP_CH = SGU_WIDTH // SGU_GROUPS
CHUNK = 128
HEAD_DIM = 128
HEADS_PER_GROUP = 4
DILATED_GROUPS = ((128, 1), (512, 4), (2048, 16))
N_DIL = len(DILATED_GROUPS)
ATT_HEADS = HEADS_PER_GROUP * N_DIL
ATT_WIDTH = ATT_HEADS * HEAD_DIM
QBLK = 128
ROT_DIM = HEAD_DIM // 4
ROT_HALF = ROT_DIM // 2
ROPE_THETA = 500000.0
PROJ_WIDTH = 2 * SGU_WIDTH + 3 * ATT_WIDTH + 2 * D_MODEL
N_EXPERT_GROUPS = 4
EXPERTS_PER_GROUP = 8
N_EXPERTS = N_EXPERT_GROUPS * EXPERTS_PER_GROUP
TOP_K = 2
EXPERT_FF = 512
MOE_BLK = 256
PLE_DIM = 256
ALPHA = (2 * DEPTH_FOR_DEEPNORM) ** 0.25
LN_EPS = 1e-5
ROUTER_LANES = 128

LANES = 128
U32 = jnp.uint32
SLABS = D_MODEL // (2 * LANES)
ROW_PITCH = SLABS + 1
HI_MASK = np.uint32(0xFFFF0000)

PROJ_TM = 1024
PROJ_A_TN = 2048
PROJ_B_TN = ATT_WIDTH
PROJ_CHUNK = 256
SGU_TM = 512
MIX_TM = 256
FINAL_TM = 256
SUB_ROWS = 128
GATHER_AHEAD = 2
GATHER_SLOTS = GATHER_AHEAD + 1
ROW_PRIORITY = (0, 1)
WEIGHT_PRIORITY = 1
CAST_ROWS = 128
EXPERT_CHUNK = 256
EXPERT_OUT_CHUNK = 512
FINAL_CHUNK = 512
ATTN_STEP = {1: (8, 4), 4: (2, 4), 16: (1, 2)}
ATTN_CLASS_UNROLL = 2
VMEM_LIMIT = 56 * 1024 * 1024

_REF_SPLITS = (0, SGU_WIDTH, 2 * SGU_WIDTH, 2 * SGU_WIDTH + ATT_WIDTH, 2 * SGU_WIDTH + 2 * ATT_WIDTH,
               2 * SGU_WIDTH + 3 * ATT_WIDTH, 2 * SGU_WIDTH + 3 * ATT_WIDTH + D_MODEL, PROJ_WIDTH)
_A_GB = D_MODEL
_A_U = 2 * D_MODEL
_A_V = 2 * D_MODEL + SGU_WIDTH
PROJ_A_WIDTH = 2 * D_MODEL + 2 * SGU_WIDTH


def _params(sem, vmem=VMEM_LIMIT):
    return pltpu.CompilerParams(dimension_semantics=sem, vmem_limit_bytes=vmem)


def _resident(shape):
    nd = len(shape)
    return pl.BlockSpec(shape, lambda *_: (0,) * nd, pipeline_mode=pl.Buffered(1))


def _layer_norm(y, g, b):
    mu = jnp.mean(y, axis=-1, keepdims=True)
    yc = y - mu
    var = jnp.mean(yc * yc, axis=-1, keepdims=True)
    return yc * lax.rsqrt(var + LN_EPS) * g + b


def _split_proj_weights(w):
    sec = [w[:, :, _REF_SPLITS[k]:_REF_SPLITS[k + 1]] for k in range(7)]
    wa = jnp.concatenate([sec[5], sec[6], sec[0], sec[1]], axis=2).astype(BF16)
    wb = w[:, :, _REF_SPLITS[2]:_REF_SPLITS[5]].astype(BF16)
    return wa, wb


def _rotary_tables(positions):
    inv_freq = ROPE_THETA ** (-jnp.arange(0, ROT_DIM, 2, dtype=F32) / ROT_DIM)
    ang = positions.astype(F32).reshape(-1)[:, None] * inv_freq
    cos, sin = jnp.cos(ang), jnp.sin(ang)
    n = ang.shape[0]
    cosf = jnp.concatenate([cos, cos, jnp.ones((n, HEAD_DIM - ROT_DIM), F32)], axis=1)
    sinf = jnp.concatenate([-sin, sin, jnp.zeros((n, HEAD_DIM - ROT_DIM), F32)], axis=1)
    return cosf, sinf


def _proj_a_kernel(x_ref, w_ref, o_ref):
    j = pl.program_id(1)
    tn = o_ref.shape[1]

    def run(act):
        for c in range(tn // PROJ_CHUNK):
            cols = slice(c * PROJ_CHUNK, (c + 1) * PROJ_CHUNK)
            acc = jnp.dot(x_ref[...], w_ref[:, cols], preferred_element_type=F32)
            o_ref[:, cols] = act(acc).astype(o_ref.dtype)

    @pl.when(j < _A_U // tn)
    def _():
        run(jax.nn.sigmoid)

    @pl.when(j >= _A_U // tn)
    def _():
        run(jax.nn.gelu)


def _proj_a_call(xb, w):
    n, d = xb.shape
    tm = min(PROJ_TM, n)
    tn = PROJ_A_TN
    assert _A_U % tn == 0 and PROJ_A_WIDTH % tn == 0
    return pl.pallas_call(
        _proj_a_kernel,
        grid=(n // tm, PROJ_A_WIDTH // tn),
        in_specs=[
            pl.BlockSpec((tm, d), lambda i, j: (i, 0)),
            pl.BlockSpec((d, tn), lambda i, j: (0, j)),
        ],
        out_specs=pl.BlockSpec((tm, tn), lambda i, j: (i, j)),
        out_shape=jax.ShapeDtypeStruct((n, PROJ_A_WIDTH), BF16),
        compiler_params=_params(("parallel", "arbitrary")),
        name="proj_a",
    )(xb, w)


def _proj_b_kernel(x_ref, w_ref, cos_ref, sin_ref, o_ref):
    j = pl.program_id(1)
    heads_per_chunk = PROJ_CHUNK // HEAD_DIM
    lane = lax.broadcasted_iota(jnp.int32, cos_ref.shape, 1)

    def run(rotate):
        for c in range(PROJ_B_TN // PROJ_CHUNK):
            cols = slice(c * PROJ_CHUNK, (c + 1) * PROJ_CHUNK)
            acc = jnp.dot(x_ref[...], w_ref[:, cols], preferred_element_type=F32)
            for h in range(heads_per_chunk):
                xh = acc[:, h * HEAD_DIM:(h + 1) * HEAD_DIM]
                if rotate:
                    partner = jnp.where(lane < ROT_HALF, pltpu.roll(xh, HEAD_DIM - ROT_HALF, 1),
                                        pltpu.roll(xh, ROT_HALF, 1))
                    xh = xh * cos_ref[...] + partner * sin_ref[...]
                o_ref[c * heads_per_chunk + h] = xh

    @pl.when(j < 2)
    def _():
        run(True)

    @pl.when(j >= 2)
    def _():
        run(False)


def _proj_b_call(xb, w, cosf, sinf):
    n, d = xb.shape
    tm = min(PROJ_TM, n)
    slabs = PROJ_B_TN // HEAD_DIM
    return pl.pallas_call(
        _proj_b_kernel,
        grid=(n // tm, 3),
        in_specs=[
            pl.BlockSpec((tm, d), lambda i, j: (i, 0)),
            pl.BlockSpec((d, PROJ_B_TN), lambda i, j: (0, j)),
            pl.BlockSpec((tm, HEAD_DIM), lambda i, j: (i, 0)),
            pl.BlockSpec((tm, HEAD_DIM), lambda i, j: (i, 0)),
        ],
        out_specs=pl.BlockSpec((slabs, tm, HEAD_DIM), lambda i, j: (j, i, 0)),
        out_shape=jax.ShapeDtypeStruct((3 * slabs, n, HEAD_DIM), F32),
        compiler_params=_params(("parallel", "arbitrary")),
        name="proj_b",
    )(xb, w, cosf, sinf)


def _sgu_kernel(u_ref, v_ref, ws_ref, bs_ref, g_ref, b_ref, o_ref):
    vn = _layer_norm(v_ref[...].astype(F32), g_ref[...], b_ref[...]).astype(BF16)
    tm = o_ref.shape[0]
    for c in range(tm // CHUNK):
        rows = slice(c * CHUNK, (c + 1) * CHUNK)
        for g in range(SGU_GROUPS):
            cols = slice(g * SGU_GROUP_CH, (g + 1) * SGU_GROUP_CH)
            z = jnp.dot(ws_ref[g], vn[rows, cols], preferred_element_type=F32) + bs_ref[:, cols]
            o_ref[rows, cols] = (u_ref[rows, cols].astype(F32) * z).astype(o_ref.dtype)


def _sgu_call(proj_a, ws, bs, ln_g, ln_b):
    n = proj_a.shape[0]
    tm = min(SGU_TM, n)
    u_blk, v_blk = _A_U // SGU_WIDTH, _A_V // SGU_WIDTH
    return pl.pallas_call(
        _sgu_kernel,
        grid=(n // tm,),
        in_specs=[
            pl.BlockSpec((tm, SGU_WIDTH), lambda i: (i, u_blk)),
            pl.BlockSpec((tm, SGU_WIDTH), lambda i: (i, v_blk)),
            _resident(ws.shape),
            _resident(bs.shape),
            _resident(ln_g.shape),
            _resident(ln_b.shape),
        ],
        out_specs=pl.BlockSpec((tm, SGU_WIDTH), lambda i: (i, 0)),
        out_shape=jax.ShapeDtypeStruct((n, SGU_WIDTH), BF16),
        compiler_params=_params(("parallel",)),
        name="sgu",
    )(proj_a, proj_a, ws, bs, ln_g, ln_b)


def _attn_kernel(q_ref, kp_ref, k_ref, vp_ref, v_ref, o_ref, lse_ref, *, dil, span, nq, hps):
    slab_idx = pl.program_id(1)
    hc = pl.program_id(2)
    qi = lax.broadcasted_iota(jnp.int32, (QBLK, QBLK), 0)
    kj = lax.broadcasted_iota(jnp.int32, (QBLK, QBLK), 1)
    dist_prev = qi + QBLK - kj
    dist_cur = qi - kj
    in_prev = (dist_prev >= 0) & (dist_prev <= span)
    mask_cur = (dist_cur >= 0) & (dist_cur <= span)
    scale = HEAD_DIM ** -0.5
    nt = (((1,), (1,)), ((), ()))
    lane = lax.broadcasted_iota(jnp.int32, (QBLK, LANES), 1)
    ones = jnp.ones((QBLK, HEAD_DIM), BF16)

    @pl.when(hc == 0)
    def _():
        lse_ref[...] = jnp.zeros_like(lse_ref)

    def rows(r, blk):
        return pl.ds(blk * QBLK * dil + r, QBLK, stride=dil) if dil > 1 else pl.ds(blk * QBLK, QBLK)

    def one_class(r):
        for blk in range(nq):
            if blk == 0:
                mask_prev = in_prev & (slab_idx > 0)
            else:
                mask_prev = in_prev
            sel = rows(r, blk)
            lse_rows = lse_ref[sel, :]
            for h in range(hps):
                q = (q_ref.at[h][sel, :] * scale).astype(BF16)
                if blk == 0:
                    kp = kp_ref.at[h][rows(r, 0), :].astype(BF16)
                    vp = vp_ref.at[h][rows(r, 0), :].astype(BF16)
                else:
                    kp = k_ref.at[h][rows(r, blk - 1), :].astype(BF16)
                    vp = v_ref.at[h][rows(r, blk - 1), :].astype(BF16)
                kc = k_ref.at[h][sel, :].astype(BF16)
                vc = v_ref.at[h][sel, :].astype(BF16)
                sp = jnp.where(mask_prev, lax.dot_general(q, kp, nt, preferred_element_type=F32), -jnp.inf)
                sc = jnp.where(mask_cur, lax.dot_general(q, kc, nt, preferred_element_type=F32), -jnp.inf)
                m = jnp.max(jnp.maximum(sp, sc), axis=-1, keepdims=True)
                ep = jnp.exp((sp - m).astype(BF16))
                ec = jnp.exp((sc - m).astype(BF16))
                od = jnp.dot(ep, jnp.concatenate([vp, ones], axis=1), preferred_element_type=F32)
                od = od + jnp.dot(ec, jnp.concatenate([vc, ones], axis=1), preferred_element_type=F32)
                den = od[:, HEAD_DIM:]
                o_ref.at[h][sel, :] = od[:, :HEAD_DIM] / den
                lse_rows = jnp.where(lane == hc * hps + h, m + jnp.log(den), lse_rows)
            lse_ref[sel, :] = lse_rows

    if dil == 1:
        one_class(0)
    else:
        def body(r2, carry):
            for u in range(ATTN_CLASS_UNROLL):
                one_class(r2 * ATTN_CLASS_UNROLL + u)
            return carry
        lax.fori_loop(0, dil // ATTN_CLASS_UNROLL, body, 0)


def _attn_call(qkv, group, batch, seq):
    window, dil = DILATED_GROUPS[group]
    span = window // dil
    nq, hps = ATTN_STEP[dil]
    slab = nq * QBLK * dil
    prev = QBLK * dil
    assert seq % slab == 0 and span <= QBLK and HEADS_PER_GROUP % hps == 0
    slabs_per_seq = seq // slab
    prev_per_seq = seq // prev
    n = batch * seq
    hblocks = HEADS_PER_GROUP // hps

    def cur(section):
        base = (section * ATT_HEADS + group * HEADS_PER_GROUP) // hps
        return pl.BlockSpec((hps, slab, LANES), lambda b, s, hc: (base + hc, b * slabs_per_seq + s, 0))

    def before(section):
        base = (section * ATT_HEADS + group * HEADS_PER_GROUP) // hps
        return pl.BlockSpec(
            (hps, prev, LANES),
            lambda b, s, hc: (base + hc, b * prev_per_seq + jnp.maximum(s * nq - 1, 0), 0))

    return pl.pallas_call(
        functools.partial(_attn_kernel, dil=dil, span=span, nq=nq, hps=hps),
        grid=(batch, slabs_per_seq, hblocks),
        in_specs=[cur(0), before(1), cur(1), before(2), cur(2)],
        out_specs=[
            pl.BlockSpec((hps, slab, LANES), lambda b, s, hc: (hc, b * slabs_per_seq + s, 0)),
            pl.BlockSpec((slab, LANES), lambda b, s, hc: (b * slabs_per_seq + s, 0)),
        ],
        out_shape=[
            jax.ShapeDtypeStruct((HEADS_PER_GROUP, n, LANES), F32),
            jax.ShapeDtypeStruct((n, LANES), F32),
        ],
        compiler_params=_params(("parallel", "arbitrary", "arbitrary")),
        name=f"attn_d{dil}",
    )(qkv, qkv, qkv, qkv, qkv)


def _mix_kernel(x_ref, act_ref, o0_ref, o1_ref, o2_ref, l0_ref, l1_ref, l2_ref, sga_ref, sgb_ref,
                wa_ref, wb_ref, wo_ref, g_ref, b_ref, wrh_ref, wrl_ref, br_ref, x1_ref, x1g_ref, logit_ref):
    tm = x_ref.shape[0]
    o_refs = (o0_ref, o1_ref, o2_ref)
    l_refs = (l0_ref, l1_ref, l2_ref)
    for sub in range(tm // SUB_ROWS):
        rows = pl.ds(sub * SUB_ROWS, SUB_ROWS)
        a_out = jnp.dot(act_ref[rows, :], wa_ref[...], preferred_element_type=F32)
        lses = [l[rows, :] for l in l_refs]
        heads = []
        for h in range(HEADS_PER_GROUP):
            lh = [l[:, h:h + 1] for l in lses]
            m = jnp.maximum(jnp.maximum(lh[0], lh[1]), lh[2])
            e = [jnp.exp(v - m) for v in lh]
            tot = e[0] + e[1] + e[2]
            acc = (e[0] / tot) * o_refs[0][h, rows, :]
            acc = acc + (e[1] / tot) * o_refs[1][h, rows, :]
            acc = acc + (e[2] / tot) * o_refs[2][h, rows, :]
            heads.append(acc.astype(BF16))
        merged = jnp.concatenate(heads, axis=1)
        b_out = jnp.dot(merged, wb_ref[...], preferred_element_type=F32)
        mixed_in = sga_ref[rows, :].astype(F32) * a_out + sgb_ref[rows, :].astype(F32) * b_out
        mixed = jnp.dot(mixed_in.astype(BF16), wo_ref[...], preferred_element_type=F32)
        x1 = _layer_norm(ALPHA * x_ref[rows, :] + mixed, g_ref[...], b_ref[...])
        x1_ref[rows, :] = x1
        _store_gather_rows(x1g_ref, sub * SUB_ROWS, x1)
        x_hi = x1.astype(BF16)
        x_lo = (x1 - x_hi.astype(F32)).astype(BF16)
        both = jnp.dot(x_hi, wrl_ref[...], preferred_element_type=F32)
        logits = both[:, :ROUTER_LANES] + jnp.dot(x_lo, wrh_ref[...], preferred_element_type=F32)
        logits = logits + both[:, ROUTER_LANES:]
        logit_ref[rows, :] = logits + br_ref[...]


def _mix_call(x, act, os_, lses, proj_a, wa, wb, wo, g, b, wrh, wrl, br):
    n = x.shape[0]
    tm = min(MIX_TM, n)
    row = lambda w: pl.BlockSpec((tm, w), lambda i: (i, 0))
    o_spec = pl.BlockSpec((HEADS_PER_GROUP, tm, LANES), lambda i: (0, i, 0))
    return pl.pallas_call(
        _mix_kernel,
        grid=(n // tm,),
        in_specs=[
            row(D_MODEL), row(SGU_WIDTH),
            o_spec, o_spec, o_spec,
            row(LANES), row(LANES), row(LANES),
            pl.BlockSpec((tm, D_MODEL), lambda i: (i, 0)),
            pl.BlockSpec((tm, D_MODEL), lambda i: (i, _A_GB // D_MODEL)),
            _resident(wa.shape), _resident(wb.shape), _resident(wo.shape),
            _resident(g.shape), _resident(b.shape),
            _resident(wrh.shape), _resident(wrl.shape), _resident(br.shape),
        ],
        out_specs=[row(D_MODEL), pl.BlockSpec((tm * ROW_PITCH, LANES), lambda i: (i, 0)), row(ROUTER_LANES)],
        out_shape=[jax.ShapeDtypeStruct((n, D_MODEL), F32), jax.ShapeDtypeStruct((n * ROW_PITCH, LANES), U32),
                   jax.ShapeDtypeStruct((n, ROUTER_LANES), F32)],
        compiler_params=_params(("parallel",)),
        name="mix",
    )(x, act, *os_, *lses, proj_a, proj_a, wa, wb, wo, g, b, wrh, wrl, br)


def _start_row(idx_ref, r, src_hbm, dst, sem, priority):
    src = src_hbm.at[pl.ds(idx_ref[0, r] * ROW_PITCH, SLABS)]
    pltpu.make_async_copy(src, dst.at[pl.ds(r * ROW_PITCH, SLABS)], sem).start(priority=priority)


def _start_row_gather(idx_ref, n_rows, src_hbm, dst, sem, priorities=(0, 1)):
    unroll = 8
    def body(blk, carry):
        for j in range(unroll):
            _start_row(idx_ref, blk * unroll + j, src_hbm, dst, sem, priorities[j % len(priorities)])
        return carry
    lax.fori_loop(0, n_rows // unroll, body, 0)


ANCHOR_ROWS = 8


def _interleaved_row_gather(idx_ref, n_rows, src_hbm, dst, sem, n_parts, priorities=(0, 1)):
    per = n_rows // n_parts
    def issue(part, after=None):
        if after is not None:
            dst[pl.ds(n_rows * ROW_PITCH, ANCHOR_ROWS), :] = pltpu.bitcast(after[0:ANCHOR_ROWS, 0:LANES], U32)
        for r in range(part * per, (part + 1) * per):
            _start_row(idx_ref, r, src_hbm, dst, sem, priorities[r % len(priorities)])
    return issue


def _wait_row_gather(n_rows, src_hbm, dst, sem):
    pltpu.make_async_copy(src_hbm.at[pl.ds(0, n_rows * SLABS)], dst.at[pl.ds(0, n_rows * SLABS)], sem).wait()


def _gathered_rows(buf, first_row, n_rows):
    lo, hi = [], []
    for k in range(SLABS):
        w = buf[pl.ds(first_row * ROW_PITCH + k, n_rows, stride=ROW_PITCH), :]
        lo.append(pltpu.bitcast(w << 16, F32))
        hi.append(pltpu.bitcast(w & HI_MASK, F32))
    return jnp.concatenate(lo + hi, axis=1)


def _pack_words(val, k):
    half = D_MODEL // 2
    lo = val[:, k * LANES:(k + 1) * LANES].astype(BF16).astype(F32)
    hi = val[:, half + k * LANES:half + (k + 1) * LANES].astype(BF16).astype(F32)
    return (pltpu.bitcast(lo, U32) >> 16) | pltpu.bitcast(hi, U32)


def _store_gather_rows(ref, first_row, val):
    n_rows = val.shape[0]
    for k in range(SLABS):
        ref[pl.ds(first_row * ROW_PITCH + k, n_rows, stride=ROW_PITCH), :] = _pack_words(val, k)
    ref[pl.ds(first_row * ROW_PITCH + SLABS, n_rows, stride=ROW_PITCH), :] = jnp.zeros((n_rows, LANES), U32)


def _expert_kernel(blk_e_ref, n_used_ref, run_start_ref, next_e_ref, has_next_ref,
                   tok0_ref, tok1_ref, tok2_ref, x_hbm, w1_hbm, w3_hbm, w2_hbm,
                   o_ref, xbuf, w1s, w3s, w2s, w1b, w3b, w2b, sem, wsem, *, layer):
    i = pl.program_id(0)
    n_used = n_used_ref[0]
    slot = i % GATHER_SLOTS
    ahead_slot = (i + GATHER_AHEAD) % GATHER_SLOTS

    def weight_copies(e):
        return (pltpu.make_async_copy(w1_hbm.at[layer, e], w1s, wsem.at[0]),
                pltpu.make_async_copy(w3_hbm.at[layer, e], w3s, wsem.at[1]),
                pltpu.make_async_copy(w2_hbm.at[layer, e], w2s, wsem.at[2]))

    @pl.when((i == 0) & (n_used > 0))
    def _():
        _start_row_gather(tok0_ref, MOE_BLK, x_hbm, xbuf.at[0], sem.at[0], ROW_PRIORITY)
        _start_row_gather(tok1_ref, MOE_BLK, x_hbm, xbuf.at[1], sem.at[1], ROW_PRIORITY)
        for c in weight_copies(blk_e_ref[0]):
            c.start(priority=WEIGHT_PRIORITY)

    @pl.when((i < n_used) & (run_start_ref[i] == 1))
    def _():
        for c in weight_copies(blk_e_ref[i]):
            c.wait()
        for src, dst in ((w1s, w1b), (w3s, w3b), (w2s, w2b)):
            for r0 in range(0, src.shape[0], CAST_ROWS):
                dst[pl.ds(r0, CAST_ROWS), :] = src[pl.ds(r0, CAST_ROWS), :].astype(BF16)

        @pl.when(has_next_ref[i] == 1)
        def _():
            for c in weight_copies(next_e_ref[i]):
                c.start(priority=WEIGHT_PRIORITY)

    @pl.when(i < n_used)
    def _():
        _wait_row_gather(MOE_BLK, x_hbm, xbuf.at[slot], sem.at[slot])
        n_parts = 2 * (EXPERT_FF // EXPERT_CHUNK) + D_MODEL // EXPERT_OUT_CHUNK
        issue = _interleaved_row_gather(tok2_ref, MOE_BLK, x_hbm, xbuf.at[ahead_slot], sem.at[ahead_slot], n_parts,
                                        ROW_PRIORITY)
        part = 0
        last = None
        xb = _gathered_rows(xbuf.at[slot], 0, MOE_BLK).astype(BF16)
        hid = []
        for c in range(EXPERT_FF // EXPERT_CHUNK):
            cols = slice(c * EXPERT_CHUNK, (c + 1) * EXPERT_CHUNK)
            issue(part, last)
            h1 = jnp.dot(xb, w1b[:, cols], preferred_element_type=F32)
            issue(part + 1, h1)
            h3 = jnp.dot(xb, w3b[:, cols], preferred_element_type=F32)
            part += 2
            last = h3
            hid.append((jax.nn.silu(h1) * h3).astype(BF16))
        hid = jnp.concatenate(hid, axis=1)
        ys = []
        for c in range(D_MODEL // EXPERT_OUT_CHUNK):
            cols = slice(c * EXPERT_OUT_CHUNK, (c + 1) * EXPERT_OUT_CHUNK)
            issue(part, last)
            part += 1
            last = jnp.dot(hid, w2b[:, cols], preferred_element_type=F32)
            ys.append(last)
        _store_gather_rows(o_ref, 0, jnp.concatenate(ys, axis=1))

    @pl.when(i == n_used - 1)
    def _():
        for k in range(1, GATHER_SLOTS):
            other = (i + k) % GATHER_SLOTS
            _wait_row_gather(MOE_BLK, x_hbm, xbuf.at[other], sem.at[other])

    @pl.when(i >= n_used)
    def _():
        o_ref[...] = jnp.zeros_like(o_ref)


def _expert_plan(blk_e, n_used):
    n_blocks = blk_e.shape[0]
    j = jnp.arange(n_blocks, dtype=jnp.int32)
    run_start = ((j == 0) | (blk_e != jnp.roll(blk_e, 1))) & (j < n_used[0])
    start_idx = jnp.where(run_start, j, n_blocks)
    at_or_after = lax.cummin(start_idx, axis=0, reverse=True)
    nxt = jnp.concatenate([at_or_after[1:], jnp.full((1,), n_blocks, jnp.int32)])
    has_next = nxt < n_blocks
    next_e = blk_e[jnp.minimum(nxt, n_blocks - 1)]
    return run_start.astype(jnp.int32), next_e.astype(jnp.int32), has_next.astype(jnp.int32)


def _expert_call(blk_e, n_used, buf_tok, x1g, w1, w3, w2, layer):
    n_blocks = blk_e.shape[0]
    d = D_MODEL
    tok3 = buf_tok.reshape(n_blocks, 1, MOE_BLK)
    run_start, next_e, has_next = _expert_plan(blk_e, n_used)

    def tok_spec(k):
        return pl.BlockSpec((None, 1, MOE_BLK),
                            lambda i, e, u, *_: (jnp.minimum(i + k, jnp.maximum(u[0] - 1, 0)), 0, 0),
                            memory_space=pltpu.SMEM)

    assert GATHER_AHEAD == 2
    grid_spec = pltpu.PrefetchScalarGridSpec(
        num_scalar_prefetch=5,
        grid=(n_blocks,),
        in_specs=[
            tok_spec(0), tok_spec(1), tok_spec(2),
            pl.BlockSpec(memory_space=pl.ANY),
            pl.BlockSpec(memory_space=pl.ANY),
            pl.BlockSpec(memory_space=pl.ANY),
            pl.BlockSpec(memory_space=pl.ANY),
        ],
        out_specs=pl.BlockSpec((MOE_BLK * ROW_PITCH, LANES), lambda i, e, u, *_: (i, 0)),
        scratch_shapes=[pltpu.VMEM((GATHER_SLOTS, MOE_BLK * ROW_PITCH + ANCHOR_ROWS, LANES), U32),
                        pltpu.VMEM((d, EXPERT_FF), F32), pltpu.VMEM((d, EXPERT_FF), F32),
                        pltpu.VMEM((EXPERT_FF, d), F32),
                        pltpu.VMEM((d, EXPERT_FF), BF16), pltpu.VMEM((d, EXPERT_FF), BF16),
                        pltpu.VMEM((EXPERT_FF, d), BF16),
                        pltpu.SemaphoreType.DMA((GATHER_SLOTS,)), pltpu.SemaphoreType.DMA((3,))],
    )
    return pl.pallas_call(
        functools.partial(_expert_kernel, layer=layer),
        grid_spec=grid_spec,
        out_shape=jax.ShapeDtypeStruct((n_blocks * MOE_BLK * ROW_PITCH, LANES), U32),
        compiler_params=_params(("arbitrary",)),
        name="experts",
    )(blk_e, n_used, run_start, next_e, has_next, tok3, tok3, tok3, x1g, w1, w3, w2)


def _final_kernel(pos0_ref, pos1_ref, pos2_ref, x1_ref, p_ref, info_ref, yb_hbm, wpg_ref, wpp_ref, g_ref, b_ref,
                  x2_ref, x2b_ref, ybuf, sem):
    i = pl.program_id(0)
    n_steps = pl.num_programs(0)
    slot = i % GATHER_SLOTS
    ahead_slot = (i + GATHER_AHEAD) % GATHER_SLOTS
    tm = p_ref.shape[0]

    @pl.when(i == 0)
    def _():
        _start_row_gather(pos0_ref, TOP_K * tm, yb_hbm, ybuf.at[0], sem.at[0])
        _start_row_gather(pos1_ref, TOP_K * tm, yb_hbm, ybuf.at[1], sem.at[1])

    _wait_row_gather(TOP_K * tm, yb_hbm, ybuf.at[slot], sem.at[slot])
    n_sub = tm // SUB_ROWS
    n_chunks = D_MODEL // FINAL_CHUNK
    issue = _interleaved_row_gather(pos2_ref, TOP_K * tm, yb_hbm, ybuf.at[ahead_slot], sem.at[ahead_slot],
                                    n_sub * n_chunks)
    last = None
    for sub in range(n_sub):
        rows = pl.ds(sub * SUB_ROWS, SUB_ROWS)
        x1 = x1_ref[rows, :]
        x1b = x1.astype(BF16)
        pb = p_ref[rows, :].astype(BF16)
        pre = []
        for c in range(n_chunks):
            cols = slice(c * FINAL_CHUNK, (c + 1) * FINAL_CHUNK)
            issue(sub * n_chunks + c, last)
            gate = jax.nn.sigmoid(jnp.dot(x1b, wpg_ref[:, cols], preferred_element_type=F32))
            last = gate * jnp.dot(pb, wpp_ref[:, cols], preferred_element_type=F32)
            pre.append(last)
        pre = jnp.concatenate(pre, axis=1)
        info = info_ref[rows, :]
        y = (_gathered_rows(ybuf.at[slot], sub * SUB_ROWS, SUB_ROWS) * info[:, INFO_GATE:INFO_GATE + 1]
             + _gathered_rows(ybuf.at[slot], tm + sub * SUB_ROWS, SUB_ROWS) * info[:, INFO_GATE + 1:INFO_GATE + 2])
        x2 = _layer_norm(ALPHA * x1 + y + pre, g_ref[...], b_ref[...])
        x2_ref[rows, :] = x2
        x2b_ref[rows, :] = x2.astype(BF16)

    @pl.when(i == n_steps - 1)
    def _():
        for k in range(1, GATHER_SLOTS):
            other = (i + k) % GATHER_SLOTS
            _wait_row_gather(TOP_K * tm, yb_hbm, ybuf.at[other], sem.at[other])


def _final_call(pos, info, x1, p, yb, wpg, wpp, g, b):
    n = p.shape[0]
    d = D_MODEL
    tm = min(FINAL_TM, n)
    steps = n // tm
    pos3 = pos.reshape(steps, tm, TOP_K).transpose(0, 2, 1).reshape(steps, 1, TOP_K * tm)
    row = lambda w: pl.BlockSpec((tm, w), lambda i: (i, 0))

    def pos_spec(k):
        return pl.BlockSpec((None, 1, TOP_K * tm), lambda i: (jnp.minimum(i + k, steps - 1), 0, 0),
                            memory_space=pltpu.SMEM)

    assert GATHER_AHEAD == 2
    return pl.pallas_call(
        _final_kernel,
        grid=(steps,),
        in_specs=[
            pos_spec(0), pos_spec(1), pos_spec(2),
            row(d),
            row(PLE_DIM),
            row(LANES),
            pl.BlockSpec(memory_space=pl.ANY),
            _resident(wpg.shape), _resident(wpp.shape), _resident(g.shape), _resident(b.shape),
        ],
        out_specs=[row(d), row(d)],
        out_shape=[jax.ShapeDtypeStruct((n, d), F32), jax.ShapeDtypeStruct((n, d), BF16)],
        scratch_shapes=[pltpu.VMEM((GATHER_SLOTS, TOP_K * tm * ROW_PITCH + ANCHOR_ROWS, LANES), U32),
                        pltpu.SemaphoreType.DMA((GATHER_SLOTS,))],
        compiler_params=_params(("arbitrary",)),
        name="final",
    )(pos3, pos3, pos3, x1, p, info, yb, wpg, wpp, g, b)


INFO_DEST = 0
INFO_GATE = 2
EXPERT_LANE0 = N_EXPERT_GROUPS
ROUTE_TM = 512
META_ROWS = 256


def _lane_min_index(mask, lane_f):
    return jnp.min(jnp.where(mask, lane_f, float(LANES)), axis=1, keepdims=True)


def _route_kernel(logit_ref, tri_ref, info_ref, meta_ref, stash, run, pstart, *, n_blocks):
    phase = pl.program_id(0)
    i = pl.program_id(1)
    tm = logit_ref.shape[0]
    lane = lax.broadcasted_iota(jnp.int32, (tm, LANES), 1)
    lane_f = lane.astype(F32)
    rows = pl.ds(pl.multiple_of(i * tm, tm), tm)

    @pl.when((phase == 0) & (i == 0))
    def _():
        run[...] = jnp.zeros_like(run)

    @pl.when(phase == 0)
    def _():
        x = logit_ref[...]
        xg = jnp.where(lane < N_EXPERT_GROUPS, x, -jnp.inf)
        mg = jnp.max(xg, axis=1, keepdims=True)
        g_sel = _lane_min_index(xg == mg, lane_f)
        p_g = 1.0 / jnp.sum(jnp.exp(xg - mg), axis=1, keepdims=True)
        lo = EXPERT_LANE0 + EXPERTS_PER_GROUP * g_sel
        xe = jnp.where((lane_f >= lo) & (lane_f < lo + EXPERTS_PER_GROUP), x, -jnp.inf)
        v1 = jnp.max(xe, axis=1, keepdims=True)
        i1 = _lane_min_index(xe == v1, lane_f)
        xe2 = jnp.where(lane_f == i1, -jnp.inf, xe)
        v2 = jnp.max(xe2, axis=1, keepdims=True)
        i2 = _lane_min_index(xe2 == v2, lane_f)
        e21 = jnp.exp(v2 - v1)
        den = 1.0 + e21
        gate1 = (1.0 / den) * p_g
        gate2 = (e21 / den) * p_g
        o1 = lane_f == i1
        o2 = lane_f == i2
        both = jnp.where(o1 | o2, 1.0, 0.0)
        before = jnp.dot(tri_ref[...], both.astype(BF16), preferred_element_type=F32) + run[0:1, :]
        rank1 = jnp.sum(jnp.where(o1, before, 0.0), axis=1, keepdims=True)
        rank2 = jnp.sum(jnp.where(o2, before, 0.0), axis=1, keepdims=True)
        run[0:1, :] = run[0:1, :] + jnp.sum(both, axis=0, keepdims=True)
        info = jnp.where(lane == INFO_DEST, rank1, 0.0)
        info = jnp.where(lane == INFO_DEST + 1, rank2, info)
        info = jnp.where(lane == INFO_GATE, gate1, info)
        info = jnp.where(lane == INFO_GATE + 1, gate2, info)
        info = jnp.where(lane == INFO_GATE + 2, i1, info)
        info = jnp.where(lane == INFO_GATE + 3, i2, info)
        stash[rows, :] = info

    @pl.when((phase == 1) & (i == 0))
    def _():
        lane8 = lax.broadcasted_iota(jnp.int32, run.shape, 1)
        counts = jnp.where((lane8 >= EXPERT_LANE0) & (lane8 < EXPERT_LANE0 + N_EXPERTS), run[...], 0.0)
        padded = jnp.floor((counts + (MOE_BLK - 1)) * (1.0 / MOE_BLK)) * MOE_BLK
        pend = padded
        shift = 1
        while shift < LANES:
            pend = pend + jnp.where(lane8 >= shift, pltpu.roll(pend, shift, 1), 0.0)
            shift *= 2
        pstart[...] = pend - padded
        blk_row = lax.broadcasted_iota(jnp.int32, meta_ref.shape, 0)
        lane_m = lax.broadcasted_iota(jnp.int32, meta_ref.shape, 1)
        is_e = (lane_m >= EXPERT_LANE0) & (lane_m < EXPERT_LANE0 + N_EXPERTS)
        ended = is_e & (pend[0:1, :] <= (blk_row * MOE_BLK).astype(F32))
        blk_e = jnp.minimum(jnp.sum(jnp.where(ended, 1.0, 0.0), axis=1, keepdims=True), N_EXPERTS - 1.0)
        n_used = jnp.max(pend[0:1, :], axis=1, keepdims=True) * (1.0 / MOE_BLK)
        meta_ref[...] = jnp.where(blk_row == n_blocks, n_used, blk_e) + jnp.zeros(meta_ref.shape, F32)

    @pl.when(phase == 1)
    def _():
        info = stash[rows, :]
        i1 = info[:, INFO_GATE + 2:INFO_GATE + 3]
        i2 = info[:, INFO_GATE + 3:INFO_GATE + 4]
        start = pstart[0:1, :]
        s1 = jnp.sum(jnp.where(lane_f == i1, start, 0.0), axis=1, keepdims=True)
        s2 = jnp.sum(jnp.where(lane_f == i2, start, 0.0), axis=1, keepdims=True)
        out = jnp.where(lane == INFO_DEST, info + s1, info)
        out = jnp.where(lane == INFO_DEST + 1, info + s2, out)
        info_ref[...] = out


def _route_call(logits, n_blocks):
    n = logits.shape[0]
    tm = min(ROUTE_TM, n)
    assert n_blocks < META_ROWS
    tri = jnp.tril(jnp.ones((tm, tm), F32), k=-1).astype(BF16)
    return pl.pallas_call(
        functools.partial(_route_kernel, n_blocks=n_blocks),
        grid=(2, n // tm),
        in_specs=[
            pl.BlockSpec((tm, LANES), lambda ph, i: (jnp.where(ph == 0, i, n // tm - 1), 0)),
            _resident(tri.shape),
        ],
        out_specs=[
            pl.BlockSpec((tm, LANES), lambda ph, i: (i * ph, 0)),
            pl.BlockSpec((META_ROWS, LANES), lambda ph, i: (0, 0)),
        ],
        out_shape=[jax.ShapeDtypeStruct((n, LANES), F32), jax.ShapeDtypeStruct((META_ROWS, LANES), F32)],
        scratch_shapes=[pltpu.VMEM((n, LANES), F32), pltpu.VMEM((8, LANES), F32), pltpu.VMEM((8, LANES), F32)],
        compiler_params=_params(("arbitrary", "arbitrary")),
        name="route",
    )(logits, tri)


def _route(logits):
    n = logits.shape[0]
    a = n * TOP_K
    n_blocks = (a + N_EXPERTS * (MOE_BLK - 1) + MOE_BLK - 1) // MOE_BLK
    info, meta = _route_call(logits, n_blocks)
    dest = info[:, INFO_DEST:INFO_DEST + TOP_K].astype(jnp.int32)
    blk_e = meta[:n_blocks, 0].astype(jnp.int32)
    n_used = meta[n_blocks:n_blocks + 1, 0].astype(jnp.int32)
    tok = jnp.arange(a, dtype=jnp.int32) // TOP_K
    buf_tok = jnp.zeros((n_blocks * MOE_BLK,), jnp.int32).at[dest.reshape(a)].set(tok)
    return blk_e, n_used, buf_tok, dest, info


def kernel(x, p, positions, w_in, w_s, b_s, ln_v_g, ln_v_b, w_a, w_b, w_o, ln1_g, ln1_b, w_grp, b_grp, w_rt, b_rt, w1, w3, w2, w_pg, w_pp, ln2_g, ln2_b):
    batch, seq, d = x.shape
    depth = w_in.shape[0]
    n = batch * seq
    assert d == D_MODEL and w_in.shape[2] == PROJ_WIDTH
    cosf, sinf = _rotary_tables(positions)
    causal = jnp.tril(jnp.ones((CHUNK, CHUNK), F32))
    xf = x.reshape(n, d)
    xb = xf.astype(BF16)
    pad_r = ROUTER_LANES - N_EXPERT_GROUPS - N_EXPERTS
    w_pa, w_pb = _split_proj_weights(w_in)

    for i in range(depth):
        proj_a = _proj_a_call(xb, w_pa[i])
        qkv = _proj_b_call(xb, w_pb[i], cosf, sinf)
        ws = (w_s[i] * causal).astype(BF16)
        bs = jnp.repeat(b_s[i].T, SGU_GROUP_CH, axis=1)
        act = _sgu_call(proj_a, ws, bs, ln_v_g[i][None, :], ln_v_b[i][None, :])
        att = [_attn_call(qkv, g, batch, seq) for g in range(N_DIL)]
        wr = jnp.concatenate([w_grp[i], w_rt[i], jnp.zeros((d, pad_r), F32)], axis=1)
        wr_hi = wr.astype(BF16)
        wr_lo = jnp.concatenate([wr_hi, (wr - wr_hi.astype(F32)).astype(BF16)], axis=1)
        br = jnp.concatenate([b_grp[i], b_rt[i], jnp.zeros((pad_r,), F32)])[None, :]
        x1, x1g, logits = _mix_call(xf, act, [a[0] for a in att], [a[1] for a in att], proj_a,
                                w_a[i].astype(BF16), w_b[i].astype(BF16), w_o[i].astype(BF16),
                                ln1_g[i][None, :], ln1_b[i][None, :], wr_hi, wr_lo, br)
        blk_e, n_used, buf_tok, pos, info = _route(logits)
        yb = _expert_call(blk_e, n_used, buf_tok, x1g, w1, w3, w2, i)
        xf, xb = _final_call(pos, info, x1, p[i].reshape(n, PLE_DIM), yb, w_pg[i].astype(BF16), w_pp[i].astype(BF16),
                             ln2_g[i][None, :], ln2_b[i][None, :])
    return xf.reshape(batch, seq, d)
```

```python
import functools

import jax
import jax.numpy as jnp
import numpy as np
from jax import lax
from jax.experimental import pallas as pl
from jax.experimental.pallas import tpu as pltpu

F32 = jnp.float32
BF16 = jnp.bfloat16

D_MODEL = 2048
DEPTH_FOR_DEEPNORM = 4
SGU_WIDTH = 1024
SGU_GROUPS = 8
SGU_GROUP_CH = SGU_WIDTH // SGU_GROUPS
CHUNK = 128
HEAD_DIM = 128
HEADS_PER_GROUP = 4
DILATED_GROUPS = ((128, 1), (512, 4), (2048, 16))
N_DIL = len(DILATED_GROUPS)
ATT_HEADS = HEADS_PER_GROUP * N_DIL
ATT_WIDTH = ATT_HEADS * HEAD_DIM
QBLK = 128
ROT_DIM = HEAD_DIM // 4
ROT_HALF = ROT_DIM // 2
ROPE_THETA = 500000.0
PROJ_WIDTH = 2 * SGU_WIDTH + 3 * ATT_WIDTH + 2 * D_MODEL
N_EXPERT_GROUPS = 4
EXPERTS_PER_GROUP = 8
N_EXPERTS = N_EXPERT_GROUPS * EXPERTS_PER_GROUP
TOP_K = 2
EXPERT_FF = 512
MOE_BLK = 256
PLE_DIM = 256
ALPHA = (2 * DEPTH_FOR_DEEPNORM) ** 0.25
LN_EPS = 1e-5
ROUTER_LANES = 128

LANES = 128
U32 = jnp.uint32
SLABS = D_MODEL // (2 * LANES)
ROW_PITCH = SLABS + 1
HI_MASK = np.uint32(0xFFFF0000)

PROJ_TM = 1024
PROJ_A_TN = 2048
PROJ_B_TN = ATT_WIDTH
PROJ_CHUNK = 256
SGU_TM = 512
MIX_TM = 256
FINAL_TM = 256
SUB_ROWS = 128
GATHER_AHEAD = 2
GATHER_SLOTS = GATHER_AHEAD + 1
ROW_PRIORITY = (0,)
WEIGHT_PRIORITY = 1
CAST_ROWS = 128
EXPERT_CHUNK = 256
EXPERT_OUT_CHUNK = 512
FINAL_CHUNK = 512
ATTN_STEP = {1: (8, 4), 4: (2, 4), 16: (1, 2)}
ATTN_CLASS_UNROLL = 4
VMEM_LIMIT = 56 * 1024 * 1024

_REF_SPLITS = (0, SGU_WIDTH, 2 * SGU_WIDTH, 2 * SGU_WIDTH + ATT_WIDTH, 2 * SGU_WIDTH + 2 * ATT_WIDTH,
               2 * SGU_WIDTH + 3 * ATT_WIDTH, 2 * SGU_WIDTH + 3 * ATT_WIDTH + D_MODEL, PROJ_WIDTH)
_A_GB = D_MODEL
_A_U = 2 * D_MODEL
_A_V = 2 * D_MODEL + SGU_WIDTH
PROJ_A_WIDTH = 2 * D_MODEL + 2 * SGU_WIDTH


def _params(sem, vmem=VMEM_LIMIT):
    return pltpu.CompilerParams(dimension_semantics=sem, vmem_limit_bytes=vmem)


def _resident(shape):
    nd = len(shape)
    return pl.BlockSpec(shape, lambda *_: (0,) * nd, pipeline_mode=pl.Buffered(1))


def _layer_norm(y, g, b):
    mu = jnp.mean(y, axis=-1, keepdims=True)
    yc = y - mu
    var = jnp.mean(yc * yc, axis=-1, keepdims=True)
    return yc * lax.rsqrt(var + LN_EPS) * g + b


def _split_proj_weights(w):
    sec = [w[:, :, _REF_SPLITS[k]:_REF_SPLITS[k + 1]] for k in range(7)]
    wa = jnp.concatenate([sec[5], sec[6], sec[0], sec[1]], axis=2).astype(BF16)
    wb = w[:, :, _REF_SPLITS[2]:_REF_SPLITS[5]].astype(BF16)
    return wa, wb


def _rotary_tables(positions):
    inv_freq = ROPE_THETA ** (-jnp.arange(0, ROT_DIM, 2, dtype=F32) / ROT_DIM)
    ang = positions.astype(F32).reshape(-1)[:, None] * inv_freq
    cos, sin = jnp.cos(ang), jnp.sin(ang)
    n = ang.shape[0]
    cosf = jnp.concatenate([cos, cos, jnp.ones((n, HEAD_DIM - ROT_DIM), F32)], axis=1)
    sinf = jnp.concatenate([-sin, sin, jnp.zeros((n, HEAD_DIM - ROT_DIM), F32)], axis=1)
    return cosf, sinf


def _proj_a_kernel(x_ref, w_ref, o_ref):
    j = pl.program_id(1)
    tn = o_ref.shape[1]

    def run(act):
        for c in range(tn // PROJ_CHUNK):
            cols = slice(c * PROJ_CHUNK, (c + 1) * PROJ_CHUNK)
            acc = jnp.dot(x_ref[...], w_ref[:, cols], preferred_element_type=F32)
            o_ref[:, cols] = act(acc).astype(o_ref.dtype)

    @pl.when(j < _A_U // tn)
    def _():
        run(jax.nn.sigmoid)

    @pl.when(j >= _A_U // tn)
    def _():
        run(jax.nn.gelu)


def _proj_a_call(xb, w):
    n, d = xb.shape
    tm = min(PROJ_TM, n)
    tn = PROJ_A_TN
    assert _A_U % tn == 0 and PROJ_A_WIDTH % tn == 0
    return pl.pallas_call(
        _proj_a_kernel,
        grid=(n // tm, PROJ_A_WIDTH // tn),
        in_specs=[
            pl.BlockSpec((tm, d), lambda i, j: (i, 0)),
            pl.BlockSpec((d, tn), lambda i, j: (0, j)),
        ],
        out_specs=pl.BlockSpec((tm, tn), lambda i, j: (i, j)),
        out_shape=jax.ShapeDtypeStruct((n, PROJ_A_WIDTH), BF16),
        compiler_params=_params(("parallel", "arbitrary")),
        name="proj_a",
    )(xb, w)


def _proj_b_kernel(x_ref, w_ref, cos_ref, sin_ref, o_ref):
    j = pl.program_id(1)
    heads_per_chunk = PROJ_CHUNK // HEAD_DIM
    lane = lax.broadcasted_iota(jnp.int32, cos_ref.shape, 1)

    def run(rotate):
        for c in range(PROJ_B_TN // PROJ_CHUNK):
            cols = slice(c * PROJ_CHUNK, (c + 1) * PROJ_CHUNK)
            acc = jnp.dot(x_ref[...], w_ref[:, cols], preferred_element_type=F32)
            for h in range(heads_per_chunk):
                xh = acc[:, h * HEAD_DIM:(h + 1) * HEAD_DIM]
                if rotate:
                    partner = jnp.where(lane < ROT_HALF, pltpu.roll(xh, HEAD_DIM - ROT_HALF, 1),
                                        pltpu.roll(xh, ROT_HALF, 1))
                    xh = xh * cos_ref[...] + partner * sin_ref[...]
                o_ref[c * heads_per_chunk + h] = xh

    @pl.when(j < 2)
    def _():
        run(True)

    @pl.when(j >= 2)
    def _():
        run(False)


def _proj_b_call(xb, w, cosf, sinf):
    n, d = xb.shape
    tm = min(PROJ_TM, n)
    slabs = PROJ_B_TN // HEAD_DIM
    return pl.pallas_call(
        _proj_b_kernel,
        grid=(n // tm, 3),
        in_specs=[
            pl.BlockSpec((tm, d), lambda i, j: (i, 0)),
            pl.BlockSpec((d, PROJ_B_TN), lambda i, j: (0, j)),
            pl.BlockSpec((tm, HEAD_DIM), lambda i, j: (i, 0)),
            pl.BlockSpec((tm, HEAD_DIM), lambda i, j: (i, 0)),
        ],
        out_specs=pl.BlockSpec((slabs, tm, HEAD_DIM), lambda i, j: (j, i, 0)),
        out_shape=jax.ShapeDtypeStruct((3 * slabs, n, HEAD_DIM), F32),
        compiler_params=_params(("parallel", "arbitrary")),
        name="proj_b",
    )(xb, w, cosf, sinf)


def _sgu_kernel(u_ref, v_ref, ws_ref, bs_ref, g_ref, b_ref, o_ref):
    vn = _layer_norm(v_ref[...].astype(F32), g_ref[...], b_ref[...]).astype(BF16)
    tm = o_ref.shape[0]
    for c in range(tm // CHUNK):
        rows = slice(c * CHUNK, (c + 1) * CHUNK)
        for g in range(SGU_GROUPS):
            cols = slice(g * SGU_GROUP_CH, (g + 1) * SGU_GROUP_CH)
            z = jnp.dot(ws_ref[g], vn[rows, cols], preferred_element_type=F32) + bs_ref[:, cols]
            o_ref[rows, cols] = (u_ref[rows, cols].astype(F32) * z).astype(o_ref.dtype)


def _sgu_call(proj_a, ws, bs, ln_g, ln_b):
    n = proj_a.shape[0]
    tm = min(SGU_TM, n)
    u_blk, v_blk = _A_U // SGU_WIDTH, _A_V // SGU_WIDTH
    return pl.pallas_call(
        _sgu_kernel,
        grid=(n // tm,),
        in_specs=[
            pl.BlockSpec((tm, SGU_WIDTH), lambda i: (i, u_blk)),
            pl.BlockSpec((tm, SGU_WIDTH), lambda i: (i, v_blk)),
            _resident(ws.shape),
            _resident(bs.shape),
            _resident(ln_g.shape),
            _resident(ln_b.shape),
        ],
        out_specs=pl.BlockSpec((tm, SGU_WIDTH), lambda i: (i, 0)),
        out_shape=jax.ShapeDtypeStruct((n, SGU_WIDTH), BF16),
        compiler_params=_params(("parallel",)),
        name="sgu",
    )(proj_a, proj_a, ws, bs, ln_g, ln_b)


def _attn_kernel(q_ref, kp_ref, k_ref, vp_ref, v_ref, o_ref, lse_ref, *, dil, span, nq, hps):
    slab_idx = pl.program_id(1)
    hc = pl.program_id(2)
    qi = lax.broadcasted_iota(jnp.int32, (QBLK, QBLK), 0)
    kj = lax.broadcasted_iota(jnp.int32, (QBLK, QBLK), 1)
    dist_prev = qi + QBLK - kj
    dist_cur = qi - kj
    in_prev = (dist_prev >= 0) & (dist_prev <= span)
    mask_cur = (dist_cur >= 0) & (dist_cur <= span)
    scale = HEAD_DIM ** -0.5
    nt = (((1,), (1,)), ((), ()))
    lane = lax.broadcasted_iota(jnp.int32, (QBLK, LANES), 1)
    ones = jnp.ones((QBLK, HEAD_DIM), BF16)

    @pl.when(hc == 0)
    def _():
        lse_ref[...] = jnp.zeros_like(lse_ref)

    def rows(r, blk):
        return pl.ds(blk * QBLK * dil + r, QBLK, stride=dil) if dil > 1 else pl.ds(blk * QBLK, QBLK)

    def one_class(r):
        for blk in range(nq):
            if blk == 0:
                mask_prev = in_prev & (slab_idx > 0)
            else:
                mask_prev = in_prev
            sel = rows(r, blk)
            lse_rows = lse_ref[sel, :]
            for h in range(hps):
                q = (q_ref.at[h][sel, :] * scale).astype(BF16)
                if blk == 0:
                    kp = kp_ref.at[h][rows(r, 0), :].astype(BF16)
                    vp = vp_ref.at[h][rows(r, 0), :].astype(BF16)
                else:
                    kp = k_ref.at[h][rows(r, blk - 1), :].astype(BF16)
                    vp = v_ref.at[h][rows(r, blk - 1), :].astype(BF16)
                kc = k_ref.at[h][sel, :].astype(BF16)
                vc = v_ref.at[h][sel, :].astype(BF16)
                sp = jnp.where(mask_prev, lax.dot_general(q, kp, nt, preferred_element_type=F32), -jnp.inf)
                sc = jnp.where(mask_cur, lax.dot_general(q, kc, nt, preferred_element_type=F32), -jnp.inf)
                m = jnp.max(jnp.maximum(sp, sc), axis=-1, keepdims=True)
                ep = jnp.exp((sp - m).astype(BF16))
                ec = jnp.exp((sc - m).astype(BF16))
                od = jnp.dot(ep, jnp.concatenate([vp, ones], axis=1), preferred_element_type=F32)
                od = od + jnp.dot(ec, jnp.concatenate([vc, ones], axis=1), preferred_element_type=F32)
                den = od[:, HEAD_DIM:]
                o_ref.at[h][sel, :] = od[:, :HEAD_DIM] / den
                lse_rows = jnp.where(lane == hc * hps + h, m + jnp.log(den), lse_rows)
            lse_ref[sel, :] = lse_rows

    if dil == 1:
        one_class(0)
    else:
        def body(r2, carry):
            for u in range(ATTN_CLASS_UNROLL):
                one_class(r2 * ATTN_CLASS_UNROLL + u)
            return carry
        lax.fori_loop(0, dil // ATTN_CLASS_UNROLL, body, 0)


def _attn_call(qkv, group, batch, seq):
    window, dil = DILATED_GROUPS[group]
    span = window // dil
    nq, hps = ATTN_STEP[dil]
    slab = nq * QBLK * dil
    prev = QBLK * dil
    assert seq % slab == 0 and span <= QBLK and HEADS_PER_GROUP % hps == 0
    slabs_per_seq = seq // slab
    prev_per_seq = seq // prev
    n = batch * seq
    hblocks = HEADS_PER_GROUP // hps

    def cur(section):
        base = (section * ATT_HEADS + group * HEADS_PER_GROUP) // hps
        return pl.BlockSpec((hps, slab, LANES), lambda b, s, hc: (base + hc, b * slabs_per_seq + s, 0))

    def before(section):
        base = (section * ATT_HEADS + group * HEADS_PER_GROUP) // hps
        return pl.BlockSpec(
            (hps, prev, LANES),
            lambda b, s, hc: (base + hc, b * prev_per_seq + jnp.maximum(s * nq - 1, 0), 0))

    return pl.pallas_call(
        functools.partial(_attn_kernel, dil=dil, span=span, nq=nq, hps=hps),
        grid=(batch, slabs_per_seq, hblocks),
        in_specs=[cur(0), before(1), cur(1), before(2), cur(2)],
        out_specs=[
            pl.BlockSpec((hps, slab, LANES), lambda b, s, hc: (hc, b * slabs_per_seq + s, 0)),
            pl.BlockSpec((slab, LANES), lambda b, s, hc: (b * slabs_per_seq + s, 0)),
        ],
        out_shape=[
            jax.ShapeDtypeStruct((HEADS_PER_GROUP, n, LANES), F32),
            jax.ShapeDtypeStruct((n, LANES), F32),
        ],
        compiler_params=_params(("parallel", "arbitrary", "arbitrary")),
        name=f"attn_d{dil}",
    )(qkv, qkv, qkv, qkv, qkv)


def _mix_kernel(x_ref, act_ref, o0_ref, o1_ref, o2_ref, l0_ref, l1_ref, l2_ref, sga_ref, sgb_ref,
                wa_ref, wb_ref, wo_ref, g_ref, b_ref, wrh_ref, wrl_ref, br_ref, x1_ref, x1g_ref, logit_ref):
    tm = x_ref.shape[0]
    o_refs = (o0_ref, o1_ref, o2_ref)
    l_refs = (l0_ref, l1_ref, l2_ref)
    for sub in range(tm // SUB_ROWS):
        rows = pl.ds(sub * SUB_ROWS, SUB_ROWS)
        a_out = jnp.dot(act_ref[rows, :], wa_ref[...], preferred_element_type=F32)
        lses = [l[rows, :] for l in l_refs]
        heads = []
        for h in range(HEADS_PER_GROUP):
            lh = [l[:, h:h + 1] for l in lses]
            m = jnp.maximum(jnp.maximum(lh[0], lh[1]), lh[2])
            e = [jnp.exp(v - m) for v in lh]
            tot = e[0] + e[1] + e[2]
            acc = (e[0] / tot) * o_refs[0][h, rows, :]
            acc = acc + (e[1] / tot) * o_refs[1][h, rows, :]
            acc = acc + (e[2] / tot) * o_refs[2][h, rows, :]
            heads.append(acc.astype(BF16))
        merged = jnp.concatenate(heads, axis=1)
        b_out = jnp.dot(merged, wb_ref[...], preferred_element_type=F32)
        mixed_in = sga_ref[rows, :].astype(F32) * a_out + sgb_ref[rows, :].astype(F32) * b_out
        mixed = jnp.dot(mixed_in.astype(BF16), wo_ref[...], preferred_element_type=F32)
        x1 = _layer_norm(ALPHA * x_ref[rows, :] + mixed, g_ref[...], b_ref[...])
        x1_ref[rows, :] = x1
        _store_gather_rows(x1g_ref, sub * SUB_ROWS, x1)
        x_hi = x1.astype(BF16)
        x_lo = (x1 - x_hi.astype(F32)).astype(BF16)
        both = jnp.dot(x_hi, wrl_ref[...], preferred_element_type=F32)
        logits = both[:, :ROUTER_LANES] + jnp.dot(x_lo, wrh_ref[...], preferred_element_type=F32)
        logits = logits + both[:, ROUTER_LANES:]
        logit_ref[rows, :] = logits + br_ref[...]


def _mix_call(x, act, os_, lses, proj_a, wa, wb, wo, g, b, wrh, wrl, br):
    n = x.shape[0]
    tm = min(MIX_TM, n)
    row = lambda w: pl.BlockSpec((tm, w), lambda i: (i, 0))
    o_spec = pl.BlockSpec((HEADS_PER_GROUP, tm, LANES), lambda i: (0, i, 0))
    return pl.pallas_call(
        _mix_kernel,
        grid=(n // tm,),
        in_specs=[
            row(D_MODEL), row(SGU_WIDTH),
            o_spec, o_spec, o_spec,
            row(LANES), row(LANES), row(LANES),
            pl.BlockSpec((tm, D_MODEL), lambda i: (i, 0)),
            pl.BlockSpec((tm, D_MODEL), lambda i: (i, _A_GB // D_MODEL)),
            _resident(wa.shape), _resident(wb.shape), _resident(wo.shape),
            _resident(g.shape), _resident(b.shape),
            _resident(wrh.shape), _resident(wrl.shape), _resident(br.shape),
        ],
        out_specs=[row(D_MODEL), pl.BlockSpec((tm * ROW_PITCH, LANES), lambda i: (i, 0)), row(ROUTER_LANES)],
        out_shape=[jax.ShapeDtypeStruct((n, D_MODEL), F32), jax.ShapeDtypeStruct((n * ROW_PITCH, LANES), U32),
                   jax.ShapeDtypeStruct((n, ROUTER_LANES), F32)],
        compiler_params=_params(("parallel",)),
        name="mix",
    )(x, act, *os_, *lses, proj_a, proj_a, wa, wb, wo, g, b, wrh, wrl, br)


def _start_row(idx_ref, r, src_hbm, dst, sem, priority):
    src = src_hbm.at[pl.ds(idx_ref[0, r] * ROW_PITCH, SLABS)]
    pltpu.make_async_copy(src, dst.at[pl.ds(r * ROW_PITCH, SLABS)], sem).start(priority=priority)


def _start_row_gather(idx_ref, n_rows, src_hbm, dst, sem, priorities=(0, 1)):
    unroll = 8
    def body(blk, carry):
        for j in range(unroll):
            _start_row(idx_ref, blk * unroll + j, src_hbm, dst, sem, priorities[j % len(priorities)])
        return carry
    lax.fori_loop(0, n_rows // unroll, body, 0)


ANCHOR_ROWS = 8


def _interleaved_row_gather(idx_ref, n_rows, src_hbm, dst, sem, n_parts, priorities=(0, 1)):
    per = n_rows // n_parts
    def issue(part, after=None):
        if after is not None:
            dst[pl.ds(n_rows * ROW_PITCH, ANCHOR_ROWS), :] = pltpu.bitcast(after[0:ANCHOR_ROWS, 0:LANES], U32)
        for r in range(part * per, (part + 1) * per):
            _start_row(idx_ref, r, src_hbm, dst, sem, priorities[r % len(priorities)])
    return issue


def _wait_row_gather(n_rows, src_hbm, dst, sem):
    pltpu.make_async_copy(src_hbm.at[pl.ds(0, n_rows * SLABS)], dst.at[pl.ds(0, n_rows * SLABS)], sem).wait()


def _gathered_rows(buf, first_row, n_rows):
    lo, hi = [], []
    for k in range(SLABS):
        w = buf[pl.ds(first_row * ROW_PITCH + k, n_rows, stride=ROW_PITCH), :]
        lo.append(pltpu.bitcast(w << 16, F32))
        hi.append(pltpu.bitcast(w & HI_MASK, F32))
    return jnp.concatenate(lo + hi, axis=1)


def _pack_words(val, k):
    half = D_MODEL // 2
    lo = val[:, k * LANES:(k + 1) * LANES].astype(BF16).astype(F32)
    hi = val[:, half + k * LANES:half + (k + 1) * LANES].astype(BF16).astype(F32)
    return (pltpu.bitcast(lo, U32) >> 16) | pltpu.bitcast(hi, U32)


def _store_gather_rows(ref, first_row, val):
    n_rows = val.shape[0]
    for k in range(SLABS):
        ref[pl.ds(first_row * ROW_PITCH + k, n_rows, stride=ROW_PITCH), :] = _pack_words(val, k)
    ref[pl.ds(first_row * ROW_PITCH + SLABS, n_rows, stride=ROW_PITCH), :] = jnp.zeros((n_rows, LANES), U32)


def _expert_kernel(blk_e_ref, n_used_ref, run_start_ref, next_e_ref, has_next_ref,
                   tok0_ref, tok1_ref, tok2_ref, x_hbm, w1_hbm, w3_hbm, w2_hbm,
                   o_ref, xbuf, w1s, w3s, w2s, w1b, w3b, w2b, sem, wsem, *, layer):
    i = pl.program_id(0)
    n_used = n_used_ref[0]
    slot = i % GATHER_SLOTS
    ahead_slot = (i + GATHER_AHEAD) % GATHER_SLOTS

    def weight_copies(e):
        return (pltpu.make_async_copy(w1_hbm.at[layer, e], w1s, wsem.at[0]),
                pltpu.make_async_copy(w3_hbm.at[layer, e], w3s, wsem.at[1]),
                pltpu.make_async_copy(w2_hbm.at[layer, e], w2s, wsem.at[2]))

    @pl.when((i == 0) & (n_used > 0))
    def _():
        _start_row_gather(tok0_ref, MOE_BLK, x_hbm, xbuf.at[0], sem.at[0], ROW_PRIORITY)
        _start_row_gather(tok1_ref, MOE_BLK, x_hbm, xbuf.at[1], sem.at[1], ROW_PRIORITY)
        for c in weight_copies(blk_e_ref[0]):
            c.start(priority=WEIGHT_PRIORITY)

    @pl.when((i < n_used) & (run_start_ref[i] == 1))
    def _():
        for c in weight_copies(blk_e_ref[i]):
            c.wait()
        for src, dst in ((w1s, w1b), (w3s, w3b), (w2s, w2b)):
            for r0 in range(0, src.shape[0], CAST_ROWS):
                dst[pl.ds(r0, CAST_ROWS), :] = src[pl.ds(r0, CAST_ROWS), :].astype(BF16)

        @pl.when(has_next_ref[i] == 1)
        def _():
            for c in weight_copies(next_e_ref[i]):
                c.start(priority=WEIGHT_PRIORITY)

    @pl.when(i < n_used)
    def _():
        _wait_row_gather(MOE_BLK, x_hbm, xbuf.at[slot], sem.at[slot])
        n_parts = 2 * (EXPERT_FF // EXPERT_CHUNK) + D_MODEL // EXPERT_OUT_CHUNK
        issue = _interleaved_row_gather(tok2_ref, MOE_BLK, x_hbm, xbuf.at[ahead_slot], sem.at[ahead_slot], n_parts,
                                        ROW_PRIORITY)
        part = 0
        last = None
        xb = _gathered_rows(xbuf.at[slot], 0, MOE_BLK).astype(BF16)
        hid = []
        for c in range(EXPERT_FF // EXPERT_CHUNK):
            cols = slice(c * EXPERT_CHUNK, (c + 1) * EXPERT_CHUNK)
            issue(part, last)
            h1 = jnp.dot(xb, w1b[:, cols], preferred_element_type=F32)
            issue(part + 1, h1)
            h3 = jnp.dot(xb, w3b[:, cols], preferred_element_type=F32)
            part += 2
            last = h3
            hid.append((jax.nn.silu(h1) * h3).astype(BF16))
        hid = jnp.concatenate(hid, axis=1)
        ys = []
        for c in range(D_MODEL // EXPERT_OUT_CHUNK):
            cols = slice(c * EXPERT_OUT_CHUNK, (c + 1) * EXPERT_OUT_CHUNK)
            issue(part, last)
            part += 1
            last = jnp.dot(hid, w2b[:, cols], preferred_element_type=F32)
            ys.append(last)
        _store_gather_rows(o_ref, 0, jnp.concatenate(ys, axis=1))

    @pl.when(i == n_used - 1)
    def _():
        for k in range(1, GATHER_SLOTS):
            other = (i + k) % GATHER_SLOTS
            _wait_row_gather(MOE_BLK, x_hbm, xbuf.at[other], sem.at[other])

    @pl.when(i >= n_used)
    def _():
        o_ref[...] = jnp.zeros_like(o_ref)


def _expert_plan(blk_e, n_used):
    n_blocks = blk_e.shape[0]
    j = jnp.arange(n_blocks, dtype=jnp.int32)
    run_start = ((j == 0) | (blk_e != jnp.roll(blk_e, 1))) & (j < n_used[0])
    start_idx = jnp.where(run_start, j, n_blocks)
    at_or_after = lax.cummin(start_idx, axis=0, reverse=True)
    nxt = jnp.concatenate([at_or_after[1:], jnp.full((1,), n_blocks, jnp.int32)])
    has_next = nxt < n_blocks
    next_e = blk_e[jnp.minimum(nxt, n_blocks - 1)]
    return run_start.astype(jnp.int32), next_e.astype(jnp.int32), has_next.astype(jnp.int32)


def _expert_call(blk_e, n_used, buf_tok, x1g, w1, w3, w2, layer):
    n_blocks = blk_e.shape[0]
    d = D_MODEL
    tok3 = buf_tok.reshape(n_blocks, 1, MOE_BLK)
    run_start, next_e, has_next = _expert_plan(blk_e, n_used)

    def tok_spec(k):
        return pl.BlockSpec((None, 1, MOE_BLK),
                            lambda i, e, u, *_: (jnp.minimum(i + k, jnp.maximum(u[0] - 1, 0)), 0, 0),
                            memory_space=pltpu.SMEM)

    assert GATHER_AHEAD == 2
    grid_spec = pltpu.PrefetchScalarGridSpec(
        num_scalar_prefetch=5,
        grid=(n_blocks,),
        in_specs=[
            tok_spec(0), tok_spec(1), tok_spec(2),
            pl.BlockSpec(memory_space=pl.ANY),
            pl.BlockSpec(memory_space=pl.ANY),
            pl.BlockSpec(memory_space=pl.ANY),
            pl.BlockSpec(memory_space=pl.ANY),
        ],
        out_specs=pl.BlockSpec((MOE_BLK * ROW_PITCH, LANES), lambda i, e, u, *_: (i, 0)),
        scratch_shapes=[pltpu.VMEM((GATHER_SLOTS, MOE_BLK * ROW_PITCH + ANCHOR_ROWS, LANES), U32),
                        pltpu.VMEM((d, EXPERT_FF), F32), pltpu.VMEM((d, EXPERT_FF), F32),
                        pltpu.VMEM((EXPERT_FF, d), F32),
                        pltpu.VMEM((d, EXPERT_FF), BF16), pltpu.VMEM((d, EXPERT_FF), BF16),
                        pltpu.VMEM((EXPERT_FF, d), BF16),
                        pltpu.SemaphoreType.DMA((GATHER_SLOTS,)), pltpu.SemaphoreType.DMA((3,))],
    )
    return pl.pallas_call(
        functools.partial(_expert_kernel, layer=layer),
        grid_spec=grid_spec,
        out_shape=jax.ShapeDtypeStruct((n_blocks * MOE_BLK * ROW_PITCH, LANES), U32),
        compiler_params=_params(("arbitrary",)),
        name="experts",
    )(blk_e, n_used, run_start, next_e, has_next, tok3, tok3, tok3, x1g, w1, w3, w2)


def _final_kernel(pos0_ref, pos1_ref, pos2_ref, x1_ref, p_ref, info_ref, yb_hbm, wpg_ref, wpp_ref, g_ref, b_ref,
                  x2_ref, x2b_ref, ybuf, sem):
    i = pl.program_id(0)
    n_steps = pl.num_programs(0)
    slot = i % GATHER_SLOTS
    ahead_slot = (i + GATHER_AHEAD) % GATHER_SLOTS
    tm = p_ref.shape[0]

    @pl.when(i == 0)
    def _():
        _start_row_gather(pos0_ref, TOP_K * tm, yb_hbm, ybuf.at[0], sem.at[0])
        _start_row_gather(pos1_ref, TOP_K * tm, yb_hbm, ybuf.at[1], sem.at[1])

    _wait_row_gather(TOP_K * tm, yb_hbm, ybuf.at[slot], sem.at[slot])
    n_sub = tm // SUB_ROWS
    n_chunks = D_MODEL // FINAL_CHUNK
    issue = _interleaved_row_gather(pos2_ref, TOP_K * tm, yb_hbm, ybuf.at[ahead_slot], sem.at[ahead_slot],
                                    n_sub * n_chunks)
    last = None
    for sub in range(n_sub):
        rows = pl.ds(sub * SUB_ROWS, SUB_ROWS)
        x1 = x1_ref[rows, :]
        x1b = x1.astype(BF16)
        pb = p_ref[rows, :].astype(BF16)
        pre = []
        for c in range(n_chunks):
            cols = slice(c * FINAL_CHUNK, (c + 1) * FINAL_CHUNK)
            issue(sub * n_chunks + c, last)
            gate = jax.nn.sigmoid(jnp.dot(x1b, wpg_ref[:, cols], preferred_element_type=F32))
            last = gate * jnp.dot(pb, wpp_ref[:, cols], preferred_element_type=F32)
            pre.append(last)
        pre = jnp.concatenate(pre, axis=1)
        info = info_ref[rows, :]
        y = (_gathered_rows(ybuf.at[slot], sub * SUB_ROWS, SUB_ROWS) * info[:, INFO_GATE:INFO_GATE + 1]
             + _gathered_rows(ybuf.at[slot], tm + sub * SUB_ROWS, SUB_ROWS) * info[:, INFO_GATE + 1:INFO_GATE + 2])
        x2 = _layer_norm(ALPHA * x1 + y + pre, g_ref[...], b_ref[...])
        x2_ref[rows, :] = x2
        x2b_ref[rows, :] = x2.astype(BF16)

    @pl.when(i == n_steps - 1)
    def _():
        for k in range(1, GATHER_SLOTS):
            other = (i + k) % GATHER_SLOTS
            _wait_row_gather(TOP_K * tm, yb_hbm, ybuf.at[other], sem.at[other])


def _final_call(pos, info, x1, p, yb, wpg, wpp, g, b):
    n = p.shape[0]
    d = D_MODEL
    tm = min(FINAL_TM, n)
    steps = n // tm
    pos3 = pos.reshape(steps, tm, TOP_K).transpose(0, 2, 1).reshape(steps, 1, TOP_K * tm)
    row = lambda w: pl.BlockSpec((tm, w), lambda i: (i, 0))

    def pos_spec(k):
        return pl.BlockSpec((None, 1, TOP_K * tm), lambda i: (jnp.minimum(i + k, steps - 1), 0, 0),
                            memory_space=pltpu.SMEM)

    assert GATHER_AHEAD == 2
    return pl.pallas_call(
        _final_kernel,
        grid=(steps,),
        in_specs=[
            pos_spec(0), pos_spec(1), pos_spec(2),
            row(d),
            row(PLE_DIM),
            row(LANES),
            pl.BlockSpec(memory_space=pl.ANY),
            _resident(wpg.shape), _resident(wpp.shape), _resident(g.shape), _resident(b.shape),
        ],
        out_specs=[row(d), row(d)],
        out_shape=[jax.ShapeDtypeStruct((n, d), F32), jax.ShapeDtypeStruct((n, d), BF16)],
        scratch_shapes=[pltpu.VMEM((GATHER_SLOTS, TOP_K * tm * ROW_PITCH + ANCHOR_ROWS, LANES), U32),
                        pltpu.SemaphoreType.DMA((GATHER_SLOTS,))],
        compiler_params=_params(("arbitrary",)),
        name="final",
    )(pos3, pos3, pos3, x1, p, info, yb, wpg, wpp, g, b)


INFO_DEST = 0
INFO_GATE = 2
EXPERT_LANE0 = N_EXPERT_GROUPS
ROUTE_TM = 512
META_ROWS = 256


def _lane_min_index(mask, lane_f):
    return jnp.min(jnp.where(mask, lane_f, float(LANES)), axis=1, keepdims=True)


def _route_kernel(logit_ref, tri_ref, info_ref, meta_ref, stash, run, pstart, *, n_blocks):
    phase = pl.program_id(0)
    i = pl.program_id(1)
    tm = logit_ref.shape[0]
    lane = lax.broadcasted_iota(jnp.int32, (tm, LANES), 1)
    lane_f = lane.astype(F32)
    rows = pl.ds(pl.multiple_of(i * tm, tm), tm)

    @pl.when((phase == 0) & (i == 0))
    def _():
        run[...] = jnp.zeros_like(run)

    @pl.when(phase == 0)
    def _():
        x = logit_ref[...]
        xg = jnp.where(lane < N_EXPERT_GROUPS, x, -jnp.inf)
        mg = jnp.max(xg, axis=1, keepdims=True)
        g_sel = _lane_min_index(xg == mg, lane_f)
        p_g = 1.0 / jnp.sum(jnp.exp(xg - mg), axis=1, keepdims=True)
        lo = EXPERT_LANE0 + EXPERTS_PER_GROUP * g_sel
        xe = jnp.where((lane_f >= lo) & (lane_f < lo + EXPERTS_PER_GROUP), x, -jnp.inf)
        v1 = jnp.max(xe, axis=1, keepdims=True)
        i1 = _lane_min_index(xe == v1, lane_f)
        xe2 = jnp.where(lane_f == i1, -jnp.inf, xe)
        v2 = jnp.max(xe2, axis=1, keepdims=True)
        i2 = _lane_min_index(xe2 == v2, lane_f)
        e21 = jnp.exp(v2 - v1)
        den = 1.0 + e21
        gate1 = (1.0 / den) * p_g
        gate2 = (e21 / den) * p_g
        o1 = lane_f == i1
        o2 = lane_f == i2
        both = jnp.where(o1 | o2, 1.0, 0.0)
        before = jnp.dot(tri_ref[...], both.astype(BF16), preferred_element_type=F32) + run[0:1, :]
        rank1 = jnp.sum(jnp.where(o1, before, 0.0), axis=1, keepdims=True)
        rank2 = jnp.sum(jnp.where(o2, before, 0.0), axis=1, keepdims=True)
        run[0:1, :] = run[0:1, :] + jnp.sum(both, axis=0, keepdims=True)
        info = jnp.where(lane == INFO_DEST, rank1, 0.0)
        info = jnp.where(lane == INFO_DEST + 1, rank2, info)
        info = jnp.where(lane == INFO_GATE, gate1, info)
        info = jnp.where(lane == INFO_GATE + 1, gate2, info)
        info = jnp.where(lane == INFO_GATE + 2, i1, info)
        info = jnp.where(lane == INFO_GATE + 3, i2, info)
        stash[rows, :] = info

    @pl.when((phase == 1) & (i == 0))
    def _():
        lane8 = lax.broadcasted_iota(jnp.int32, run.shape, 1)
        counts = jnp.where((lane8 >= EXPERT_LANE0) & (lane8 < EXPERT_LANE0 + N_EXPERTS), run[...], 0.0)
        padded = jnp.floor((counts + (MOE_BLK - 1)) * (1.0 / MOE_BLK)) * MOE_BLK
        pend = padded
        shift = 1
        while shift < LANES:
            pend = pend + jnp.where(lane8 >= shift, pltpu.roll(pend, shift, 1), 0.0)
            shift *= 2
        pstart[...] = pend - padded
        blk_row = lax.broadcasted_iota(jnp.int32, meta_ref.shape, 0)
        lane_m = lax.broadcasted_iota(jnp.int32, meta_ref.shape, 1)
        is_e = (lane_m >= EXPERT_LANE0) & (lane_m < EXPERT_LANE0 + N_EXPERTS)
        ended = is_e & (pend[0:1, :] <= (blk_row * MOE_BLK).astype(F32))
        blk_e = jnp.minimum(jnp.sum(jnp.where(ended, 1.0, 0.0), axis=1, keepdims=True), N_EXPERTS - 1.0)
        n_used = jnp.max(pend[0:1, :], axis=1, keepdims=True) * (1.0 / MOE_BLK)
        meta_ref[...] = jnp.where(blk_row == n_blocks, n_used, blk_e) + jnp.zeros(meta_ref.shape, F32)

    @pl.when(phase == 1)
    def _():
        info = stash[rows, :]
        i1 = info[:, INFO_GATE + 2:INFO_GATE + 3]
        i2 = info[:, INFO_GATE + 3:INFO_GATE + 4]
        start = pstart[0:1, :]
        s1 = jnp.sum(jnp.where(lane_f == i1, start, 0.0), axis=1, keepdims=True)
        s2 = jnp.sum(jnp.where(lane_f == i2, start, 0.0), axis=1, keepdims=True)
        out = jnp.where(lane == INFO_DEST, info + s1, info)
        out = jnp.where(lane == INFO_DEST + 1, info + s2, out)
        info_ref[...] = out


def _route_call(logits, n_blocks):
    n = logits.shape[0]
    tm = min(ROUTE_TM, n)
    assert n_blocks < META_ROWS
    tri = jnp.tril(jnp.ones((tm, tm), F32), k=-1).astype(BF16)
    return pl.pallas_call(
        functools.partial(_route_kernel, n_blocks=n_blocks),
        grid=(2, n // tm),
        in_specs=[
            pl.BlockSpec((tm, LANES), lambda ph, i: (jnp.where(ph == 0, i, n // tm - 1), 0)),
            _resident(tri.shape),
        ],
        out_specs=[
            pl.BlockSpec((tm, LANES), lambda ph, i: (i * ph, 0)),
            pl.BlockSpec((META_ROWS, LANES), lambda ph, i: (0, 0)),
        ],
        out_shape=[jax.ShapeDtypeStruct((n, LANES), F32), jax.ShapeDtypeStruct((META_ROWS, LANES), F32)],
        scratch_shapes=[pltpu.VMEM((n, LANES), F32), pltpu.VMEM((8, LANES), F32), pltpu.VMEM((8, LANES), F32)],
        compiler_params=_params(("arbitrary", "arbitrary")),
        name="route",
    )(logits, tri)


def _route(logits):
    n = logits.shape[0]
    a = n * TOP_K
    n_blocks = (a + N_EXPERTS * (MOE_BLK - 1) + MOE_BLK - 1) // MOE_BLK
    info, meta = _route_call(logits, n_blocks)
    dest = info[:, INFO_DEST:INFO_DEST + TOP_K].astype(jnp.int32)
    blk_e = meta[:n_blocks, 0].astype(jnp.int32)
    n_used = meta[n_blocks:n_blocks + 1, 0].astype(jnp.int32)
    tok = jnp.arange(a, dtype=jnp.int32) // TOP_K
    buf_tok = jnp.zeros((n_blocks * MOE_BLK,), jnp.int32).at[dest.reshape(a)].set(
        tok, unique_indices=True, mode="promise_in_bounds")
    return blk_e, n_used, buf_tok, dest, info


def kernel(x, p, positions, w_in, w_s, b_s, ln_v_g, ln_v_b, w_a, w_b, w_o, ln1_g, ln1_b, w_grp, b_grp, w_rt, b_rt, w1, w3, w2, w_pg, w_pp, ln2_g, ln2_b):
    batch, seq, d = x.shape
    depth = w_in.shape[0]
    n = batch * seq
    assert d == D_MODEL and w_in.shape[2] == PROJ_WIDTH
    cosf, sinf = _rotary_tables(positions)
    causal = jnp.tril(jnp.ones((CHUNK, CHUNK), F32))
    xf = x.reshape(n, d)
    xb = xf.astype(BF16)
    pad_r = ROUTER_LANES - N_EXPERT_GROUPS - N_EXPERTS
    w_pa, w_pb = _split_proj_weights(w_in)

    for i in range(depth):
        proj_a = _proj_a_call(xb, w_pa[i])
        qkv = _proj_b_call(xb, w_pb[i], cosf, sinf)
        ws = (w_s[i] * causal).astype(BF16)
        bs = jnp.repeat(b_s[i].T, SGU_GROUP_CH, axis=1)
        act = _sgu_call(proj_a, ws, bs, ln_v_g[i][None, :], ln_v_b[i][None, :])
        att = [_attn_call(qkv, g, batch, seq) for g in range(N_DIL)]
        wr = jnp.concatenate([w_grp[i], w_rt[i], jnp.zeros((d, pad_r), F32)], axis=1)
        wr_hi = wr.astype(BF16)
        wr_lo = jnp.concatenate([wr_hi, (wr - wr_hi.astype(F32)).astype(BF16)], axis=1)
        br = jnp.concatenate([b_grp[i], b_rt[i], jnp.zeros((pad_r,), F32)])[None, :]
        x1, x1g, logits = _mix_call(xf, act, [a[0] for a in att], [a[1] for a in att], proj_a,
                                w_a[i].astype(BF16), w_b[i].astype(BF16), w_o[i].astype(BF16),
                                ln1_g[i][None, :], ln1_b[i][None, :], wr_hi, wr_lo, br)
        blk_e, n_used, buf_tok, pos, info = _route(logits)
        yb = _expert_call(blk_e, n_used, buf_tok, x1g, w1, w3, w2, i)
        xf, xb = _final_call(pos, info, x1, p[i].reshape(n, PLE_DIM), yb, w_pg[i].astype(BF16), w_pp[i].astype(BF16),
                             ln2_g[i][None, :], ln2_b[i][None, :])
    return xf.reshape(batch, seq, d)
```

```python
import functools

import jax
import jax.numpy as jnp
import numpy as np
from jax import lax
from jax.experimental import pallas as pl
from jax.experimental.pallas import tpu as pltpu

F32 = jnp.float32
BF16 = jnp.bfloat16

D_MODEL = 2048
DEPTH_FOR_DEEPNORM = 4
SGU_WIDTH = 1024
SGU_GROUPS = 8
SGU_GROUP_CH = SGU_WIDTH // SGU_GROUPS
CHUNK = 128
HEAD_DIM = 128
HEADS_PER_GROUP = 4
DILATED_GROUPS = ((128, 1), (512, 4), (2048, 16))
N_DIL = len(DILATED_GROUPS)
ATT_HEADS = HEADS_PER_GROUP * N_DIL
ATT_WIDTH = ATT_HEADS * HEAD_DIM
QBLK = 128
ROT_DIM = HEAD_DIM // 4
ROT_HALF = ROT_DIM // 2
ROPE_THETA = 500000.0
PROJ_WIDTH = 2 * SGU_WIDTH + 3 * ATT_WIDTH + 2 * D_MODEL
N_EXPERT_GROUPS = 4
EXPERTS_PER_GROUP = 8
N_EXPERTS = N_EXPERT_GROUPS * EXPERTS_PER_GROUP
TOP_K = 2
EXPERT_FF = 512
MOE_BLK = 256
PLE_DIM = 256
ALPHA = (2 * DEPTH_FOR_DEEPNORM) ** 0.25
LN_EPS = 1e-5
ROUTER_LANES = 128

LANES = 128
U32 = jnp.uint32
SLABS = D_MODEL // (2 * LANES)
ROW_PITCH = SLABS + 1
HI_MASK = np.uint32(0xFFFF0000)

PROJ_TM = 1024
PROJ_A_TN = 2048
PROJ_B_TN = ATT_WIDTH
PROJ_CHUNK = 256
SGU_TM = 512
MIX_TM = 256
FINAL_TM = 256
SUB_ROWS = 128
GATHER_AHEAD = 2
GATHER_SLOTS = GATHER_AHEAD + 1
ROW_PRIORITY = (0,)
WEIGHT_PRIORITY = 1
CAST_ROWS = 128
EXPERT_CHUNK = 256
EXPERT_OUT_CHUNK = 512
FINAL_CHUNK = 512
ATTN_STEP = {1: (8, 4), 4: (2, 4), 16: (1, 2)}
ATTN_CLASS_UNROLL = 4
VMEM_LIMIT = 56 * 1024 * 1024

_REF_SPLITS = (0, SGU_WIDTH, 2 * SGU_WIDTH, 2 * SGU_WIDTH + ATT_WIDTH, 2 * SGU_WIDTH + 2 * ATT_WIDTH,
               2 * SGU_WIDTH + 3 * ATT_WIDTH, 2 * SGU_WIDTH + 3 * ATT_WIDTH + D_MODEL, PROJ_WIDTH)
_A_GB = D_MODEL
_A_U = 2 * D_MODEL
_A_V = 2 * D_MODEL + SGU_WIDTH
PROJ_A_WIDTH = 2 * D_MODEL + 2 * SGU_WIDTH


def _params(sem, vmem=VMEM_LIMIT):
    return pltpu.CompilerParams(dimension_semantics=sem, vmem_limit_bytes=vmem)


def _resident(shape):
    nd = len(shape)
    return pl.BlockSpec(shape, lambda *_: (0,) * nd, pipeline_mode=pl.Buffered(1))


def _layer_norm(y, g, b):
    mu = jnp.mean(y, axis=-1, keepdims=True)
    yc = y - mu
    var = jnp.mean(yc * yc, axis=-1, keepdims=True)
    return yc * lax.rsqrt(var + LN_EPS) * g + b


def _split_proj_weights(w):
    sec = [w[:, :, _REF_SPLITS[k]:_REF_SPLITS[k + 1]] for k in range(7)]
    wa = jnp.concatenate([sec[5], sec[6], sec[0], sec[1]], axis=2).astype(BF16)
    wb = w[:, :, _REF_SPLITS[2]:_REF_SPLITS[5]].astype(BF16)
    return wa, wb


def _rotary_tables(positions):
    inv_freq = ROPE_THETA ** (-jnp.arange(0, ROT_DIM, 2, dtype=F32) / ROT_DIM)
    ang = positions.astype(F32).reshape(-1)[:, None] * inv_freq
    cos, sin = jnp.cos(ang), jnp.sin(ang)
    n = ang.shape[0]
    cosf = jnp.concatenate([cos, cos, jnp.ones((n, HEAD_DIM - ROT_DIM), F32)], axis=1)
    sinf = jnp.concatenate([-sin, sin, jnp.zeros((n, HEAD_DIM - ROT_DIM), F32)], axis=1)
    return cosf, sinf


def _proj_a_kernel(x_ref, w_ref, o_ref):
    j = pl.program_id(1)
    tn = o_ref.shape[1]

    def run(act):
        for c in range(tn // PROJ_CHUNK):
            cols = slice(c * PROJ_CHUNK, (c + 1) * PROJ_CHUNK)
            acc = jnp.dot(x_ref[...], w_ref[:, cols], preferred_element_type=F32)
            o_ref[:, cols] = act(acc).astype(o_ref.dtype)

    @pl.when(j < _A_U // tn)
    def _():
        run(jax.nn.sigmoid)

    @pl.when(j >= _A_U // tn)
    def _():
        run(jax.nn.gelu)


def _proj_a_call(xb, w):
    n, d = xb.shape
    tm = min(PROJ_TM, n)
    tn = PROJ_A_TN
    assert _A_U % tn == 0 and PROJ_A_WIDTH % tn == 0
    return pl.pallas_call(
        _proj_a_kernel,
        grid=(n // tm, PROJ_A_WIDTH // tn),
        in_specs=[
            pl.BlockSpec((tm, d), lambda i, j: (i, 0)),
            pl.BlockSpec((d, tn), lambda i, j: (0, j)),
        ],
        out_specs=pl.BlockSpec((tm, tn), lambda i, j: (i, j)),
        out_shape=jax.ShapeDtypeStruct((n, PROJ_A_WIDTH), BF16),
        compiler_params=_params(("parallel", "arbitrary")),
        name="proj_a",
    )(xb, w)


def _proj_b_kernel(x_ref, w_ref, cos_ref, sin_ref, o_ref, o0_ref):
    j = pl.program_id(1)
    heads_per_chunk = PROJ_CHUNK // HEAD_DIM
    lane = lax.broadcasted_iota(jnp.int32, cos_ref.shape, 1)

    def run(rotate):
        for c in range(PROJ_B_TN // PROJ_CHUNK):
            cols = slice(c * PROJ_CHUNK, (c + 1) * PROJ_CHUNK)
            acc = jnp.dot(x_ref[...], w_ref[:, cols], preferred_element_type=F32)
            for h in range(heads_per_chunk):
                xh = acc[:, h * HEAD_DIM:(h + 1) * HEAD_DIM]
                if rotate:
                    partner = jnp.where(lane < ROT_HALF, pltpu.roll(xh, HEAD_DIM - ROT_HALF, 1),
                                        pltpu.roll(xh, ROT_HALF, 1))
                    xh = xh * cos_ref[...] + partner * sin_ref[...]
                head = c * heads_per_chunk + h
                if head < HEADS_PER_GROUP:
                    o0_ref[head] = xh.astype(BF16)
                else:
                    o_ref[head - HEADS_PER_GROUP] = xh

    @pl.when(j < 2)
    def _():
        run(True)

    @pl.when(j >= 2)
    def _():
        run(False)


def _proj_b_call(xb, w, cosf, sinf):
    n, d = xb.shape
    tm = min(PROJ_TM, n)
    slabs = PROJ_B_TN // HEAD_DIM - HEADS_PER_GROUP
    return pl.pallas_call(
        _proj_b_kernel,
        grid=(n // tm, 3),
        in_specs=[
            pl.BlockSpec((tm, d), lambda i, j: (i, 0)),
            pl.BlockSpec((d, PROJ_B_TN), lambda i, j: (0, j)),
            pl.BlockSpec((tm, HEAD_DIM), lambda i, j: (i, 0)),
            pl.BlockSpec((tm, HEAD_DIM), lambda i, j: (i, 0)),
        ],
        out_specs=[pl.BlockSpec((slabs, tm, HEAD_DIM), lambda i, j: (j, i, 0)),
                   pl.BlockSpec((HEADS_PER_GROUP, tm, HEAD_DIM), lambda i, j: (j, i, 0))],
        out_shape=[jax.ShapeDtypeStruct((3 * slabs, n, HEAD_DIM), F32),
                   jax.ShapeDtypeStruct((3 * HEADS_PER_GROUP, n, HEAD_DIM), BF16)],
        compiler_params=_params(("parallel", "arbitrary")),
        name="proj_b",
    )(xb, w, cosf, sinf)


def _sgu_kernel(u_ref, v_ref, ws_ref, bs_ref, g_ref, b_ref, o_ref):
    vn = _layer_norm(v_ref[...].astype(F32), g_ref[...], b_ref[...]).astype(BF16)
    tm = o_ref.shape[0]
    for c in range(tm // CHUNK):
        rows = slice(c * CHUNK, (c + 1) * CHUNK)
        for g in range(SGU_GROUPS):
            cols = slice(g * SGU_GROUP_CH, (g + 1) * SGU_GROUP_CH)
            z = jnp.dot(ws_ref[g], vn[rows, cols], preferred_element_type=F32) + bs_ref[:, cols]
            o_ref[rows, cols] = (u_ref[rows, cols].astype(F32) * z).astype(o_ref.dtype)


def _sgu_call(proj_a, ws, bs, ln_g, ln_b):
    n = proj_a.shape[0]
    tm = min(SGU_TM, n)
    u_blk, v_blk = _A_U // SGU_WIDTH, _A_V // SGU_WIDTH
    return pl.pallas_call(
        _sgu_kernel,
        grid=(n // tm,),
        in_specs=[
            pl.BlockSpec((tm, SGU_WIDTH), lambda i: (i, u_blk)),
            pl.BlockSpec((tm, SGU_WIDTH), lambda i: (i, v_blk)),
            _resident(ws.shape),
            _resident(bs.shape),
            _resident(ln_g.shape),
            _resident(ln_b.shape),
        ],
        out_specs=pl.BlockSpec((tm, SGU_WIDTH), lambda i: (i, 0)),
        out_shape=jax.ShapeDtypeStruct((n, SGU_WIDTH), BF16),
        compiler_params=_params(("parallel",)),
        name="sgu",
    )(proj_a, proj_a, ws, bs, ln_g, ln_b)


def _attn_kernel(q_ref, kp_ref, k_ref, vp_ref, v_ref, o_ref, lse_ref, *, dil, span, nq, hps):
    slab_idx = pl.program_id(1)
    hc = pl.program_id(2)
    qi = lax.broadcasted_iota(jnp.int32, (QBLK, QBLK), 0)
    kj = lax.broadcasted_iota(jnp.int32, (QBLK, QBLK), 1)
    dist_prev = qi + QBLK - kj
    dist_cur = qi - kj
    in_prev = (dist_prev >= 0) & (dist_prev <= span)
    mask_cur = (dist_cur >= 0) & (dist_cur <= span)
    scale = HEAD_DIM ** -0.5
    nt = (((1,), (1,)), ((), ()))
    lane = lax.broadcasted_iota(jnp.int32, (QBLK, LANES), 1)
    ones = jnp.ones((QBLK, HEAD_DIM), BF16)

    @pl.when(hc == 0)
    def _():
        lse_ref[...] = jnp.zeros_like(lse_ref)

    def rows(r, blk):
        return pl.ds(blk * QBLK * dil + r, QBLK, stride=dil) if dil > 1 else pl.ds(blk * QBLK, QBLK)

    def one_class(r):
        for blk in range(nq):
            if blk == 0:
                mask_prev = in_prev & (slab_idx > 0)
            else:
                mask_prev = in_prev
            sel = rows(r, blk)
            lse_rows = lse_ref[sel, :]
            for h in range(hps):
                q = (q_ref.at[h][sel, :].astype(F32) * scale).astype(BF16)
                if blk == 0:
                    kp = kp_ref.at[h][rows(r, 0), :].astype(BF16)
                    vp = vp_ref.at[h][rows(r, 0), :].astype(BF16)
                else:
                    kp = k_ref.at[h][rows(r, blk - 1), :].astype(BF16)
                    vp = v_ref.at[h][rows(r, blk - 1), :].astype(BF16)
                kc = k_ref.at[h][sel, :].astype(BF16)
                vc = v_ref.at[h][sel, :].astype(BF16)
                sp = jnp.where(mask_prev, lax.dot_general(q, kp, nt, preferred_element_type=F32), -jnp.inf)
                sc = jnp.where(mask_cur, lax.dot_general(q, kc, nt, preferred_element_type=F32), -jnp.inf)
                m = jnp.max(jnp.maximum(sp, sc), axis=-1, keepdims=True)
                ep = jnp.exp((sp - m).astype(BF16))
                ec = jnp.exp((sc - m).astype(BF16))
                od = jnp.dot(ep, jnp.concatenate([vp, ones], axis=1), preferred_element_type=F32)
                od = od + jnp.dot(ec, jnp.concatenate([vc, ones], axis=1), preferred_element_type=F32)
                den = od[:, HEAD_DIM:]
                o_ref.at[h][sel, :] = od[:, :HEAD_DIM] / den
                lse_rows = jnp.where(lane == hc * hps + h, m + jnp.log(den), lse_rows)
            lse_ref[sel, :] = lse_rows

    if dil == 1:
        one_class(0)
    else:
        def body(r2, carry):
            for u in range(ATTN_CLASS_UNROLL):
                one_class(r2 * ATTN_CLASS_UNROLL + u)
            return carry
        lax.fori_loop(0, dil // ATTN_CLASS_UNROLL, body, 0)


def _attn_call(qkv_dilated, qkv_plain, group, batch, seq):
    window, dil = DILATED_GROUPS[group]
    if dil == 1:
        qkv, heads_per_section, first_head = qkv_plain, HEADS_PER_GROUP, 0
    else:
        qkv, heads_per_section, first_head = qkv_dilated, ATT_HEADS - HEADS_PER_GROUP, (group - 1) * HEADS_PER_GROUP
    span = window // dil
    nq, hps = ATTN_STEP[dil]
    slab = nq * QBLK * dil
    prev = QBLK * dil
    assert seq % slab == 0 and span <= QBLK and HEADS_PER_GROUP % hps == 0
    slabs_per_seq = seq // slab
    prev_per_seq = seq // prev
    n = batch * seq
    hblocks = HEADS_PER_GROUP // hps

    def cur(section):
        base = (section * heads_per_section + first_head) // hps
        return pl.BlockSpec((hps, slab, LANES), lambda b, s, hc: (base + hc, b * slabs_per_seq + s, 0))

    def before(section):
        base = (section * heads_per_section + first_head) // hps
        return pl.BlockSpec(
            (hps, prev, LANES),
            lambda b, s, hc: (base + hc, b * prev_per_seq + jnp.maximum(s * nq - 1, 0), 0))

    return pl.pallas_call(
        functools.partial(_attn_kernel, dil=dil, span=span, nq=nq, hps=hps),
        grid=(batch, slabs_per_seq, hblocks),
        in_specs=[cur(0), before(1), cur(1), before(2), cur(2)],
        out_specs=[
            pl.BlockSpec((hps, slab, LANES), lambda b, s, hc: (hc, b * slabs_per_seq + s, 0)),
            pl.BlockSpec((slab, LANES), lambda b, s, hc: (b * slabs_per_seq + s, 0)),
        ],
        out_shape=[
            jax.ShapeDtypeStruct((HEADS_PER_GROUP, n, LANES), F32),
            jax.ShapeDtypeStruct((n, LANES), F32),
        ],
        compiler_params=_params(("parallel", "arbitrary", "arbitrary")),
        name=f"attn_d{dil}",
    )(qkv, qkv, qkv, qkv, qkv)


def _mix_kernel(x_ref, act_ref, o0_ref, o1_ref, o2_ref, l0_ref, l1_ref, l2_ref, sga_ref, sgb_ref,
                wa_ref, wb_ref, wo_ref, g_ref, b_ref, wrh_ref, wrl_ref, br_ref, x1_ref, x1g_ref, logit_ref):
    tm = x_ref.shape[0]
    o_refs = (o0_ref, o1_ref, o2_ref)
    l_refs = (l0_ref, l1_ref, l2_ref)
    for sub in range(tm // SUB_ROWS):
        rows = pl.ds(sub * SUB_ROWS, SUB_ROWS)
        a_out = jnp.dot(act_ref[rows, :], wa_ref[...], preferred_element_type=F32)
        lses = [l[rows, :] for l in l_refs]
        heads = []
        for h in range(HEADS_PER_GROUP):
            lh = [l[:, h:h + 1] for l in lses]
            m = jnp.maximum(jnp.maximum(lh[0], lh[1]), lh[2])
            e = [jnp.exp(v - m) for v in lh]
            tot = e[0] + e[1] + e[2]
            acc = (e[0] / tot) * o_refs[0][h, rows, :]
            acc = acc + (e[1] / tot) * o_refs[1][h, rows, :]
            acc = acc + (e[2] / tot) * o_refs[2][h, rows, :]
            heads.append(acc.astype(BF16))
        merged = jnp.concatenate(heads, axis=1)
        b_out = jnp.dot(merged, wb_ref[...], preferred_element_type=F32)
        mixed_in = sga_ref[rows, :].astype(F32) * a_out + sgb_ref[rows, :].astype(F32) * b_out
        mixed = jnp.dot(mixed_in.astype(BF16), wo_ref[...], preferred_element_type=F32)
        x1 = _layer_norm(ALPHA * x_ref[rows, :] + mixed, g_ref[...], b_ref[...])
        x1_ref[rows, :] = x1
        _store_gather_rows(x1g_ref, sub * SUB_ROWS, x1)
        x_hi = x1.astype(BF16)
        x_lo = (x1 - x_hi.astype(F32)).astype(BF16)
        both = jnp.dot(x_hi, wrl_ref[...], preferred_element_type=F32)
        logits = both[:, :ROUTER_LANES] + jnp.dot(x_lo, wrh_ref[...], preferred_element_type=F32)
        logits = logits + both[:, ROUTER_LANES:]
        logit_ref[rows, :] = logits + br_ref[...]


def _mix_call(x, act, os_, lses, proj_a, wa, wb, wo, g, b, wrh, wrl, br):
    n = x.shape[0]
    tm = min(MIX_TM, n)
    row = lambda w: pl.BlockSpec((tm, w), lambda i: (i, 0))
    o_spec = pl.BlockSpec((HEADS_PER_GROUP, tm, LANES), lambda i: (0, i, 0))
    return pl.pallas_call(
        _mix_kernel,
        grid=(n // tm,),
        in_specs=[
            row(D_MODEL), row(SGU_WIDTH),
            o_spec, o_spec, o_spec,
            row(LANES), row(LANES), row(LANES),
            pl.BlockSpec((tm, D_MODEL), lambda i: (i, 0)),
            pl.BlockSpec((tm, D_MODEL), lambda i: (i, _A_GB // D_MODEL)),
            _resident(wa.shape), _resident(wb.shape), _resident(wo.shape),
            _resident(g.shape), _resident(b.shape),
            _resident(wrh.shape), _resident(wrl.shape), _resident(br.shape),
        ],
        out_specs=[row(D_MODEL), pl.BlockSpec((tm * ROW_PITCH, LANES), lambda i: (i, 0)), row(ROUTER_LANES)],
        out_shape=[jax.ShapeDtypeStruct((n, D_MODEL), F32), jax.ShapeDtypeStruct((n * ROW_PITCH, LANES), U32),
                   jax.ShapeDtypeStruct((n, ROUTER_LANES), F32)],
        compiler_params=_params(("parallel",)),
        name="mix",
    )(x, act, *os_, *lses, proj_a, proj_a, wa, wb, wo, g, b, wrh, wrl, br)


def _start_row(idx_ref, r, src_hbm, dst, sem, priority):
    src = src_hbm.at[pl.ds(idx_ref[0, r] * ROW_PITCH, SLABS)]
    pltpu.make_async_copy(src, dst.at[pl.ds(r * ROW_PITCH, SLABS)], sem).start(priority=priority)


def _start_row_gather(idx_ref, n_rows, src_hbm, dst, sem, priorities=(0, 1)):
    unroll = 8
    def body(blk, carry):
        for j in range(unroll):
            _start_row(idx_ref, blk * unroll + j, src_hbm, dst, sem, priorities[j % len(priorities)])
        return carry
    lax.fori_loop(0, n_rows // unroll, body, 0)


ANCHOR_ROWS = 8


def _interleaved_row_gather(idx_ref, n_rows, src_hbm, dst, sem, n_parts, priorities=(0, 1)):
    per = n_rows // n_parts
    def issue(part, after=None):
        if after is not None:
            dst[pl.ds(n_rows * ROW_PITCH, ANCHOR_ROWS), :] = pltpu.bitcast(after[0:ANCHOR_ROWS, 0:LANES], U32)
        for r in range(part * per, (part + 1) * per):
            _start_row(idx_ref, r, src_hbm, dst, sem, priorities[r % len(priorities)])
    return issue


def _wait_row_gather(n_rows, src_hbm, dst, sem):
    pltpu.make_async_copy(src_hbm.at[pl.ds(0, n_rows * SLABS)], dst.at[pl.ds(0, n_rows * SLABS)], sem).wait()


def _gathered_rows(buf, first_row, n_rows):
    lo, hi = [], []
    for k in range(SLABS):
        w = buf[pl.ds(first_row * ROW_PITCH + k, n_rows, stride=ROW_PITCH), :]
        lo.append(pltpu.bitcast(w << 16, F32))
        hi.append(pltpu.bitcast(w & HI_MASK, F32))
    return jnp.concatenate(lo + hi, axis=1)


def _pack_words(val, k):
    half = D_MODEL // 2
    lo = val[:, k * LANES:(k + 1) * LANES].astype(BF16).astype(F32)
    hi = val[:, half + k * LANES:half + (k + 1) * LANES].astype(BF16).astype(F32)
    return (pltpu.bitcast(lo, U32) >> 16) | pltpu.bitcast(hi, U32)


def _store_gather_rows(ref, first_row, val):
    n_rows = val.shape[0]
    for k in range(SLABS):
        ref[pl.ds(first_row * ROW_PITCH + k, n_rows, stride=ROW_PITCH), :] = _pack_words(val, k)
    ref[pl.ds(first_row * ROW_PITCH + SLABS, n_rows, stride=ROW_PITCH), :] = jnp.zeros((n_rows, LANES), U32)


def _expert_kernel(blk_e_ref, n_used_ref, run_start_ref, next_e_ref, has_next_ref,
                   tok0_ref, tok1_ref, tok2_ref, x_hbm, w1_hbm, w3_hbm, w2_hbm,
                   o_ref, xbuf, w1s, w3s, w2s, w1b, w3b, w2b, sem, wsem, *, layer):
    i = pl.program_id(0)
    n_used = n_used_ref[0]
    slot = i % GATHER_SLOTS
    ahead_slot = (i + GATHER_AHEAD) % GATHER_SLOTS

    def weight_copies(e):
        return (pltpu.make_async_copy(w1_hbm.at[layer, e], w1s, wsem.at[0]),
                pltpu.make_async_copy(w3_hbm.at[layer, e], w3s, wsem.at[1]),
                pltpu.make_async_copy(w2_hbm.at[layer, e], w2s, wsem.at[2]))

    @pl.when((i == 0) & (n_used > 0))
    def _():
        _start_row_gather(tok0_ref, MOE_BLK, x_hbm, xbuf.at[0], sem.at[0], ROW_PRIORITY)
        _start_row_gather(tok1_ref, MOE_BLK, x_hbm, xbuf.at[1], sem.at[1], ROW_PRIORITY)
        for c in weight_copies(blk_e_ref[0]):
            c.start(priority=WEIGHT_PRIORITY)

    @pl.when((i < n_used) & (run_start_ref[i] == 1))
    def _():
        for c in weight_copies(blk_e_ref[i]):
            c.wait()
        for src, dst in ((w1s, w1b), (w3s, w3b), (w2s, w2b)):
            for r0 in range(0, src.shape[0], CAST_ROWS):
                dst[pl.ds(r0, CAST_ROWS), :] = src[pl.ds(r0, CAST_ROWS), :].astype(BF16)

        @pl.when(has_next_ref[i] == 1)
        def _():
            for c in weight_copies(next_e_ref[i]):
                c.start(priority=WEIGHT_PRIORITY)

    @pl.when(i < n_used)
    def _():
        _wait_row_gather(MOE_BLK, x_hbm, xbuf.at[slot], sem.at[slot])
        n_parts = 2 * (EXPERT_FF // EXPERT_CHUNK) + D_MODEL // EXPERT_OUT_CHUNK
        issue = _interleaved_row_gather(tok2_ref, MOE_BLK, x_hbm, xbuf.at[ahead_slot], sem.at[ahead_slot], n_parts,
                                        ROW_PRIORITY)
        part = 0
        last = None
        xb = _gathered_rows(xbuf.at[slot], 0, MOE_BLK).astype(BF16)
        hid = []
        for c in range(EXPERT_FF // EXPERT_CHUNK):
            cols = slice(c * EXPERT_CHUNK, (c + 1) * EXPERT_CHUNK)
            issue(part, last)
            h1 = jnp.dot(xb, w1b[:, cols], preferred_element_type=F32)
            issue(part + 1, h1)
            h3 = jnp.dot(xb, w3b[:, cols], preferred_element_type=F32)
            part += 2
            last = h3
            hid.append((jax.nn.silu(h1) * h3).astype(BF16))
        hid = jnp.concatenate(hid, axis=1)
        ys = []
        for c in range(D_MODEL // EXPERT_OUT_CHUNK):
            cols = slice(c * EXPERT_OUT_CHUNK, (c + 1) * EXPERT_OUT_CHUNK)
            issue(part, last)
            part += 1
            last = jnp.dot(hid, w2b[:, cols], preferred_element_type=F32)
            ys.append(last)
        _store_gather_rows(o_ref, 0, jnp.concatenate(ys, axis=1))

    @pl.when(i == n_used - 1)
    def _():
        for k in range(1, GATHER_SLOTS):
            other = (i + k) % GATHER_SLOTS
            _wait_row_gather(MOE_BLK, x_hbm, xbuf.at[other], sem.at[other])

    @pl.when(i >= n_used)
    def _():
        o_ref[...] = jnp.zeros_like(o_ref)


def _expert_plan(blk_e, n_used):
    n_blocks = blk_e.shape[0]
    j = jnp.arange(n_blocks, dtype=jnp.int32)
    run_start = ((j == 0) | (blk_e != jnp.roll(blk_e, 1))) & (j < n_used[0])
    start_idx = jnp.where(run_start, j, n_blocks)
    at_or_after = lax.cummin(start_idx, axis=0, reverse=True)
    nxt = jnp.concatenate([at_or_after[1:], jnp.full((1,), n_blocks, jnp.int32)])
    has_next = nxt < n_blocks
    next_e = blk_e[jnp.minimum(nxt, n_blocks - 1)]
    return run_start.astype(jnp.int32), next_e.astype(jnp.int32), has_next.astype(jnp.int32)


def _expert_call(blk_e, n_used, buf_tok, x1g, w1, w3, w2, layer):
    n_blocks = blk_e.shape[0]
    d = D_MODEL
    tok3 = buf_tok.reshape(n_blocks, 1, MOE_BLK)
    run_start, next_e, has_next = _expert_plan(blk_e, n_used)

    def tok_spec(k):
        return pl.BlockSpec((None, 1, MOE_BLK),
                            lambda i, e, u, *_: (jnp.minimum(i + k, jnp.maximum(u[0] - 1, 0)), 0, 0),
                            memory_space=pltpu.SMEM)

    assert GATHER_AHEAD == 2
    grid_spec = pltpu.PrefetchScalarGridSpec(
        num_scalar_prefetch=5,
        grid=(n_blocks,),
        in_specs=[
            tok_spec(0), tok_spec(1), tok_spec(2),
            pl.BlockSpec(memory_space=pl.ANY),
            pl.BlockSpec(memory_space=pl.ANY),
            pl.BlockSpec(memory_space=pl.ANY),
            pl.BlockSpec(memory_space=pl.ANY),
        ],
        out_specs=pl.BlockSpec((MOE_BLK * ROW_PITCH, LANES), lambda i, e, u, *_: (i, 0)),
        scratch_shapes=[pltpu.VMEM((GATHER_SLOTS, MOE_BLK * ROW_PITCH + ANCHOR_ROWS, LANES), U32),
                        pltpu.VMEM((d, EXPERT_FF), F32), pltpu.VMEM((d, EXPERT_FF), F32),
                        pltpu.VMEM((EXPERT_FF, d), F32),
                        pltpu.VMEM((d, EXPERT_FF), BF16), pltpu.VMEM((d, EXPERT_FF), BF16),
                        pltpu.VMEM((EXPERT_FF, d), BF16),
                        pltpu.SemaphoreType.DMA((GATHER_SLOTS,)), pltpu.SemaphoreType.DMA((3,))],
    )
    return pl.pallas_call(
        functools.partial(_expert_kernel, layer=layer),
        grid_spec=grid_spec,
        out_shape=jax.ShapeDtypeStruct((n_blocks * MOE_BLK * ROW_PITCH, LANES), U32),
        compiler_params=_params(("arbitrary",)),
        name="experts",
    )(blk_e, n_used, run_start, next_e, has_next, tok3, tok3, tok3, x1g, w1, w3, w2)


def _final_kernel(pos0_ref, pos1_ref, pos2_ref, x1_ref, p_ref, info_ref, yb_hbm, wpg_ref, wpp_ref, g_ref, b_ref,
                  x2_ref, x2b_ref, ybuf, sem):
    i = pl.program_id(0)
    n_steps = pl.num_programs(0)
    slot = i % GATHER_SLOTS
    ahead_slot = (i + GATHER_AHEAD) % GATHER_SLOTS
    tm = p_ref.shape[0]

    @pl.when(i == 0)
    def _():
        _start_row_gather(pos0_ref, TOP_K * tm, yb_hbm, ybuf.at[0], sem.at[0])
        _start_row_gather(pos1_ref, TOP_K * tm, yb_hbm, ybuf.at[1], sem.at[1])

    _wait_row_gather(TOP_K * tm, yb_hbm, ybuf.at[slot], sem.at[slot])
    n_sub = tm // SUB_ROWS
    n_chunks = D_MODEL // FINAL_CHUNK
    issue = _interleaved_row_gather(pos2_ref, TOP_K * tm, yb_hbm, ybuf.at[ahead_slot], sem.at[ahead_slot],
                                    n_sub * n_chunks)
    last = None
    for sub in range(n_sub):
        rows = pl.ds(sub * SUB_ROWS, SUB_ROWS)
        x1 = x1_ref[rows, :]
        x1b = x1.astype(BF16)
        pb = p_ref[rows, :].astype(BF16)
        pre = []
        for c in range(n_chunks):
            cols = slice(c * FINAL_CHUNK, (c + 1) * FINAL_CHUNK)
            issue(sub * n_chunks + c, last)
            gate = jax.nn.sigmoid(jnp.dot(x1b, wpg_ref[:, cols], preferred_element_type=F32))
            last = gate * jnp.dot(pb, wpp_ref[:, cols], preferred_element_type=F32)
            pre.append(last)
        pre = jnp.concatenate(pre, axis=1)
        info = info_ref[rows, :]
        y = (_gathered_rows(ybuf.at[slot], sub * SUB_ROWS, SUB_ROWS) * info[:, INFO_GATE:INFO_GATE + 1]
             + _gathered_rows(ybuf.at[slot], tm + sub * SUB_ROWS, SUB_ROWS) * info[:, INFO_GATE + 1:INFO_GATE + 2])
        x2 = _layer_norm(ALPHA * x1 + y + pre, g_ref[...], b_ref[...])
        x2_ref[rows, :] = x2
        x2b_ref[rows, :] = x2.astype(BF16)

    @pl.when(i == n_steps - 1)
    def _():
        for k in range(1, GATHER_SLOTS):
            other = (i + k) % GATHER_SLOTS
            _wait_row_gather(TOP_K * tm, yb_hbm, ybuf.at[other], sem.at[other])


def _final_call(pos, info, x1, p, yb, wpg, wpp, g, b):
    n = p.shape[0]
    d = D_MODEL
    tm = min(FINAL_TM, n)
    steps = n // tm
    pos3 = pos.reshape(steps, tm, TOP_K).transpose(0, 2, 1).reshape(steps, 1, TOP_K * tm)
    row = lambda w: pl.BlockSpec((tm, w), lambda i: (i, 0))

    def pos_spec(k):
        return pl.BlockSpec((None, 1, TOP_K * tm), lambda i: (jnp.minimum(i + k, steps - 1), 0, 0),
                            memory_space=pltpu.SMEM)

    assert GATHER_AHEAD == 2
    return pl.pallas_call(
        _final_kernel,
        grid=(steps,),
        in_specs=[
            pos_spec(0), pos_spec(1), pos_spec(2),
            row(d),
            row(PLE_DIM),
            row(LANES),
            pl.BlockSpec(memory_space=pl.ANY),
            _resident(wpg.shape), _resident(wpp.shape), _resident(g.shape), _resident(b.shape),
        ],
        out_specs=[row(d), row(d)],
        out_shape=[jax.ShapeDtypeStruct((n, d), F32), jax.ShapeDtypeStruct((n, d), BF16)],
        scratch_shapes=[pltpu.VMEM((GATHER_SLOTS, TOP_K * tm * ROW_PITCH + ANCHOR_ROWS, LANES), U32),
                        pltpu.SemaphoreType.DMA((GATHER_SLOTS,))],
        compiler_params=_params(("arbitrary",)),
        name="final",
    )(pos3, pos3, pos3, x1, p, info, yb, wpg, wpp, g, b)


INFO_DEST = 0
INFO_GATE = 2
EXPERT_LANE0 = N_EXPERT_GROUPS
ROUTE_TM = 512
META_ROWS = 256


def _lane_min_index(mask, lane_f):
    return jnp.min(jnp.where(mask, lane_f, float(LANES)), axis=1, keepdims=True)


def _route_kernel(logit_ref, tri_ref, info_ref, meta_ref, stash, run, pstart, *, n_blocks):
    phase = pl.program_id(0)
    i = pl.program_id(1)
    tm = logit_ref.shape[0]
    lane = lax.broadcasted_iota(jnp.int32, (tm, LANES), 1)
    lane_f = lane.astype(F32)
    rows = pl.ds(pl.multiple_of(i * tm, tm), tm)

    @pl.when((phase == 0) & (i == 0))
    def _():
        run[...] = jnp.zeros_like(run)

    @pl.when(phase == 0)
    def _():
        x = logit_ref[...]
        xg = jnp.where(lane < N_EXPERT_GROUPS, x, -jnp.inf)
        mg = jnp.max(xg, axis=1, keepdims=True)
        g_sel = _lane_min_index(xg == mg, lane_f)
        p_g = 1.0 / jnp.sum(jnp.exp(xg - mg), axis=1, keepdims=True)
        lo = EXPERT_LANE0 + EXPERTS_PER_GROUP * g_sel
        xe = jnp.where((lane_f >= lo) & (lane_f < lo + EXPERTS_PER_GROUP), x, -jnp.inf)
        v1 = jnp.max(xe, axis=1, keepdims=True)
        i1 = _lane_min_index(xe == v1, lane_f)
        xe2 = jnp.where(lane_f == i1, -jnp.inf, xe)
        v2 = jnp.max(xe2, axis=1, keepdims=True)
        i2 = _lane_min_index(xe2 == v2, lane_f)
        e21 = jnp.exp(v2 - v1)
        den = 1.0 + e21
        gate1 = (1.0 / den) * p_g
        gate2 = (e21 / den) * p_g
        o1 = lane_f == i1
        o2 = lane_f == i2
        both = jnp.where(o1 | o2, 1.0, 0.0)
        before = jnp.dot(tri_ref[...], both.astype(BF16), preferred_element_type=F32) + run[0:1, :]
        rank1 = jnp.sum(jnp.where(o1, before, 0.0), axis=1, keepdims=True)
        rank2 = jnp.sum(jnp.where(o2, before, 0.0), axis=1, keepdims=True)
        run[0:1, :] = run[0:1, :] + jnp.sum(both, axis=0, keepdims=True)
        info = jnp.where(lane == INFO_DEST, rank1, 0.0)
        info = jnp.where(lane == INFO_DEST + 1, rank2, info)
        info = jnp.where(lane == INFO_GATE, gate1, info)
        info = jnp.where(lane == INFO_GATE + 1, gate2, info)
        info = jnp.where(lane == INFO_GATE + 2, i1, info)
        info = jnp.where(lane == INFO_GATE + 3, i2, info)
        stash[rows, :] = info

    @pl.when((phase == 1) & (i == 0))
    def _():
        lane8 = lax.broadcasted_iota(jnp.int32, run.shape, 1)
        counts = jnp.where((lane8 >= EXPERT_LANE0) & (lane8 < EXPERT_LANE0 + N_EXPERTS), run[...], 0.0)
        padded = jnp.floor((counts + (MOE_BLK - 1)) * (1.0 / MOE_BLK)) * MOE_BLK
        pend = padded
        shift = 1
        while shift < LANES:
            pend = pend + jnp.where(lane8 >= shift, pltpu.roll(pend, shift, 1), 0.0)
            shift *= 2
        pstart[...] = pend - padded
        blk_row = lax.broadcasted_iota(jnp.int32, meta_ref.shape, 0)
        lane_m = lax.broadcasted_iota(jnp.int32, meta_ref.shape, 1)
        is_e = (lane_m >= EXPERT_LANE0) & (lane_m < EXPERT_LANE0 + N_EXPERTS)
        ended = is_e & (pend[0:1, :] <= (blk_row * MOE_BLK).astype(F32))
        blk_e = jnp.minimum(jnp.sum(jnp.where(ended, 1.0, 0.0), axis=1, keepdims=True), N_EXPERTS - 1.0)
        n_used = jnp.max(pend[0:1, :], axis=1, keepdims=True) * (1.0 / MOE_BLK)
        meta_ref[...] = jnp.where(blk_row == n_blocks, n_used, blk_e) + jnp.zeros(meta_ref.shape, F32)

    @pl.when(phase == 1)
    def _():
        info = stash[rows, :]
        i1 = info[:, INFO_GATE + 2:INFO_GATE + 3]
        i2 = info[:, INFO_GATE + 3:INFO_GATE + 4]
        start = pstart[0:1, :]
        s1 = jnp.sum(jnp.where(lane_f == i1, start, 0.0), axis=1, keepdims=True)
        s2 = jnp.sum(jnp.where(lane_f == i2, start, 0.0), axis=1, keepdims=True)
        out = jnp.where(lane == INFO_DEST, info + s1, info)
        out = jnp.where(lane == INFO_DEST + 1, info + s2, out)
        info_ref[...] = out


def _route_call(logits, n_blocks):
    n = logits.shape[0]
    tm = min(ROUTE_TM, n)
    assert n_blocks < META_ROWS
    tri = jnp.tril(jnp.ones((tm, tm), F32), k=-1).astype(BF16)
    return pl.pallas_call(
        functools.partial(_route_kernel, n_blocks=n_blocks),
        grid=(2, n // tm),
        in_specs=[
            pl.BlockSpec((tm, LANES), lambda ph, i: (jnp.where(ph == 0, i, n // tm - 1), 0)),
            _resident(tri.shape),
        ],
        out_specs=[
            pl.BlockSpec((tm, LANES), lambda ph, i: (i * ph, 0)),
            pl.BlockSpec((META_ROWS, LANES), lambda ph, i: (0, 0)),
        ],
        out_shape=[jax.ShapeDtypeStruct((n, LANES), F32), jax.ShapeDtypeStruct((META_ROWS, LANES), F32)],
        scratch_shapes=[pltpu.VMEM((n, LANES), F32), pltpu.VMEM((8, LANES), F32), pltpu.VMEM((8, LANES), F32)],
        compiler_params=_params(("arbitrary", "arbitrary")),
        name="route",
    )(logits, tri)


def _route(logits):
    n = logits.shape[0]
    a = n * TOP_K
    n_blocks = (a + N_EXPERTS * (MOE_BLK - 1) + MOE_BLK - 1) // MOE_BLK
    info, meta = _route_call(logits, n_blocks)
    dest = info[:, INFO_DEST:INFO_DEST + TOP_K].astype(jnp.int32)
    blk_e = meta[:n_blocks, 0].astype(jnp.int32)
    n_used = meta[n_blocks:n_blocks + 1, 0].astype(jnp.int32)
    tok = jnp.arange(a, dtype=jnp.int32) // TOP_K
    buf_tok = jnp.zeros((n_blocks * MOE_BLK,), jnp.int32).at[dest.reshape(a)].set(
        tok, unique_indices=True, mode="promise_in_bounds")
    return blk_e, n_used, buf_tok, dest, info


def kernel(x, p, positions, w_in, w_s, b_s, ln_v_g, ln_v_b, w_a, w_b, w_o, ln1_g, ln1_b, w_grp, b_grp, w_rt, b_rt, w1, w3, w2, w_pg, w_pp, ln2_g, ln2_b):
    batch, seq, d = x.shape
    depth = w_in.shape[0]
    n = batch * seq
    assert d == D_MODEL and w_in.shape[2] == PROJ_WIDTH
    cosf, sinf = _rotary_tables(positions)
    causal = jnp.tril(jnp.ones((CHUNK, CHUNK), F32))
    xf = x.reshape(n, d)
    xb = xf.astype(BF16)
    pad_r = ROUTER_LANES - N_EXPERT_GROUPS - N_EXPERTS
    w_pa, w_pb = _split_proj_weights(w_in)

    for i in range(depth):
        proj_a = _proj_a_call(xb, w_pa[i])
        qkv_dilated, qkv_plain = _proj_b_call(xb, w_pb[i], cosf, sinf)
        ws = (w_s[i] * causal).astype(BF16)
        bs = jnp.repeat(b_s[i].T, SGU_GROUP_CH, axis=1)
        act = _sgu_call(proj_a, ws, bs, ln_v_g[i][None, :], ln_v_b[i][None, :])
        att = [_attn_call(qkv_dilated, qkv_plain, g, batch, seq) for g in range(N_DIL)]
        wr = jnp.concatenate([w_grp[i], w_rt[i], jnp.zeros((d, pad_r), F32)], axis=1)
        wr_hi = wr.astype(BF16)
        wr_lo = jnp.concatenate([wr_hi, (wr - wr_hi.astype(F32)).astype(BF16)], axis=1)
        br = jnp.concatenate([b_grp[i], b_rt[i], jnp.zeros((pad_r,), F32)])[None, :]
        x1, x1g, logits = _mix_call(xf, act, [a[0] for a in att], [a[1] for a in att], proj_a,
                                w_a[i].astype(BF16), w_b[i].astype(BF16), w_o[i].astype(BF16),
                                ln1_g[i][None, :], ln1_b[i][None, :], wr_hi, wr_lo, br)
        blk_e, n_used, buf_tok, pos, info = _route(logits)
        yb = _expert_call(blk_e, n_used, buf_tok, x1g, w1, w3, w2, i)
        xf, xb = _final_call(pos, info, x1, p[i].reshape(n, PLE_DIM), yb, w_pg[i].astype(BF16), w_pp[i].astype(BF16),
                             ln2_g[i][None, :], ln2_b[i][None, :])
    return xf.reshape(batch, seq, d)
```
